```python
import math
import jax
import jax.numpy as jnp
from jax import lax
import numpy as np

D_MODEL = 1024
BATCH = 16
SEQ = 256
DEPTH = 4
DEC_BATCH = 2
DEC_SEQ = 4096
PAST_LEN = 512

GRID_W = 64
N_MIXERS = 3
CHUNK = 64
SHORT_CONV = 3
NORM_EPS = 1e-6
D_FF = 2816

SSM_DI = 2 * D_MODEL
SSM_HEADDIM = 64
SSM_HEADS = SSM_DI // SSM_HEADDIM
SSM_GROUPS = 8
SSM_HPG = SSM_HEADS // SSM_GROUPS
SSM_STATE = 128
SSM_GN = SSM_GROUPS * SSM_STATE
SSM_CONV_DIM = SSM_DI + 2 * SSM_GN
SSM_PROJ = SSM_DI + SSM_CONV_DIM + 2 * SSM_HEADS

GLA_HEADS = 4
GLA_K = D_MODEL // 2
GLA_V = D_MODEL
GLA_DK = GLA_K // GLA_HEADS
GLA_DV = GLA_V // GLA_HEADS
GLA_RANK = 16
GLA_NORMALIZER = 16.0
GLA_PROJ = 2 * GLA_K + 2 * GLA_V

GDN_HEADS = 8
GDN_DK = 128
GDN_DV = 256
GDN_K = GDN_HEADS * GDN_DK
GDN_V = GDN_HEADS * GDN_DV
GDN_CONV_DIM = 2 * GDN_K + GDN_V
GDN_PROJ = GDN_CONV_DIM + GDN_V + 4 * GDN_HEADS

N_SSM_LAYERS = (DEPTH + N_MIXERS - 1) // N_MIXERS
N_GLA_LAYERS = (DEPTH + N_MIXERS - 2) // N_MIXERS
N_GDN_LAYERS = DEPTH // N_MIXERS

kernel_name = 'bidir_hybrid_ssd_gla_gdn_prefix_step'


def rmsnorm(x, w):
    xf = x.astype(jnp.float32)
    y = xf * lax.rsqrt(jnp.mean(xf * xf, axis=-1, keepdims=True) + NORM_EPS)
    return (y * w.astype(jnp.float32)).astype(x.dtype)


def l2norm(x):
    xf = x.astype(jnp.float32)
    return xf * lax.rsqrt(jnp.sum(xf * xf, axis=-1, keepdims=True) + 1e-6)


def modulate(x, norm_w, shift, scale):
    return rmsnorm(x, norm_w) * (1.0 + scale[:, None]) + shift[:, None]


def dwconv1d(x, w, b=None):
    k, L = w.shape[0], x.shape[1]
    pad = k // 2
    xp = jnp.pad(x, ((0, 0), (pad, k - 1 - pad), (0, 0)))
    y = xp[:, 0:L] * w[0]
    for i in range(1, k):
        y = y + xp[:, i:i + L] * w[i]
    return y if b is None else y + b


def dwconv2d_grid(x, w, b):
    bsz, L, C = x.shape
    rows = L // GRID_W
    img = x.reshape(bsz, rows, GRID_W, C)
    y = lax.conv_general_dilated(img, w[:, :, None, :].astype(x.dtype), (1, 1), 'SAME',
                                 dimension_numbers=('NHWC', 'HWIO', 'NHWC'), feature_group_count=C)
    return y.reshape(bsz, L, C) + b


def _chunk(t, nc):
    return t.reshape(t.shape[0], nc, CHUNK, *t.shape[2:])


def _bidir(scan_fn, fwd_inputs, bwd_inputs, h0):
    y_f, s_f = scan_fn(*fwd_inputs, h0[:, 0])
    y_b, s_b = scan_fn(*(jnp.flip(t, axis=1) for t in bwd_inputs), h0[:, 1])
    return y_f + jnp.flip(y_b, axis=1), jnp.stack([s_f, s_b], axis=1)


def ssd_chunked(xv, loga, bm, cm, h0):
    f32 = jnp.float32
    bsz, L = xv.shape[:2]
    nc = L // CHUNK
    xv, loga, bm, cm = (_chunk(t.astype(f32), nc) for t in (xv, loga, bm, cm))
    causal = jnp.tril(jnp.ones((CHUNK, CHUNK), dtype=bool))
    cs = jnp.cumsum(loga, axis=2)
    seg = cs[:, :, :, None] - cs[:, :, None, :]
    decay = jnp.exp(jnp.where(causal[:, :, None, None], seg, -jnp.inf))
    scores = jnp.einsum('bcign,bcjgn->bcijg', cm, bm)
    y_intra = jnp.einsum('bcijgr,bcjgrp->bcigrp', scores[..., None] * decay, xv)
    xw = xv * jnp.exp(cs[:, :, -1:] - cs)[..., None]

    def step(h, inp):
        cm_c, bm_c, xw_c, cs_c = inp
        y_c = jnp.einsum('bign,bgrnp->bigrp', cm_c, h) * jnp.exp(cs_c)[..., None]
        h = h * jnp.exp(cs_c[:, -1])[..., None, None] + jnp.einsum('bjgn,bjgrp->bgrnp', bm_c, xw_c)
        return h, y_c

    h_last, y_inter = lax.scan(step, h0.astype(f32), tuple(jnp.moveaxis(t, 1, 0) for t in (cm, bm, xw, cs)))
    y = y_intra + jnp.moveaxis(y_inter, 0, 1)
    return y.reshape(bsz, L, *y.shape[3:]), h_last


def gla_chunked(q, k, v, logg, h0):
    f32 = jnp.float32
    bsz, L = q.shape[:2]
    nc = L // CHUNK
    q, k, v, logg = (_chunk(t.astype(f32), nc) for t in (q, k, v, logg))
    causal = jnp.tril(jnp.ones((CHUNK, CHUNK), dtype=bool))
    cs = jnp.cumsum(logg, axis=2)
    qg = q * jnp.exp(cs)
    kg = k * jnp.exp(-cs)
    scores = jnp.where(causal[:, :, None], jnp.einsum('bcihd,bcjhd->bcijh', qg, kg), 0.0)
    y_intra = jnp.einsum('bcijh,bcjhe->bcihe', scores, v)
    kw = k * jnp.exp(cs[:, :, -1:] - cs)

    def step(s, inp):
        qg_c, kw_c, v_c, cs_c = inp
        y_c = jnp.einsum('bihd,bhde->bihe', qg_c, s)
        s = s * jnp.exp(cs_c[:, -1])[..., None] + jnp.einsum('bjhd,bjhe->bhde', kw_c, v_c)
        return s, y_c

    s_last, y_inter = lax.scan(step, h0.astype(f32), tuple(jnp.moveaxis(t, 1, 0) for t in (qg, kw, v, cs)))
    y = y_intra + jnp.moveaxis(y_inter, 0, 1)
    return y.reshape(bsz, L, *y.shape[3:]), s_last


def gated_delta_chunked(q, k, v, beta, logg, h0):
    f32 = jnp.float32
    bsz, L = q.shape[:2]
    nc = L // CHUNK
    q, k, v, beta, logg = (_chunk(t.astype(f32), nc) for t in (q, k, v, beta, logg))
    incl = jnp.tril(jnp.ones((CHUNK, CHUNK), dtype=bool))
    strict = jnp.tril(jnp.ones((CHUNK, CHUNK), dtype=bool), -1)
    eye = jnp.eye(CHUNK, dtype=f32)
    cs = jnp.cumsum(logg, axis=2)
    seg = cs[:, :, :, None] - cs[:, :, None, :]
    lmask = jnp.moveaxis(jnp.exp(jnp.where(incl[:, :, None], seg, -jnp.inf)), -1, 2)
    kb = k * beta[..., None]
    m = jnp.where(strict, jnp.einsum('bcihd,bcjhd->bchij', kb, k) * lmask, 0.0)
    t_inv = lax.linalg.triangular_solve(m + eye, jnp.broadcast_to(eye, m.shape),
                                        left_side=True, lower=True, unit_diagonal=True)
    u = jnp.einsum('bchij,bcjhe->bcihe', t_inv, v * beta[..., None])
    wk = jnp.einsum('bchij,bcjhd->bcihd', t_inv, kb * jnp.exp(cs)[..., None])
    a_intra = jnp.einsum('bcihd,bcjhd->bchij', q, k) * lmask
    qd = q * jnp.exp(cs)[..., None]
    kd = k * jnp.exp(cs[:, :, -1:] - cs)[..., None]

    def step(s, inp):
        qd_c, kd_c, u_c, w_c, a_c, cs_c = inp
        v_new = u_c - jnp.einsum('bihd,bhde->bihe', w_c, s)
        o_c = jnp.einsum('bihd,bhde->bihe', qd_c, s) + jnp.einsum('bhij,bjhe->bihe', a_c, v_new)
        s = s * jnp.exp(cs_c[:, -1])[..., None, None] + jnp.einsum('bjhd,bjhe->bhde', kd_c, v_new)
        return s, o_c

    s_last, o = lax.scan(step, h0.astype(f32),
                         tuple(jnp.moveaxis(t, 1, 0) for t in (qd, kd, u, wk, a_intra, cs)))
    o = jnp.moveaxis(o, 0, 1)
    return o.reshape(bsz, L, *o.shape[3:]), s_last


def mamba2_mixer(h, in_w, conv_w, conv_b, dt_bias, a_log, d_skip, norm_w, out_w, h0):
    bsz, L, _ = h.shape
    G, R, N, P = SSM_GROUPS, SSM_HPG, SSM_STATE, SSM_HEADDIM
    z, xbc, dt_raw = jnp.split(h @ in_w, [SSM_DI, SSM_DI + SSM_CONV_DIM], axis=-1)
    xbc = jax.nn.silu(dwconv1d(xbc, conv_w, conv_b))
    xs, bm, cm = jnp.split(xbc, [SSM_DI, SSM_DI + SSM_GN], axis=-1)
    xs = xs.reshape(bsz, L, G, R, P)
    bm = bm.reshape(bsz, L, G, N)
    cm = cm.reshape(bsz, L, G, N)
    dt = jax.nn.softplus((dt_raw.reshape(bsz, L, 2, SSM_HEADS) + dt_bias).astype(jnp.float32))
    dt = dt.reshape(bsz, L, 2, G, R)
    loga = dt * (-jnp.exp(a_log.astype(jnp.float32))).reshape(2, G, R)
    if h0 is None:
        h0 = jnp.zeros((bsz, 2, SSM_HEADS, N, P), jnp.float32)
    h0 = h0.reshape(bsz, 2, G, R, N, P)
    y, state = _bidir(ssd_chunked,
                      (xs * dt[:, :, 0][..., None], loga[:, :, 0], bm, cm),
                      (xs * dt[:, :, 1][..., None], loga[:, :, 1], bm, cm), h0)
    y = (y + xs * d_skip.reshape(G, R, 1)).reshape(bsz, L, SSM_DI).astype(h.dtype) * jax.nn.silu(z)
    y = rmsnorm(y.reshape(bsz, L, G, SSM_DI // G), norm_w.reshape(G, SSM_DI // G)).reshape(bsz, L, SSM_DI)
    return y @ out_w, state.reshape(bsz, 2, SSM_HEADS, N, P).astype(h.dtype)


def gla_mixer(h, in_w, gate_w1, gate_w2, gate_b, norm_w, out_w, h0):
    bsz, L, _ = h.shape
    q, k, v, r = jnp.split(h @ in_w, [GLA_K, 2 * GLA_K, 2 * GLA_K + GLA_V], axis=-1)
    q = q.reshape(bsz, L, GLA_HEADS, GLA_DK) * GLA_DK ** -0.5
    k = k.reshape(bsz, L, GLA_HEADS, GLA_DK)
    v = v.reshape(bsz, L, GLA_HEADS, GLA_DV)
    gate_lr = jnp.einsum('bld,edr->bler', h, gate_w1)
    gate_logit = jnp.einsum('bler,erk->blek', gate_lr, gate_w2) + gate_b
    logg = jax.nn.log_sigmoid(gate_logit.astype(jnp.float32)) / GLA_NORMALIZER
    logg = logg.reshape(bsz, L, 2, GLA_HEADS, GLA_DK)
    if h0 is None:
        h0 = jnp.zeros((bsz, 2, GLA_HEADS, GLA_DK, GLA_DV), jnp.float32)
    o, state = _bidir(gla_chunked, (q, k, v, logg[:, :, 0]), (q, k, v, logg[:, :, 1]), h0)
    o = rmsnorm(o.astype(h.dtype), norm_w) * jax.nn.silu(r.reshape(bsz, L, GLA_HEADS, GLA_DV))
    return o.reshape(bsz, L, GLA_V) @ out_w, state.astype(h.dtype)


def gdn_mixer(h, in_w, conv_w, dt_bias, a_log, norm_w, out_w, h0):
    bsz, L, _ = h.shape
    qkv, z, a_raw, b_raw = jnp.split(h @ in_w, [GDN_CONV_DIM, GDN_CONV_DIM + GDN_V,
                                                GDN_CONV_DIM + GDN_V + 2 * GDN_HEADS], axis=-1)
    qkv = jax.nn.silu(dwconv1d(qkv, conv_w))
    q, k, v = jnp.split(qkv, [GDN_K, 2 * GDN_K], axis=-1)
    q = l2norm(q.reshape(bsz, L, GDN_HEADS, GDN_DK)) * GDN_DK ** -0.5
    k = l2norm(k.reshape(bsz, L, GDN_HEADS, GDN_DK))
    v = v.reshape(bsz, L, GDN_HEADS, GDN_DV)
    beta = jax.nn.sigmoid(b_raw.reshape(bsz, L, 2, GDN_HEADS).astype(jnp.float32))
    logg = -jnp.exp(a_log.astype(jnp.float32)) * jax.nn.softplus(
        (a_raw.reshape(bsz, L, 2, GDN_HEADS) + dt_bias).astype(jnp.float32))
    if h0 is None:
        h0 = jnp.zeros((bsz, 2, GDN_HEADS, GDN_DK, GDN_DV), jnp.float32)
    o, state = _bidir(gated_delta_chunked, (q, k, v, beta[:, :, 0], logg[:, :, 0]),
                      (q, k, v, beta[:, :, 1], logg[:, :, 1]), h0)
    o = rmsnorm(o.astype(h.dtype), norm_w) * jax.nn.silu(z.reshape(bsz, L, GDN_HEADS, GDN_DV))
    return o.reshape(bsz, L, GDN_V) @ out_w, state.astype(h.dtype)


def conv_glu(h, in_w, conv_w, conv_b, out_w, on_grid):
    a, v = jnp.split(h @ in_w, 2, axis=-1)
    a = dwconv2d_grid(a, conv_w, conv_b) if on_grid else dwconv1d(a, conv_w[1], conv_b)
    return (jax.nn.silu(a) * v) @ out_w


def run_trunk(x, cond, ssm_state, gla_state, gdn_state, on_grid, w):
    new_ssm, new_gla, new_gdn = [], [], []
    cond_act = jax.nn.silu(cond)
    for i in range(DEPTH):
        mod = cond_act @ w['ada_w'][i] + w['ada_b'][i]
        sh1, sc1, g1, sh2, sc2, g2 = jnp.split(mod, 6, axis=-1)
        h = modulate(x, w['norm_mix_w'][i], sh1, sc1)
        kind, j = i % N_MIXERS, i // N_MIXERS
        if kind == 0:
            y, st = mamba2_mixer(h, w['ssm_in_w'][j], w['ssm_conv_w'][j], w['ssm_conv_b'][j],
                                 w['ssm_dt_bias'][j], w['ssm_a_log'][j], w['ssm_d'][j],
                                 w['ssm_norm_w'][j], w['ssm_out_w'][j],
                                 None if ssm_state is None else ssm_state[:, j])
            new_ssm.append(st)
        elif kind == 1:
            y, st = gla_mixer(h, w['gla_in_w'][j], w['gla_gate_w1'][j], w['gla_gate_w2'][j],
                              w['gla_gate_b'][j], w['gla_norm_w'][j], w['gla_out_w'][j],
                              None if gla_state is None else gla_state[:, j])
            new_gla.append(st)
        else:
            y, st = gdn_mixer(h, w['gdn_in_w'][j], w['gdn_conv_w'][j], w['gdn_dt_bias'][j],
                              w['gdn_a_log'][j], w['gdn_norm_w'][j], w['gdn_out_w'][j],
                              None if gdn_state is None else gdn_state[:, j])
            new_gdn.append(st)
        x = x + g1[:, None] * y
        h = modulate(x, w['norm_ffn_w'][i], sh2, sc2)
        x = x + g2[:, None] * conv_glu(h, w['ffn_in_w'][i], w['ffn_conv_w'][i], w['ffn_conv_b'][i],
                                       w['ffn_out_w'][i], on_grid)
    return (rmsnorm(x, w['final_norm_w']), jnp.stack(new_ssm, axis=1),
            jnp.stack(new_gla, axis=1), jnp.stack(new_gdn, axis=1))


def setup_inputs(seed: int = 0) -> dict:
    key = jax.random.key(seed)
    keys = iter(jax.random.split(key, 48))
    f32 = jnp.float32
    D = D_MODEL
    nA, nB, nC = N_SSM_LAYERS, N_GLA_LAYERS, N_GDN_LAYERS

    def normal(shape, scale):
        return jax.random.normal(next(keys), shape, f32) * scale

    def gain(shape):
        return 1.0 + normal(shape, 0.02)

    def dt_bias(shape):
        dt = jnp.exp(jax.random.uniform(next(keys), shape, f32, math.log(1e-3), math.log(1e-1)))
        return dt + jnp.log(-jnp.expm1(-dt))

    def a_log(shape):
        return jnp.log(jax.random.uniform(next(keys), shape, f32, 1.0, 16.0))

    return {
        'x_prompt': normal((BATCH, SEQ, D), 1.0),
        'x_sample': normal((DEC_BATCH, DEC_SEQ, D), 1.0),
        'state_ssm': normal((DEC_BATCH, nA, 2, SSM_HEADS, SSM_STATE, SSM_HEADDIM), 0.1),
        'state_gla': normal((DEC_BATCH, nB, 2, GLA_HEADS, GLA_DK, GLA_DV), 0.1),
        'state_gdn': normal((DEC_BATCH, nC, 2, GDN_HEADS, GDN_DK, GDN_DV), 0.1),
        'c': normal((DEC_BATCH, D), 1.0),
        'c_ctx': normal((D,), 1.0),
        'norm_mix_w': gain((DEPTH, D)),
        'norm_ffn_w': gain((DEPTH, D)),
        'ada_w': normal((DEPTH, D, 6 * D), 0.5 * D ** -0.5),
        'ada_b': normal((DEPTH, 6 * D), 0.02),
        'ffn_in_w': normal((DEPTH, D, 2 * D_FF), D ** -0.5),
        'ffn_conv_w': normal((DEPTH, 3, 3, D_FF), 1.0 / 3.0),
        'ffn_conv_b': normal((DEPTH, D_FF), 0.02),
        'ffn_out_w': normal((DEPTH, D_FF, D), D_FF ** -0.5),
        'ssm_in_w': normal((nA, D, SSM_PROJ), D ** -0.5),
        'ssm_conv_w': normal((nA, SHORT_CONV, SSM_CONV_DIM), SHORT_CONV ** -0.5),
        'ssm_conv_b': normal((nA, SSM_CONV_DIM), 0.02),
        'ssm_dt_bias': dt_bias((nA, 2, SSM_HEADS)),
        'ssm_a_log': a_log((nA, 2, SSM_HEADS)),
        'ssm_d': gain((nA, SSM_HEADS)),
        'ssm_norm_w': gain((nA, SSM_DI)),
        'ssm_out_w': normal((nA, SSM_DI, D), SSM_DI ** -0.5),
        'gla_in_w': normal((nB, D, GLA_PROJ), D ** -0.5),
        'gla_gate_w1': normal((nB, 2, D, GLA_RANK), D ** -0.5),
        'gla_gate_w2': normal((nB, 2, GLA_RANK, GLA_K), GLA_RANK ** -0.5),
        'gla_gate_b': normal((nB, 2, GLA_K), 0.1),
        'gla_norm_w': gain((nB, GLA_DV)),
        'gla_out_w': normal((nB, GLA_V, D), GLA_V ** -0.5),
        'gdn_in_w': normal((nC, D, GDN_PROJ), D ** -0.5),
        'gdn_conv_w': normal((nC, SHORT_CONV, GDN_CONV_DIM), SHORT_CONV ** -0.5),
        'gdn_dt_bias': dt_bias((nC, 2, GDN_HEADS)),
        'gdn_a_log': a_log((nC, 2, GDN_HEADS)),
        'gdn_norm_w': gain((nC, GDN_DV)),
        'gdn_out_w': normal((nC, GDN_V, D), GDN_V ** -0.5),
        'final_norm_w': gain((D,)),
    }


def reference(x_prompt, x_sample, state_ssm, state_gla, state_gdn, c, c_ctx,
              norm_mix_w, norm_ffn_w, ada_w, ada_b, ffn_in_w, ffn_conv_w, ffn_conv_b, ffn_out_w,
              ssm_in_w, ssm_conv_w, ssm_conv_b, ssm_dt_bias, ssm_a_log, ssm_d, ssm_norm_w, ssm_out_w,
              gla_in_w, gla_gate_w1, gla_gate_w2, gla_gate_b, gla_norm_w, gla_out_w,
              gdn_in_w, gdn_conv_w, gdn_dt_bias, gdn_a_log, gdn_norm_w, gdn_out_w,
              final_norm_w):
    w = dict(norm_mix_w=norm_mix_w, norm_ffn_w=norm_ffn_w, ada_w=ada_w, ada_b=ada_b,
             ffn_in_w=ffn_in_w, ffn_conv_w=ffn_conv_w, ffn_conv_b=ffn_conv_b, ffn_out_w=ffn_out_w,
             ssm_in_w=ssm_in_w, ssm_conv_w=ssm_conv_w, ssm_conv_b=ssm_conv_b, ssm_dt_bias=ssm_dt_bias,
             ssm_a_log=ssm_a_log, ssm_d=ssm_d, ssm_norm_w=ssm_norm_w, ssm_out_w=ssm_out_w,
             gla_in_w=gla_in_w, gla_gate_w1=gla_gate_w1, gla_gate_w2=gla_gate_w2, gla_gate_b=gla_gate_b,
             gla_norm_w=gla_norm_w, gla_out_w=gla_out_w,
             gdn_in_w=gdn_in_w, gdn_conv_w=gdn_conv_w, gdn_dt_bias=gdn_dt_bias, gdn_a_log=gdn_a_log,
             gdn_norm_w=gdn_norm_w, gdn_out_w=gdn_out_w, final_norm_w=final_norm_w)
    cond_ctx = jnp.broadcast_to(c_ctx, (x_prompt.shape[0], c_ctx.shape[0]))
    y_prompt, new_state_ssm, new_state_gla, new_state_gdn = run_trunk(
        x_prompt, cond_ctx, None, None, None, False, w)
    y_sample, _, _, _ = run_trunk(x_sample, c, state_ssm, state_gla, state_gdn, True, w)
    return (y_prompt, y_sample, new_state_ssm, new_state_gla, new_state_gdn)
```

```python
import functools
from typing import NamedTuple

import numpy as np
import jax
import jax.numpy as jnp
from jax import lax
from jax.experimental import pallas as pl
from jax.experimental.pallas import tpu as pltpu

f32, bf16 = jnp.float32, jnp.bfloat16

BLK = 256
CHUNK = 64
GRID_W = 64
NORM_EPS = 1e-6
SMALL = 128
GLA_RANK = 16
GLA_NORMALIZER = 16.0
SSM_GROUPS = 8
SSM_HEADDIM = 64
SSM_STATE = 128
VMEM_LIMIT = 48 * 1024 * 1024

NN = (((1,), (0,)), ((), ()))
NT = (((1,), (1,)), ((), ()))
TN = (((0,), (0,)), ((), ()))


class Geo(NamedTuple):
    D: int
    Tc: int
    Ts: int
    SEQ: int
    LS: int
    Bc: int
    Bs: int

    @property
    def T(self):
        return self.Tc + self.Ts

    @property
    def nb(self):
        return self.T // BLK

    @property
    def ncb(self):
        return self.Tc // BLK

    @property
    def bpc(self):
        return self.SEQ // BLK

    @property
    def bps(self):
        return self.LS // BLK


def _mm(a, b, dims=NN):
    return lax.dot_general(a, b, dims, preferred_element_type=f32)


def _split3(a):
    hi = a.astype(bf16)
    r1 = a - hi.astype(f32)
    mid = r1.astype(bf16)
    lo = (r1 - mid.astype(f32)).astype(bf16)
    return hi, mid, lo


def _mm_x(a, b, dims=NN):
    if a.dtype == bf16:
        ps = [_mm(a, p, dims) for p in _split3(b)]
    else:
        ps = [_mm(p, b, dims) for p in _split3(a)]
    return ps[0] + ps[1] + ps[2]


def _mm_hi(a, b):
    ah = a.astype(bf16)
    al = (a - ah.astype(f32)).astype(bf16)
    bh = b.astype(bf16)
    bl = (b - bh.astype(f32)).astype(bf16)
    return _mm(ah, bh) + _mm(al, bh) + _mm(ah, bl)


def _silu(x):
    return x * jax.nn.sigmoid(x)


def _tri_mask(n, upper, chunk=None, strict=False):
    r = lax.broadcasted_iota(jnp.int32, (n, n), 0)
    c = lax.broadcasted_iota(jnp.int32, (n, n), 1)
    if upper:
        m = (c > r) if strict else (c >= r)
    else:
        m = (c < r) if strict else (c <= r)
    if chunk is not None:
        sh = int(np.log2(chunk))
        same = lax.shift_right_logical(r, sh) == lax.shift_right_logical(c, sh)
        m = jnp.logical_and(m, same)
    return m


def _as_bf16(mask):
    return jnp.where(mask, 1.0, 0.0).astype(bf16)


def _ada_kernel(c_ref, w_ref, b_ref, o_ref):
    o_ref[0] = _mm_hi(_silu(c_ref[...]), w_ref[0]) + b_ref[0]


def _ada_table(cond8, ada_w, ada_b):
    depth, D, N6 = ada_w.shape
    tn = N6 // 4
    return pl.pallas_call(
        _ada_kernel,
        out_shape=jax.ShapeDtypeStruct((depth, 8, N6), f32),
        grid=(depth, N6 // tn),
        in_specs=[pl.BlockSpec((8, D), lambda l, j: (0, 0)),
                  pl.BlockSpec((1, D, tn), lambda l, j: (l, 0, j)),
                  pl.BlockSpec((1, 1, tn), lambda l, j: (l, 0, j))],
        out_specs=pl.BlockSpec((1, 8, tn), lambda l, j: (l, 0, j)),
        compiler_params=pltpu.CompilerParams(
            dimension_semantics=("parallel", "parallel"), vmem_limit_bytes=VMEM_LIMIT),
        name="ada_table",
    )(cond8, ada_w, ada_b.reshape(depth, 1, N6))


def _in_proj_kernel(*refs, has_small):
    if has_small:
        x_ref, nw_ref, sh_ref, sc_ref, w_ref, ws_ref, o_ref, os_ref, h_ref = refs
    else:
        x_ref, nw_ref, sh_ref, sc_ref, w_ref, o_ref, h_ref = refs

    @pl.when(pl.program_id(1) == 0)
    def _():
        x = x_ref[...]
        h = x * lax.rsqrt(jnp.mean(x * x, axis=-1, keepdims=True) + NORM_EPS) * nw_ref[...]
        h = h * (1.0 + sc_ref[0]) + sh_ref[0]
        hb = h.astype(bf16)
        h_ref[...] = hb
        if has_small:
            ws = ws_ref[...]
            wh = ws.astype(bf16)
            wl = (ws - wh.astype(f32)).astype(bf16)
            hl = (h - hb.astype(f32)).astype(bf16)
            os_ref[...] = _mm(hb, wh) + _mm(hl, wh) + _mm(hb, wl)

    o_ref[...] = _mm(h_ref[...], w_ref[...])


def _row_tile(geo, pref):
    for tm in (pref, 512, 256):
        if tm <= pref and geo.Tc % tm == 0 and geo.LS % tm == 0:
            return tm
    raise ValueError("token counts must be multiples of 256")


def _cond_of_tile(i, geo, tm):
    nct = geo.Tc // tm
    return jnp.where(i < nct, 0, 1 + jnp.maximum(i - nct, 0) // (geo.LS // tm))


def _in_proj(x, norm_w, mod, shift_idx, w_main, w_small, geo, tn=512):
    T, D = x.shape
    N = w_main.shape[1]
    tm = _row_tile(geo, 1024)
    assert N % tn == 0
    has_small = w_small is not None
    cond = functools.partial(_cond_of_tile, geo=geo, tm=tm)
    in_specs = [pl.BlockSpec((tm, D), lambda i, j: (i, 0)),
                pl.BlockSpec((1, D), lambda i, j: (0, 0)),
                pl.BlockSpec((1, 1, D), lambda i, j: (cond(i), 0, shift_idx)),
                pl.BlockSpec((1, 1, D), lambda i, j: (cond(i), 0, shift_idx + 1)),
                pl.BlockSpec((D, tn), lambda i, j: (0, j))]
    args = [x, norm_w.reshape(1, D), mod, mod, w_main]
    out_shape = [jax.ShapeDtypeStruct((T, N), f32)]
    out_specs = [pl.BlockSpec((tm, tn), lambda i, j: (i, j))]
    if has_small:
        in_specs.append(pl.BlockSpec((D, SMALL), lambda i, j: (0, 0)))
        args.append(w_small)
        out_shape.append(jax.ShapeDtypeStruct((T, SMALL), f32))
        out_specs.append(pl.BlockSpec((tm, SMALL), lambda i, j: (i, 0)))
    res = pl.pallas_call(
        functools.partial(_in_proj_kernel, has_small=has_small),
        out_shape=out_shape,
        grid=(T // tm, N // tn),
        in_specs=in_specs,
        out_specs=out_specs,
        scratch_shapes=[pltpu.VMEM((tm, D), bf16)],
        compiler_params=pltpu.CompilerParams(
            dimension_semantics=("parallel", "arbitrary"), vmem_limit_bytes=VMEM_LIMIT),
        name="in_proj",
    )(*args)
    return res if has_small else (res[0], None)


def _out_proj_kernel(*refs, final):
    if final:
        u_ref, w_ref, x_ref, g_ref, fw_ref, o_ref = refs
    else:
        u_ref, w_ref, x_ref, g_ref, o_ref = refs
    r = x_ref[...] + g_ref[0] * _mm(u_ref[...], w_ref[...])
    if final:
        r = r * lax.rsqrt(jnp.mean(r * r, axis=-1, keepdims=True) + NORM_EPS) * fw_ref[...]
    o_ref[...] = r


def _out_proj(u, w, x, mod, gate_idx, geo, final_w=None):
    T, K = u.shape
    D = w.shape[1]
    tm = _row_tile(geo, 512)
    cond = functools.partial(_cond_of_tile, geo=geo, tm=tm)
    final = final_w is not None
    in_specs = [pl.BlockSpec((tm, K), lambda i: (i, 0)),
                pl.BlockSpec((K, D), lambda i: (0, 0)),
                pl.BlockSpec((tm, D), lambda i: (i, 0)),
                pl.BlockSpec((1, 1, D), lambda i: (cond(i), 0, gate_idx))]
    args = [u, w, x, mod]
    if final:
        in_specs.append(pl.BlockSpec((1, D), lambda i: (0, 0)))
        args.append(final_w.reshape(1, D))
    return pl.pallas_call(
        functools.partial(_out_proj_kernel, final=final),
        out_shape=jax.ShapeDtypeStruct((T, D), f32),
        grid=(T // tm,),
        in_specs=in_specs,
        out_specs=pl.BlockSpec((tm, D), lambda i: (i, 0)),
        compiler_params=pltpu.CompilerParams(
            dimension_semantics=("parallel",), vmem_limit_bytes=VMEM_LIMIT),
        name="out_proj",
    )(*args)


def _ffn_conv_kernel(a_ref, v_ref, w_ref, b_ref, o_ref, pad_ref, *, n_ctx_img, IB, SEQ):
    PADR = 128
    tf = a_ref.shape[1]
    pad_ref[0:PADR, :] = jnp.zeros((PADR, tf), f32)
    pad_ref[PADR + IB:PADR + IB + PADR, :] = jnp.zeros((PADR, tf), f32)
    pad_ref[PADR:PADR + IB, :] = a_ref[...]
    w = w_ref[...]
    b = b_ref[...]
    row = lax.broadcasted_iota(jnp.int32, (BLK, tf), 0)

    def taps(s, pos, width, kr):
        left = jnp.where(pos > 0, pad_ref[s - 1:s - 1 + BLK, :], 0.0)
        right = jnp.where(pos < width - 1, pad_ref[s + 1:s + 1 + BLK, :], 0.0)
        return (left * w[3 * kr:3 * kr + 1] + pad_ref[s:s + BLK, :] * w[3 * kr + 1:3 * kr + 2]
                + right * w[3 * kr + 2:3 * kr + 3])

    def finish(c, acc):
        o_ref[c * BLK:(c + 1) * BLK, :] = (_silu(acc + b) * v_ref[c * BLK:(c + 1) * BLK, :]).astype(bf16)

    @pl.when(pl.program_id(0) < n_ctx_img)
    def _():
        for c in range(IB // BLK):
            pos = jnp.bitwise_and(row + (c * BLK) % SEQ, SEQ - 1)
            finish(c, taps(PADR + c * BLK, pos, SEQ, 1))

    @pl.when(pl.program_id(0) >= n_ctx_img)
    def _():
        col = jnp.bitwise_and(row, GRID_W - 1)
        for c in range(IB // BLK):
            s = PADR + c * BLK
            acc = taps(s - GRID_W, col, GRID_W, 0) + taps(s, col, GRID_W, 1) + taps(s + GRID_W, col, GRID_W, 2)
            finish(c, acc)


def _ffn_conv(proj, conv_w9, conv_b, geo, tf=256):
    T, F2 = proj.shape
    F = F2 // 2
    IB = geo.LS
    assert F % tf == 0 and geo.Tc % IB == 0 and IB % BLK == 0
    assert geo.SEQ & (geo.SEQ - 1) == 0 and BLK % GRID_W == 0
    nf = F // tf
    return pl.pallas_call(
        functools.partial(_ffn_conv_kernel, n_ctx_img=geo.Tc // IB, IB=IB, SEQ=geo.SEQ),
        out_shape=jax.ShapeDtypeStruct((T, F), bf16),
        grid=(T // IB, nf),
        in_specs=[pl.BlockSpec((IB, tf), lambda i, j: (i, j)),
                  pl.BlockSpec((IB, tf), lambda i, j: (i, nf + j)),
                  pl.BlockSpec((9, tf), lambda i, j: (0, j)),
                  pl.BlockSpec((1, tf), lambda i, j: (0, j))],
        out_specs=pl.BlockSpec((IB, tf), lambda i, j: (i, j)),
        scratch_shapes=[pltpu.VMEM((IB + 256, tf), f32)],
        compiler_params=pltpu.CompilerParams(
            dimension_semantics=("parallel", "parallel"), vmem_limit_bytes=VMEM_LIMIT),
        name="ffn_conv",
    )(proj, proj, conv_w9, conv_b.reshape(1, F))


def _blk_idx(i, geo, rev):
    return (geo.nb - 1 - i) if rev else i


def _tok_spec(cw, colfn, geo, rev):
    return pl.BlockSpec((BLK, cw), lambda u, i: (_blk_idx(i, geo, rev), colfn(u)))


def _halo_specs(cw, colfn, geo, rev):
    r8 = BLK // 8
    last = geo.T // 8 - 1
    prev = pl.BlockSpec((8, cw), lambda u, i: (jnp.maximum(_blk_idx(i, geo, rev) * r8 - 1, 0), colfn(u)))
    nxt = pl.BlockSpec((8, cw), lambda u, i: (jnp.minimum((_blk_idx(i, geo, rev) + 1) * r8, last), colfn(u)))
    return prev, nxt


def _h0_spec(geo, rev):
    def idx(u, i):
        ip = _blk_idx(i, geo, rev)
        return (jnp.maximum(ip - geo.ncb, 0) // geo.bps, u, 0, 0)
    return pl.BlockSpec((1, 1, 128, 256), idx)


def _st_spec(geo, rev):
    def idx(u, i):
        ip = _blk_idx(i, geo, rev)
        return (jnp.minimum(ip // geo.bpc, geo.Bc - 1), u, 0, 0)
    return pl.BlockSpec((1, 1, 128, 256), idx)


def _param_spec(rows, cw, colfn):
    return pl.BlockSpec((rows, cw), lambda u, i: (0, colfn(u)))


def _flags(geo, rev):
    ip = _blk_idx(pl.program_id(1), geo, rev)
    is_ctx = ip < geo.ncb
    pos = jnp.where(is_ctx, lax.rem(ip, geo.bpc), lax.rem(jnp.maximum(ip - geo.ncb, 0), geo.bps))
    n = jnp.where(is_ctx, geo.bpc, geo.bps)
    first = pos == 0
    last = pos == n - 1
    return is_ctx, first, last, (last if rev else first)


def _conv3_silu(cur_ref, prev_ref, next_ref, w_ref, b_ref, first, last):
    cur = cur_ref[...]
    n = cur.shape[0]
    row = lax.broadcasted_iota(jnp.int32, cur.shape, 0)
    pr = jnp.where(first, 0.0, prev_ref[7:8, :])
    nx = jnp.where(last, 0.0, next_ref[0:1, :])
    xm = jnp.where(row == 0, pr, pltpu.roll(cur, 1, axis=0))
    xp = jnp.where(row == n - 1, nx, pltpu.roll(cur, n - 1, axis=0))
    w = w_ref[...]
    y = xm * w[0:1] + cur * w[1:2] + xp * w[2:3]
    if b_ref is not None:
        y = y + b_ref[...]
    return _silu(y)


def _init_state(S, h0_ref, is_ctx, start):
    @pl.when(start)
    def _():
        S[...] = jnp.where(is_ctx, 0.0, h0_ref[0, 0])


def _norm_gate_out(y_ref, o, gate, nw_ref, gate_first):
    g = _silu(gate)
    if gate_first:
        o = o * g
    o = o * lax.rsqrt(jnp.mean(o * o, axis=-1, keepdims=True) + NORM_EPS) * nw_ref[...]
    if not gate_first:
        o = o * g
    y_ref[...] = o.astype(y_ref.dtype)


def _scan_call(kernel, name, grid_u, in_specs, args, y_dtype, geo, rev, ycols):
    T = geo.T
    return pl.pallas_call(
        kernel,
        out_shape=[jax.ShapeDtypeStruct((T, ycols), y_dtype),
                   jax.ShapeDtypeStruct((geo.Bc, grid_u, 128, 256), f32)],
        grid=(grid_u, geo.nb),
        in_specs=in_specs,
        out_specs=[_tok_spec(256, lambda u: u, geo, rev), _st_spec(geo, rev)],
        scratch_shapes=[pltpu.VMEM((128, 256), f32)],
        compiler_params=pltpu.CompilerParams(
            dimension_semantics=("parallel", "arbitrary"), vmem_limit_bytes=VMEM_LIMIT),
        name=name,
    )(*args)


def _ssd_kernel(*refs, rev, geo):
    (xs_c, xs_p, xs_n, b_c, b_p, b_n, c_c, c_p, c_n, dt_ref,
     cwx, cbx, cwb, cbb, cwc, cbc, dtb_ref, alog_ref, xsel_ref, xh_ref, h0_ref) = refs[:21]
    if rev:
        dsk_ref, z_ref, nw_ref, yf_ref, y_ref, st_ref, S = refs[21:]
    else:
        y_ref, st_ref, S = refs[21:]
    is_ctx, first, last, start = _flags(geo, rev)
    _init_state(S, h0_ref, is_ctx, start)

    x = _conv3_silu(xs_c, xs_p, xs_n, cwx, cbx, first, last)
    bm = _conv3_silu(b_c, b_p, b_n, cwb, cbb, first, last)
    cm = _conv3_silu(c_c, c_p, c_n, cwc, cbc, first, last)
    dt = jax.nn.softplus(dt_ref[...] + dtb_ref[...])
    la = dt * (-jnp.exp(alog_ref[...]))
    xsel = xsel_ref[0]
    dt_e = _mm_x(dt, xsel)
    la_e = _mm_x(la, xsel)
    tri = _tri_mask(BLK, rev)
    cs_e = _mm_x(_as_bf16(tri), la_e)
    la_h = _mm_x(la, xh_ref[0])
    cs_t = _mm_x(la_h, _as_bf16(_tri_mask(BLK, not rev)), TN)

    xv = x * dt_e
    cmb = cm.astype(bf16)
    bmb = bm.astype(bf16)
    g = _mm(cmb, bmb, NT)
    xvb = xv.astype(bf16)
    ys = []
    for r in range(4):
        seg = cs_e[:, r * 64:r * 64 + 1] - cs_t[r:r + 1, :]
        m = jnp.where(tri, g * jnp.exp(seg), 0.0)
        ys.append(_mm(m.astype(bf16), xvb[:, r * 64:(r + 1) * 64]))
    y = jnp.concatenate(ys, axis=1)
    Sv = S[...]
    y = y + _mm(cmb, Sv.astype(bf16)) * jnp.exp(cs_e)
    li = 0 if rev else BLK - 1
    cl = cs_e[li:li + 1, :]
    xw = xv * jnp.exp(cl - cs_e)
    Sn = Sv * jnp.exp(cl) + _mm(bmb, xw.astype(bf16), TN)
    S[...] = Sn

    @pl.when(is_ctx)
    def _():
        st_ref[0, 0] = Sn

    if rev:
        o = yf_ref[...] + y + x * dsk_ref[...]
        _norm_gate_out(y_ref, o, z_ref[...], nw_ref, gate_first=True)
    else:
        y_ref[...] = y


def _ssd_scan(proj, dtp, conv_w, conv_b, dtb, alog, dskip, norm_w, h0, yf, geo, rev):
    G = SSM_GROUPS
    d = 1 if rev else 0
    xsel = np.zeros((G, 128, 256), np.float32)
    xh = np.zeros((G, 128, 128), np.float32)
    for g in range(G):
        for r in range(4):
            xsel[g, d * 32 + 4 * g + r, r * 64:(r + 1) * 64] = 1.0
            xh[g, d * 32 + 4 * g + r, r] = 1.0
    cols = [(256, lambda u: 8 + u), (128, lambda u: 32 + u), (128, lambda u: 40 + u)]
    in_specs, args = [], []
    for cw, fn in cols:
        in_specs += [_tok_spec(cw, fn, geo, rev), *_halo_specs(cw, fn, geo, rev)]
        args += [proj, proj, proj]
    in_specs.append(_tok_spec(SMALL, lambda u: 0, geo, rev))
    args.append(dtp)
    for cw, fn in [(256, lambda u: u), (128, lambda u: 16 + u), (128, lambda u: 24 + u)]:
        in_specs += [_param_spec(3, cw, fn), _param_spec(1, cw, fn)]
        args += [conv_w, conv_b]
    in_specs += [_param_spec(1, SMALL, lambda u: 0), _param_spec(1, SMALL, lambda u: 0),
                 pl.BlockSpec((1, 128, 256), lambda u, i: (u, 0, 0)),
                 pl.BlockSpec((1, 128, 128), lambda u, i: (u, 0, 0)),
                 _h0_spec(geo, rev)]
    args += [dtb, alog, jnp.asarray(xsel, bf16), jnp.asarray(xh, bf16), h0]
    if rev:
        in_specs += [_param_spec(1, 256, lambda u: u), _tok_spec(256, lambda u: u, geo, rev),
                     _param_spec(1, 256, lambda u: u), _tok_spec(256, lambda u: u, geo, rev)]
        args += [dskip, proj, norm_w, yf]
    return _scan_call(functools.partial(_ssd_kernel, rev=rev, geo=geo), "ssd_bwd" if rev else "ssd_fwd",
                      G, in_specs, args, bf16 if rev else f32, geo, rev, 2048)


def _gla_kernel(*refs, rev, geo):
    q_ref, k_ref, v_ref, lr_ref, w2_ref, gb_ref, h0_ref = refs[:7]
    if rev:
        r_ref, nw_ref, yf_ref, y_ref, st_ref, S = refs[7:]
    else:
        y_ref, st_ref, S = refs[7:]
    is_ctx, first, last, start = _flags(geo, rev)
    _init_state(S, h0_ref, is_ctx, start)
    d = 1 if rev else 0
    dk = q_ref.shape[1]

    q = q_ref[...] * dk ** -0.5
    k = k_ref[...]
    v = v_ref[...]
    lr = lr_ref[...][:, d * GLA_RANK:(d + 1) * GLA_RANK]
    logit = _mm_hi(lr, w2_ref[0]) + gb_ref[0]
    lg = jax.nn.log_sigmoid(logit) * (1.0 / GLA_NORMALIZER)
    tri = _tri_mask(BLK, rev, chunk=CHUNK)
    cs = _mm_x(_as_bf16(tri), lg)
    cs_t = _mm_x(lg, _as_bf16(_tri_mask(BLK, not rev, chunk=CHUNK)), TN)
    qg = q * jnp.exp(cs)
    kg = k * jnp.exp(-cs)
    qgb = qg.astype(bf16)
    vb = v.astype(bf16)
    sc = jnp.where(tri, _mm(qgb, kg.astype(bf16), NT), 0.0)
    y = _mm(sc.astype(bf16), vb)

    Sv = S[...]
    nch = BLK // CHUNK
    ys = [None] * nch
    for c in (range(nch - 1, -1, -1) if rev else range(nch)):
        lo = c * CHUNK
        li = lo if rev else lo + CHUNK - 1
        ys[c] = _mm(qgb[lo:lo + CHUNK], Sv.astype(bf16))
        kw = k[lo:lo + CHUNK] * jnp.exp(cs[li:li + 1, :] - cs[lo:lo + CHUNK])
        Sv = Sv * jnp.exp(cs_t[:, li:li + 1]) + _mm(kw.astype(bf16), vb[lo:lo + CHUNK], TN)
    y = y + jnp.concatenate(ys, axis=0)
    S[...] = Sv

    @pl.when(is_ctx)
    def _():
        st_ref[0, 0] = Sv

    if rev:
        _norm_gate_out(y_ref, yf_ref[...] + y, r_ref[...], nw_ref, gate_first=False)
    else:
        y_ref[...] = y


def _gla_scan(proj, lrp, w2, gb, norm_w, h0, yf, geo, rev):
    H = 4
    d = 1 if rev else 0
    in_specs = [_tok_spec(128, lambda u: u, geo, rev), _tok_spec(128, lambda u: 4 + u, geo, rev),
                _tok_spec(256, lambda u: 4 + u, geo, rev), _tok_spec(SMALL, lambda u: 0, geo, rev),
                pl.BlockSpec((1, GLA_RANK, 128), lambda u, i: (d, 0, u)),
                pl.BlockSpec((1, 1, 128), lambda u, i: (d, 0, u)),
                _h0_spec(geo, rev)]
    args = [proj, proj, proj, lrp, w2, gb, h0]
    if rev:
        in_specs += [_tok_spec(256, lambda u: 8 + u, geo, rev), _param_spec(1, 256, lambda u: 0),
                     _tok_spec(256, lambda u: u, geo, rev)]
        args += [proj, norm_w, yf]
    return _scan_call(functools.partial(_gla_kernel, rev=rev, geo=geo), "gla_bwd" if rev else "gla_fwd",
                      H, in_specs, args, bf16 if rev else f32, geo, rev, 1024)


def _gdn_kernel(*refs, rev, geo):
    (q_c, q_p, q_n, k_c, k_p, k_n, v_c, v_p, v_n, cwq, cwk, cwv,
     ab_ref, dtb_ref, alog_ref, xab_ref, h0_ref) = refs[:17]
    if rev:
        z_ref, nw_ref, yf_ref, y_ref, st_ref, S = refs[17:]
    else:
        y_ref, st_ref, S = refs[17:]
    is_ctx, first, last, start = _flags(geo, rev)
    _init_state(S, h0_ref, is_ctx, start)
    dk = q_c.shape[1]

    q = _conv3_silu(q_c, q_p, q_n, cwq, None, first, last)
    k = _conv3_silu(k_c, k_p, k_n, cwk, None, first, last)
    v = _conv3_silu(v_c, v_p, v_n, cwv, None, first, last)
    q = q * lax.rsqrt(jnp.sum(q * q, axis=-1, keepdims=True) + 1e-6) * dk ** -0.5
    k = k * lax.rsqrt(jnp.sum(k * k, axis=-1, keepdims=True) + 1e-6)

    ab = ab_ref[...]
    gall = -jnp.exp(alog_ref[...]) * jax.nn.softplus(ab + dtb_ref[...])
    lane = lax.broadcasted_iota(jnp.int32, ab.shape, 1)
    sel = _mm_x(jnp.where(lane < 16, gall, jax.nn.sigmoid(ab)), xab_ref[0])
    beta = sel[:, 1:2]
    incl = _tri_mask(BLK, rev, chunk=CHUNK)
    strict = _tri_mask(BLK, rev, chunk=CHUNK, strict=True)
    cs = _mm_x(_as_bf16(incl), sel)[:, 0:1]
    cs_r = _mm_x(sel, _as_bf16(_tri_mask(BLK, not rev, chunk=CHUNK)), TN)[0:1, :]
    lmask = jnp.where(incl, jnp.exp(cs - cs_r), 0.0)

    kb = k * beta
    kbf = k.astype(bf16)
    m = jnp.where(strict, _mm(kb.astype(bf16), kbf, NT) * lmask, 0.0)
    r_i = lax.broadcasted_iota(jnp.int32, (BLK, BLK), 0)
    c_i = lax.broadcasted_iota(jnp.int32, (BLK, BLK), 1)
    p = -m
    tinv = jnp.where(r_i == c_i, 1.0, 0.0) + p
    for _ in range(5):
        p = _mm_hi(p, p)
        tinv = tinv + _mm_hi(tinv, p)
    tb = tinv.astype(bf16)
    ecs = jnp.exp(cs)
    u = _mm(tb, (v * beta).astype(bf16))
    wk = _mm(tb, (kb * ecs).astype(bf16)).astype(bf16)
    a_in = (_mm(q.astype(bf16), kbf, NT) * lmask).astype(bf16)
    qd = (q * ecs).astype(bf16)

    Sv = S[...]
    nch = BLK // CHUNK
    os_ = [None] * nch
    for c in (range(nch - 1, -1, -1) if rev else range(nch)):
        lo = c * CHUNK
        li = lo if rev else lo + CHUNK - 1
        cl = cs[li:li + 1, :]
        Sb = Sv.astype(bf16)
        v_new = (u[lo:lo + CHUNK] - _mm(wk[lo:lo + CHUNK], Sb)).astype(bf16)
        os_[c] = _mm(qd[lo:lo + CHUNK], Sb) + _mm(a_in[lo:lo + CHUNK, lo:lo + CHUNK], v_new)
        kd = k[lo:lo + CHUNK] * jnp.exp(cl - cs[lo:lo + CHUNK])
        Sv = Sv * jnp.exp(cl) + _mm(kd.astype(bf16), v_new, TN)
    o = jnp.concatenate(os_, axis=0)
    S[...] = Sv

    @pl.when(is_ctx)
    def _():
        st_ref[0, 0] = Sv

    if rev:
        _norm_gate_out(y_ref, yf_ref[...] + o, z_ref[...], nw_ref, gate_first=False)
    else:
        y_ref[...] = o


def _gdn_scan(proj, abp, conv_w, dtb, alog, norm_w, h0, yf, geo, rev):
    H = 8
    d = 1 if rev else 0
    xab = np.zeros((H, 128, 128), np.float32)
    for h in range(H):
        xab[h, d * H + h, 0] = 1.0
        xab[h, 2 * H + d * H + h, 1] = 1.0
    cols = [(128, lambda u: u), (128, lambda u: 8 + u), (256, lambda u: 8 + u)]
    in_specs, args = [], []
    for cw, fn in cols:
        in_specs += [_tok_spec(cw, fn, geo, rev), *_halo_specs(cw, fn, geo, rev)]
        args += [proj, proj, proj]
    for cw, fn in cols:
        in_specs.append(_param_spec(3, cw, fn))
        args.append(conv_w)
    in_specs += [_tok_spec(SMALL, lambda u: 0, geo, rev),
                 _param_spec(1, SMALL, lambda u: 0), _param_spec(1, SMALL, lambda u: 0),
                 pl.BlockSpec((1, 128, 128), lambda u, i: (u, 0, 0)),
                 _h0_spec(geo, rev)]
    args += [abp, dtb, alog, jnp.asarray(xab, bf16), h0]
    if rev:
        in_specs += [_tok_spec(256, lambda u: 16 + u, geo, rev), _param_spec(1, 256, lambda u: 0),
                     _tok_spec(256, lambda u: u, geo, rev)]
        args += [proj, norm_w, yf]
    return _scan_call(functools.partial(_gdn_kernel, rev=rev, geo=geo), "gdn_bwd" if rev else "gdn_fwd",
                      H, in_specs, args, bf16 if rev else f32, geo, rev, 2048)


def _pad_cols(w, n=SMALL):
    return jnp.pad(w.astype(f32), ((0, 0), (0, n - w.shape[1])))


def _pad_row(v, n=SMALL):
    v = v.reshape(1, -1).astype(f32)
    return jnp.pad(v, ((0, 0), (0, n - v.shape[1])))


def kernel(x_prompt, x_sample, state_ssm, state_gla, state_gdn, c, c_ctx, norm_mix_w, norm_ffn_w, ada_w, ada_b, ffn_in_w, ffn_conv_w, ffn_conv_b, ffn_out_w, ssm_in_w, ssm_conv_w, ssm_conv_b, ssm_dt_bias, ssm_a_log, ssm_d, ssm_norm_w, ssm_out_w, gla_in_w, gla_gate_w1, gla_gate_w2, gla_gate_b, gla_norm_w, gla_out_w, gdn_in_w, gdn_conv_w, gdn_dt_bias, gdn_a_log, gdn_norm_w, gdn_out_w, final_norm_w):
    Bc, SEQ, D = x_prompt.shape
    Bs, LS, _ = x_sample.shape
    geo = Geo(D=D, Tc=Bc * SEQ, Ts=Bs * LS, SEQ=SEQ, LS=LS, Bc=Bc, Bs=Bs)
    assert SEQ % BLK == 0 and LS % BLK == 0 and Bs < 8
    depth = ada_w.shape[0]
    G, P, N = SSM_GROUPS, SSM_HEADDIM, SSM_STATE

    x = jnp.concatenate([x_prompt.reshape(geo.Tc, D), x_sample.reshape(geo.Ts, D)], axis=0)
    cond8 = jnp.zeros((8, D), f32).at[0].set(c_ctx).at[1:1 + Bs].set(c)
    mod_all = _ada_table(cond8, ada_w, ada_b)[:, :1 + Bs].reshape(depth, 1 + Bs, 1, 6 * D)

    new_ssm, new_gla, new_gdn = [], [], []
    for i in range(depth):
        mod = mod_all[i]
        kind, j = i % 3, i // 3
        if kind == 0:
            w_in = ssm_in_w[j]
            di = ssm_out_w.shape[1]
            n_main = w_in.shape[1] - 2 * (di // P)
            proj, dtp = _in_proj(x, norm_mix_w[i], mod, 0, w_in[:, :n_main].astype(bf16),
                                 _pad_cols(w_in[:, n_main:]), geo)
            dtb, alog = _pad_row(ssm_dt_bias[j]), _pad_row(ssm_a_log[j])
            dskip = jnp.repeat(ssm_d[j], P).reshape(1, di)
            nw = ssm_norm_w[j].reshape(1, di)
            cw, cb = ssm_conv_w[j], ssm_conv_b[j].reshape(1, -1)
            h0 = state_ssm[:, j].reshape(Bs, 2, G, 4, N, P).transpose(1, 0, 2, 4, 3, 5).reshape(2, Bs, G, N, 4 * P)
            yf, st_f = _ssd_scan(proj, dtp, cw, cb, dtb, alog, None, None, h0[0], None, geo, False)
            u, st_b = _ssd_scan(proj, dtp, cw, cb, dtb, alog, dskip, nw, h0[1], yf, geo, True)
            st = jnp.stack([st_f, st_b], axis=1)
            st = st.reshape(Bc, 2, G, N, 4, P).transpose(0, 1, 2, 4, 3, 5).reshape(Bc, 2, G * 4, N, P)
            new_ssm.append(st)
            w_out = ssm_out_w[j]
        elif kind == 1:
            w_small = _pad_cols(jnp.concatenate([gla_gate_w1[j, 0], gla_gate_w1[j, 1]], axis=1))
            proj, lrp = _in_proj(x, norm_mix_w[i], mod, 0, gla_in_w[j].astype(bf16), w_small, geo)
            w2 = gla_gate_w2[j]
            gb = gla_gate_b[j].reshape(2, 1, -1)
            nw = gla_norm_w[j].reshape(1, -1)
            yf, st_f = _gla_scan(proj, lrp, w2, gb, None, state_gla[:, j, 0], None, geo, False)
            u, st_b = _gla_scan(proj, lrp, w2, gb, nw, state_gla[:, j, 1], yf, geo, True)
            new_gla.append(jnp.stack([st_f, st_b], axis=1))
            w_out = gla_out_w[j]
        else:
            w_in = gdn_in_w[j]
            n_main = w_in.shape[1] - 32
            proj, abp = _in_proj(x, norm_mix_w[i], mod, 0, w_in[:, :n_main].astype(bf16),
                                 _pad_cols(w_in[:, n_main:]), geo)
            dtb, alog = _pad_row(gdn_dt_bias[j]), _pad_row(gdn_a_log[j])
            nw = gdn_norm_w[j].reshape(1, -1)
            yf, st_f = _gdn_scan(proj, abp, gdn_conv_w[j], dtb, alog, None, state_gdn[:, j, 0], None, geo, False)
            u, st_b = _gdn_scan(proj, abp, gdn_conv_w[j], dtb, alog, nw, state_gdn[:, j, 1], yf, geo, True)
            new_gdn.append(jnp.stack([st_f, st_b], axis=1))
            w_out = gdn_out_w[j]
        x = _out_proj(u, w_out.astype(bf16), x, mod, 2, geo)

        proj, _ = _in_proj(x, norm_ffn_w[i], mod, 3, ffn_in_w[i].astype(bf16), None, geo)
        u = _ffn_conv(proj, ffn_conv_w[i].reshape(9, -1), ffn_conv_b[i], geo)
        x = _out_proj(u, ffn_out_w[i].astype(bf16), x, mod, 5, geo,
                      final_w=final_norm_w if i == depth - 1 else None)

    y_prompt = x[:geo.Tc].reshape(Bc, SEQ, D)
    y_sample = x[geo.Tc:].reshape(Bs, LS, D)
    return (y_prompt, y_sample, jnp.stack(new_ssm, axis=1), jnp.stack(new_gla, axis=1),
            jnp.stack(new_gdn, axis=1))
```

```python
import functools
from typing import NamedTuple

import numpy as np
import jax
import jax.numpy as jnp
from jax import lax
from jax.experimental import pallas as pl
from jax.experimental.pallas import tpu as pltpu

f32, bf16 = jnp.float32, jnp.bfloat16

BLK = 256
CHUNK = 64
SSD_CHUNK = 128
HALO = 16
GRID_W = 64
NORM_EPS = 1e-6
SMALL = 128
GLA_RANK = 16
GLA_NORMALIZER = 16.0
SSM_GROUPS = 8
SSM_HEADDIM = 64
SSM_STATE = 128
SSD_GP = 2
GDN_HP = 4
VMEM_LIMIT = 48 * 1024 * 1024

NN = (((1,), (0,)), ((), ()))
NT = (((1,), (1,)), ((), ()))
TN = (((0,), (0,)), ((), ()))


class Geo(NamedTuple):
    D: int
    Tc: int
    Ts: int
    SEQ: int
    LS: int
    Bc: int
    Bs: int

    @property
    def T(self):
        return self.Tc + self.Ts

    @property
    def nb(self):
        return self.T // BLK

    @property
    def ncb(self):
        return self.Tc // BLK

    @property
    def bpc(self):
        return self.SEQ // BLK

    @property
    def bps(self):
        return self.LS // BLK


def _mm(a, b, dims=NN):
    return lax.dot_general(a, b, dims, preferred_element_type=f32)


def _split3(a):
    hi = a.astype(bf16)
    r1 = a - hi.astype(f32)
    mid = r1.astype(bf16)
    lo = (r1 - mid.astype(f32)).astype(bf16)
    return hi, mid, lo


def _mm_x(a, b, dims=NN):
    if isinstance(a, tuple):
        ps = [_mm(p, b, dims) for p in a]
    elif a.dtype == bf16:
        ps = [_mm(a, p, dims) for p in _split3(b)]
    else:
        ps = [_mm(p, b, dims) for p in _split3(a)]
    return ps[0] + ps[1] + ps[2]


def _mm_hi(a, b):
    ah = a.astype(bf16)
    al = (a - ah.astype(f32)).astype(bf16)
    bh = b.astype(bf16)
    bl = (b - bh.astype(f32)).astype(bf16)
    return _mm(ah, bh) + _mm(al, bh) + _mm(ah, bl)


def _silu(x):
    return x * jax.nn.sigmoid(x)


def _tri_mask(n, upper, chunk=None, strict=False):
    r = lax.broadcasted_iota(jnp.int32, (n, n), 0)
    c = lax.broadcasted_iota(jnp.int32, (n, n), 1)
    if upper:
        m = (c > r) if strict else (c >= r)
    else:
        m = (c < r) if strict else (c <= r)
    if chunk is not None and chunk < n:
        sh = int(np.log2(chunk))
        same = lax.shift_right_logical(r, sh) == lax.shift_right_logical(c, sh)
        m = jnp.logical_and(m, same)
    return m


def _as_bf16(mask):
    return jnp.where(mask, 1.0, 0.0).astype(bf16)


def _ada_kernel(c_ref, w_ref, b_ref, o_ref):
    o_ref[0] = _mm_hi(_silu(c_ref[...]), w_ref[0]) + b_ref[0]


def _ada_table(cond8, ada_w, ada_b):
    depth, D, N6 = ada_w.shape
    tn = N6 // 4
    return pl.pallas_call(
        _ada_kernel,
        out_shape=jax.ShapeDtypeStruct((depth, 8, N6), f32),
        grid=(depth, N6 // tn),
        in_specs=[pl.BlockSpec((8, D), lambda l, j: (0, 0)),
                  pl.BlockSpec((1, D, tn), lambda l, j: (l, 0, j)),
                  pl.BlockSpec((1, 1, tn), lambda l, j: (l, 0, j))],
        out_specs=pl.BlockSpec((1, 8, tn), lambda l, j: (l, 0, j)),
        compiler_params=pltpu.CompilerParams(
            dimension_semantics=("parallel", "parallel"), vmem_limit_bytes=VMEM_LIMIT),
        name="ada_table",
    )(cond8, ada_w, ada_b.reshape(depth, 1, N6))


def _in_proj_kernel(*refs, has_small):
    if has_small:
        x_ref, nw_ref, sh_ref, sc_ref, w_ref, ws_ref, o_ref, os_ref, h_ref = refs
    else:
        x_ref, nw_ref, sh_ref, sc_ref, w_ref, o_ref, h_ref = refs

    @pl.when(pl.program_id(1) == 0)
    def _():
        x = x_ref[...]
        h = x * lax.rsqrt(jnp.mean(x * x, axis=-1, keepdims=True) + NORM_EPS) * nw_ref[...]
        h = h * (1.0 + sc_ref[0]) + sh_ref[0]
        hb = h.astype(bf16)
        h_ref[...] = hb
        if has_small:
            ws = ws_ref[...]
            wh = ws.astype(bf16)
            wl = (ws - wh.astype(f32)).astype(bf16)
            hl = (h - hb.astype(f32)).astype(bf16)
            os_ref[...] = _mm(hb, wh) + _mm(hl, wh) + _mm(hb, wl)

    o_ref[...] = _mm(h_ref[...], w_ref[...]).astype(o_ref.dtype)


def _row_tile(geo, pref):
    for tm in (pref, 512, 256):
        if tm <= pref and geo.Tc % tm == 0 and geo.LS % tm == 0:
            return tm
    raise ValueError("token counts must be multiples of 256")


def _cond_of_tile(i, geo, tm):
    nct = geo.Tc // tm
    return jnp.where(i < nct, 0, 1 + jnp.maximum(i - nct, 0) // (geo.LS // tm))


def _in_proj(x, norm_w, mod, shift_idx, w_main, w_small, geo, tn=512):
    T, D = x.shape
    N = w_main.shape[1]
    tm = _row_tile(geo, 1024)
    assert N % tn == 0
    has_small = w_small is not None
    cond = functools.partial(_cond_of_tile, geo=geo, tm=tm)
    in_specs = [pl.BlockSpec((tm, D), lambda i, j: (i, 0)),
                pl.BlockSpec((1, D), lambda i, j: (0, 0)),
                pl.BlockSpec((1, 1, D), lambda i, j: (cond(i), 0, shift_idx)),
                pl.BlockSpec((1, 1, D), lambda i, j: (cond(i), 0, shift_idx + 1)),
                pl.BlockSpec((D, tn), lambda i, j: (0, j))]
    args = [x, norm_w.reshape(1, D), mod, mod, w_main]
    out_shape = [jax.ShapeDtypeStruct((T, N), bf16)]
    out_specs = [pl.BlockSpec((tm, tn), lambda i, j: (i, j))]
    if has_small:
        in_specs.append(pl.BlockSpec((D, SMALL), lambda i, j: (0, 0)))
        args.append(w_small)
        out_shape.append(jax.ShapeDtypeStruct((T, SMALL), f32))
        out_specs.append(pl.BlockSpec((tm, SMALL), lambda i, j: (i, 0)))
    res = pl.pallas_call(
        functools.partial(_in_proj_kernel, has_small=has_small),
        out_shape=out_shape,
        grid=(T // tm, N // tn),
        in_specs=in_specs,
        out_specs=out_specs,
        scratch_shapes=[pltpu.VMEM((tm, D), bf16)],
        compiler_params=pltpu.CompilerParams(
            dimension_semantics=("parallel", "arbitrary"), vmem_limit_bytes=VMEM_LIMIT),
        name="in_proj",
    )(*args)
    return res if has_small else (res[0], None)


def _out_proj_kernel(*refs, final):
    if final:
        u_ref, w_ref, x_ref, g_ref, fw_ref, o_ref = refs
    else:
        u_ref, w_ref, x_ref, g_ref, o_ref = refs
    r = x_ref[...] + g_ref[0] * _mm(u_ref[...], w_ref[...])
    if final:
        r = r * lax.rsqrt(jnp.mean(r * r, axis=-1, keepdims=True) + NORM_EPS) * fw_ref[...]
    o_ref[...] = r


def _out_proj(u, w, x, mod, gate_idx, geo, final_w=None):
    T, K = u.shape
    D = w.shape[1]
    tm = _row_tile(geo, 512)
    cond = functools.partial(_cond_of_tile, geo=geo, tm=tm)
    final = final_w is not None
    in_specs = [pl.BlockSpec((tm, K), lambda i: (i, 0)),
                pl.BlockSpec((K, D), lambda i: (0, 0)),
                pl.BlockSpec((tm, D), lambda i: (i, 0)),
                pl.BlockSpec((1, 1, D), lambda i: (cond(i), 0, gate_idx))]
    args = [u, w, x, mod]
    if final:
        in_specs.append(pl.BlockSpec((1, D), lambda i: (0, 0)))
        args.append(final_w.reshape(1, D))
    return pl.pallas_call(
        functools.partial(_out_proj_kernel, final=final),
        out_shape=jax.ShapeDtypeStruct((T, D), f32),
        grid=(T // tm,),
        in_specs=in_specs,
        out_specs=pl.BlockSpec((tm, D), lambda i: (i, 0)),
        compiler_params=pltpu.CompilerParams(
            dimension_semantics=("parallel",), vmem_limit_bytes=VMEM_LIMIT),
        name="out_proj",
    )(*args)


def _ffn_conv_kernel(a_ref, v_ref, w_ref, b_ref, o_ref, pad_ref, *, n_ctx_img, IB, SEQ):
    PADR = 128
    tf = a_ref.shape[1]
    pad_ref[0:PADR, :] = jnp.zeros((PADR, tf), f32)
    pad_ref[PADR + IB:PADR + IB + PADR, :] = jnp.zeros((PADR, tf), f32)
    pad_ref[PADR:PADR + IB, :] = a_ref[...].astype(f32)
    w = w_ref[...]
    b = b_ref[...]
    row = lax.broadcasted_iota(jnp.int32, (BLK, tf), 0)

    def taps(s, pos, width, kr):
        left = jnp.where(pos > 0, pad_ref[s - 1:s - 1 + BLK, :], 0.0)
        right = jnp.where(pos < width - 1, pad_ref[s + 1:s + 1 + BLK, :], 0.0)
        return (left * w[3 * kr:3 * kr + 1] + pad_ref[s:s + BLK, :] * w[3 * kr + 1:3 * kr + 2]
                + right * w[3 * kr + 2:3 * kr + 3])

    def finish(c, acc):
        v = v_ref[c * BLK:(c + 1) * BLK, :].astype(f32)
        o_ref[c * BLK:(c + 1) * BLK, :] = (_silu(acc + b) * v).astype(bf16)

    @pl.when(pl.program_id(0) < n_ctx_img)
    def _():
        for c in range(IB // BLK):
            pos = jnp.bitwise_and(row + (c * BLK) % SEQ, SEQ - 1)
            finish(c, taps(PADR + c * BLK, pos, SEQ, 1))

    @pl.when(pl.program_id(0) >= n_ctx_img)
    def _():
        col = jnp.bitwise_and(row, GRID_W - 1)
        for c in range(IB // BLK):
            s = PADR + c * BLK
            acc = taps(s - GRID_W, col, GRID_W, 0) + taps(s, col, GRID_W, 1) + taps(s + GRID_W, col, GRID_W, 2)
            finish(c, acc)


def _ffn_conv(proj, conv_w9, conv_b, geo, tf=256):
    T, F2 = proj.shape
    F = F2 // 2
    IB = geo.LS
    assert F % tf == 0 and geo.Tc % IB == 0 and IB % BLK == 0
    assert geo.SEQ & (geo.SEQ - 1) == 0 and BLK % GRID_W == 0
    nf = F // tf
    return pl.pallas_call(
        functools.partial(_ffn_conv_kernel, n_ctx_img=geo.Tc // IB, IB=IB, SEQ=geo.SEQ),
        out_shape=jax.ShapeDtypeStruct((T, F), bf16),
        grid=(T // IB, nf),
        in_specs=[pl.BlockSpec((IB, tf), lambda i, j: (i, j)),
                  pl.BlockSpec((IB, tf), lambda i, j: (i, nf + j)),
                  pl.BlockSpec((9, tf), lambda i, j: (0, j)),
                  pl.BlockSpec((1, tf), lambda i, j: (0, j))],
        out_specs=pl.BlockSpec((IB, tf), lambda i, j: (i, j)),
        scratch_shapes=[pltpu.VMEM((IB + 256, tf), f32)],
        compiler_params=pltpu.CompilerParams(
            dimension_semantics=("parallel", "parallel"), vmem_limit_bytes=VMEM_LIMIT),
        name="ffn_conv",
    )(proj, proj, conv_w9, conv_b.reshape(1, F))


def _blk_idx(i, geo, rev):
    return (geo.nb - 1 - i) if rev else i


def _tok_spec(cw, colfn, geo, rev):
    return pl.BlockSpec((BLK, cw), lambda u, i: (_blk_idx(i, geo, rev), colfn(u)))


def _halo_specs(cw, colfn, geo, rev):
    per = BLK // HALO
    last = geo.T // HALO - 1
    prev = pl.BlockSpec((HALO, cw), lambda u, i: (jnp.maximum(_blk_idx(i, geo, rev) * per - 1, 0), colfn(u)))
    nxt = pl.BlockSpec((HALO, cw), lambda u, i: (jnp.minimum((_blk_idx(i, geo, rev) + 1) * per, last), colfn(u)))
    return prev, nxt


def _h0_spec(geo, rev, hp):
    def idx(u, i):
        ip = _blk_idx(i, geo, rev)
        return (jnp.maximum(ip - geo.ncb, 0) // geo.bps, u, 0, 0)
    return pl.BlockSpec((1, hp, 128, 256), idx)


def _st_spec(geo, rev, hp):
    def idx(u, i):
        ip = _blk_idx(i, geo, rev)
        return (jnp.minimum(ip // geo.bpc, geo.Bc - 1), u, 0, 0)
    return pl.BlockSpec((1, hp, 128, 256), idx)


def _param_spec(rows, cw, colfn):
    return pl.BlockSpec((rows, cw), lambda u, i: (0, colfn(u)))


def _flags(geo, rev):
    ip = _blk_idx(pl.program_id(1), geo, rev)
    is_ctx = ip < geo.ncb
    pos = jnp.where(is_ctx, lax.rem(ip, geo.bpc), lax.rem(jnp.maximum(ip - geo.ncb, 0), geo.bps))
    n = jnp.where(is_ctx, geo.bpc, geo.bps)
    first = pos == 0
    last = pos == n - 1
    return is_ctx, first, last, (last if rev else first)


def _conv3_silu(cur_ref, prev_ref, next_ref, w_ref, b_ref, first, last):
    cur = cur_ref[...].astype(f32)
    n = cur.shape[0]
    row = lax.broadcasted_iota(jnp.int32, cur.shape, 0)
    pr = jnp.where(first, 0.0, prev_ref[HALO - 1:HALO, :].astype(f32))
    nx = jnp.where(last, 0.0, next_ref[0:1, :].astype(f32))
    xm = jnp.where(row == 0, pr, pltpu.roll(cur, 1, axis=0))
    xp = jnp.where(row == n - 1, nx, pltpu.roll(cur, n - 1, axis=0))
    w = w_ref[...]
    y = xm * w[0:1] + cur * w[1:2] + xp * w[2:3]
    if b_ref is not None:
        y = y + b_ref[...]
    return _silu(y)


def _init_state(S, h0_ref, is_ctx, start):
    @pl.when(start)
    def _():
        S[...] = jnp.where(is_ctx, 0.0, h0_ref[0])


def _norm_gate(o, gate, nw, gate_first):
    g = _silu(gate)
    if gate_first:
        o = o * g
    o = o * lax.rsqrt(jnp.mean(o * o, axis=-1, keepdims=True) + NORM_EPS) * nw
    if not gate_first:
        o = o * g
    return o


def _scan_call(kernel, name, grid_u, in_specs, args, y_dtype, geo, rev, ycols, hp):
    T = geo.T
    return pl.pallas_call(
        kernel,
        out_shape=[jax.ShapeDtypeStruct((T, ycols), y_dtype),
                   jax.ShapeDtypeStruct((geo.Bc, grid_u * hp, 128, 256), f32)],
        grid=(grid_u, geo.nb),
        in_specs=in_specs,
        out_specs=[_tok_spec(256 * hp, lambda u: u, geo, rev), _st_spec(geo, rev, hp)],
        scratch_shapes=[pltpu.VMEM((hp, 128, 256), f32)],
        compiler_params=pltpu.CompilerParams(
            dimension_semantics=("parallel", "arbitrary"), vmem_limit_bytes=VMEM_LIMIT),
        name=name,
    )(*args)


def _ssd_kernel(*refs, rev, geo, gp):
    (xbc_c, xbc_p, xbc_n, dt_ref, cw_ref, cb_ref, dtb_ref, alog_ref, xsel_ref, xh_ref, h0_ref) = refs[:11]
    if rev:
        dsk_ref, z_ref, nw_ref, yf_ref, y_ref, st_ref, S = refs[11:]
    else:
        y_ref, st_ref, S = refs[11:]
    is_ctx, first, last, start = _flags(geo, rev)
    _init_state(S, h0_ref, is_ctx, start)
    Q = SSD_CHUNK
    nch = BLK // Q
    groups = range(gp)

    xbc = _conv3_silu(xbc_c, xbc_p, xbc_n, cw_ref, cb_ref, first, last)
    dt = jax.nn.softplus(dt_ref[...] + dtb_ref[...])
    la = dt * (-jnp.exp(alog_ref[...]))
    tri = _tri_mask(BLK, rev, chunk=Q)
    cs3 = _split3(_mm_x(_as_bf16(tri), la))
    dtb16 = dt.astype(bf16)
    tri_q = _tri_mask(Q, rev)

    xs, xvs, bms, cms, cses, csts = [], [], [], [], [], []
    for g in groups:
        x = xbc[:, g * 512:g * 512 + 256]
        xs.append(x)
        bms.append(xbc[:, g * 512 + 256:g * 512 + 384].astype(bf16))
        cms.append(xbc[:, g * 512 + 384:g * 512 + 512].astype(bf16))
        xsel = xsel_ref[g]
        xvs.append(x * _mm(dtb16, xsel))
        cses.append(_mm_x(cs3, xsel))
        csts.append(_mm_x(cs3, xh_ref[g]).T)
    Svs = [S[g] for g in groups]
    youts = [[None] * nch for _ in groups]
    for c in (range(nch - 1, -1, -1) if rev else range(nch)):
        lo = c * Q
        li = lo if rev else lo + Q - 1
        for g in groups:
            cs_e = cses[g][lo:lo + Q]
            xv = xvs[g][lo:lo + Q]
            xvb = xv.astype(bf16)
            cmb = cms[g][lo:lo + Q]
            bmb = bms[g][lo:lo + Q]
            gm = _mm(cmb, bmb, NT)
            ys = []
            for r in range(4):
                seg = cs_e[:, r * 64:r * 64 + 1] - csts[g][r:r + 1, lo:lo + Q]
                m = jnp.where(tri_q, gm * jnp.exp(seg), 0.0)
                ys.append(_mm(m.astype(bf16), xvb[:, r * 64:(r + 1) * 64]))
            Sv = Svs[g]
            y = jnp.concatenate(ys, axis=1) + _mm(cmb, Sv.astype(bf16)) * jnp.exp(cs_e)
            cl = cses[g][li:li + 1, :]
            xw = xv * jnp.exp(cl - cs_e)
            Svs[g] = Sv * jnp.exp(cl) + _mm(bmb, xw.astype(bf16), TN)
            youts[g][c] = y

    for g in groups:
        S[g] = Svs[g]
        y = jnp.concatenate(youts[g], axis=0)
        if rev:
            cols = slice(g * 256, (g + 1) * 256)
            o = yf_ref[:, cols] + y + xs[g] * dsk_ref[:, cols]
            y = _norm_gate(o, z_ref[:, cols].astype(f32), nw_ref[:, cols], gate_first=True)
        y_ref[:, g * 256:(g + 1) * 256] = y.astype(y_ref.dtype)

    @pl.when(is_ctx)
    def _():
        for g in groups:
            st_ref[0, g] = Svs[g]


def _ssd_scan(proj, dtp, conv_w, conv_b, dtb, alog, dskip, norm_w, h0, yf, geo, rev):
    G, gp = SSM_GROUPS, SSD_GP
    nu = G // gp
    d = 1 if rev else 0
    xsel = np.zeros((G, 128, 256), np.float32)
    xh = np.zeros((G, 128, 128), np.float32)
    for g in range(G):
        for r in range(4):
            xsel[g, d * 32 + 4 * g + r, r * 64:(r + 1) * 64] = 1.0
            xh[g, d * 32 + 4 * g + r, r] = 1.0
    cw = 512 * gp
    in_specs = [_tok_spec(cw, lambda u: u, geo, rev), *_halo_specs(cw, lambda u: u, geo, rev),
                _tok_spec(SMALL, lambda u: 0, geo, rev),
                _param_spec(3, cw, lambda u: u), _param_spec(1, cw, lambda u: u),
                _param_spec(1, SMALL, lambda u: 0), _param_spec(1, SMALL, lambda u: 0),
                pl.BlockSpec((gp, 128, 256), lambda u, i: (u, 0, 0)),
                pl.BlockSpec((gp, 128, 128), lambda u, i: (u, 0, 0)),
                _h0_spec(geo, rev, gp)]
    args = [proj, proj, proj, dtp, conv_w, conv_b, dtb, alog,
            jnp.asarray(xsel, bf16), jnp.asarray(xh, bf16), h0]
    if rev:
        zw = 256 * gp
        in_specs += [_param_spec(1, zw, lambda u: u), _tok_spec(zw, lambda u: 2 * nu + u, geo, rev),
                     _param_spec(1, zw, lambda u: u), _tok_spec(zw, lambda u: u, geo, rev)]
        args += [dskip, proj, norm_w, yf]
    return _scan_call(functools.partial(_ssd_kernel, rev=rev, geo=geo, gp=gp), "ssd_bwd" if rev else "ssd_fwd",
                      nu, in_specs, args, bf16 if rev else f32, geo, rev, 2048, gp)


def _gla_kernel(*refs, rev, geo, nh):
    qkv_ref, lr_ref, w2_ref, gb_ref, h0_ref = refs[:5]
    if rev:
        r_ref, nw_ref, yf_ref, y_ref, st_ref, S = refs[5:]
    else:
        y_ref, st_ref, S = refs[5:]
    is_ctx, first, last, start = _flags(geo, rev)
    _init_state(S, h0_ref, is_ctx, start)
    d = 1 if rev else 0
    dk, dv = 128, 256
    heads = range(nh)
    K = nh * dk

    q_all = qkv_ref[:, 0:K].astype(f32) * dk ** -0.5
    k_all = qkv_ref[:, K:2 * K].astype(f32)
    vb_all = qkv_ref[:, 2 * K:2 * K + nh * dv]
    lr = lr_ref[...][:, d * GLA_RANK:(d + 1) * GLA_RANK]
    logit = _mm_hi(lr, w2_ref[0]) + gb_ref[0]
    lg = jax.nn.log_sigmoid(logit) * (1.0 / GLA_NORMALIZER)
    tri = _tri_mask(BLK, rev, chunk=CHUNK)
    cs = _mm_x(_as_bf16(tri), lg)
    cs_t = _mm_x(lg, _as_bf16(_tri_mask(BLK, not rev, chunk=CHUNK)), TN)
    qgb = (q_all * jnp.exp(cs)).astype(bf16)
    kgb = (k_all * jnp.exp(-cs)).astype(bf16)

    ys = []
    for h in heads:
        sc = jnp.where(tri, _mm(qgb[:, h * dk:(h + 1) * dk], kgb[:, h * dk:(h + 1) * dk], NT), 0.0)
        ys.append(_mm(sc.astype(bf16), vb_all[:, h * dv:(h + 1) * dv]))
    Svs = [S[h] for h in heads]
    nch = BLK // CHUNK
    yin = [[None] * nch for _ in heads]
    for c in (range(nch - 1, -1, -1) if rev else range(nch)):
        lo = c * CHUNK
        li = lo if rev else lo + CHUNK - 1
        kw_all = (k_all[lo:lo + CHUNK] * jnp.exp(cs[li:li + 1, :] - cs[lo:lo + CHUNK])).astype(bf16)
        for h in heads:
            yin[h][c] = _mm(qgb[lo:lo + CHUNK, h * dk:(h + 1) * dk], Svs[h].astype(bf16))
        for h in heads:
            Svs[h] = (Svs[h] * jnp.exp(cs_t[h * dk:(h + 1) * dk, li:li + 1])
                      + _mm(kw_all[:, h * dk:(h + 1) * dk], vb_all[lo:lo + CHUNK, h * dv:(h + 1) * dv], TN))

    for h in heads:
        S[h] = Svs[h]
        y = ys[h] + jnp.concatenate(yin[h], axis=0)
        cols = slice(h * dv, (h + 1) * dv)
        if rev:
            y = _norm_gate(yf_ref[:, cols] + y, r_ref[:, cols].astype(f32), nw_ref[...], gate_first=False)
        y_ref[:, cols] = y.astype(y_ref.dtype)

    @pl.when(is_ctx)
    def _():
        for h in heads:
            st_ref[0, h] = Svs[h]


def _gla_scan(proj, lrp, w2, gb, norm_w, h0, yf, geo, rev):
    H = 4
    d = 1 if rev else 0
    in_specs = [_tok_spec(2048, lambda u: 0, geo, rev), _tok_spec(SMALL, lambda u: 0, geo, rev),
                pl.BlockSpec((1, GLA_RANK, 512), lambda u, i: (d, 0, 0)),
                pl.BlockSpec((1, 1, 512), lambda u, i: (d, 0, 0)),
                _h0_spec(geo, rev, H)]
    args = [proj, lrp, w2, gb, h0]
    if rev:
        in_specs += [_tok_spec(1024, lambda u: 2, geo, rev), _param_spec(1, 256, lambda u: 0),
                     _tok_spec(1024, lambda u: 0, geo, rev)]
        args += [proj, norm_w, yf]
    return _scan_call(functools.partial(_gla_kernel, rev=rev, geo=geo, nh=H), "gla_bwd" if rev else "gla_fwd",
                      1, in_specs, args, bf16 if rev else f32, geo, rev, 1024, H)


def _unit_tri_inverse(ms, lev):
    eye = jnp.where(lev == -1, 1.0, 0.0)
    dinv = [eye - jnp.where(lev == 0, m, 0.0) for m in ms]
    for b in range(1, 6):
        ts = [_mm(jnp.where(lev == b, m, 0.0).astype(bf16), d.astype(bf16)).astype(bf16)
              for m, d in zip(ms, dinv)]
        dinv = [d - _mm(d.astype(bf16), t) for d, t in zip(dinv, ts)]
    return dinv


def _gdn_kernel(*refs, rev, geo, hp):
    (qkv_c, qkv_p, qkv_n, cw_ref, ab_ref, dtb_ref, alog_ref, xab_ref, lev_ref, h0_ref) = refs[:10]
    if rev:
        z_ref, nw_ref, yf_ref, y_ref, st_ref, S = refs[10:]
    else:
        y_ref, st_ref, S = refs[10:]
    is_ctx, first, last, start = _flags(geo, rev)
    _init_state(S, h0_ref, is_ctx, start)
    dk, dv = 128, 256
    heads = range(hp)

    qkv = _conv3_silu(qkv_c, qkv_p, qkv_n, cw_ref, None, first, last)

    ab = ab_ref[...]
    gall = -jnp.exp(alog_ref[...]) * jax.nn.softplus(ab + dtb_ref[...])
    lane = lax.broadcasted_iota(jnp.int32, ab.shape, 1)
    sel = _mm_x(jnp.where(lane < 16, gall, jax.nn.sigmoid(ab)), xab_ref[0])
    lev = lev_ref[...]
    r_i = lax.broadcasted_iota(jnp.int32, (BLK, BLK), 0)
    c_i = lax.broadcasted_iota(jnp.int32, (BLK, BLK), 1)
    in_chunk = lev < 6
    incl = jnp.logical_and(in_chunk, (c_i >= r_i) if rev else (c_i <= r_i))
    incl_t = jnp.logical_and(in_chunk, (c_i <= r_i) if rev else (c_i >= r_i))
    strict = jnp.logical_and(incl, lev >= 0)
    cs_all = _mm_x(_as_bf16(incl), sel)
    cs_rows = _mm_x(sel, _as_bf16(incl_t), TN)

    qs, ks, kbfs, kbs, vbs, css, lmasks, ms = [], [], [], [], [], [], [], []
    for hh in heads:
        q = qkv[:, hh * dk:(hh + 1) * dk]
        k = qkv[:, (hp + hh) * dk:(hp + hh + 1) * dk]
        v = qkv[:, 2 * hp * dk + hh * dv:2 * hp * dk + (hh + 1) * dv]
        q = q * lax.rsqrt(jnp.sum(q * q, axis=-1, keepdims=True) + 1e-6) * dk ** -0.5
        k = k * lax.rsqrt(jnp.sum(k * k, axis=-1, keepdims=True) + 1e-6)
        beta = sel[:, 2 * hh + 1:2 * hh + 2]
        cs = cs_all[:, 2 * hh:2 * hh + 1]
        lmask = jnp.where(incl, jnp.exp(cs - cs_rows[2 * hh:2 * hh + 1, :]), 0.0)
        kb = k * beta
        kbf = k.astype(bf16)
        qs.append(q)
        ks.append(k)
        kbfs.append(kbf)
        kbs.append(kb)
        vbs.append((v * beta).astype(bf16))
        css.append(cs)
        lmasks.append(lmask)
        ms.append(jnp.where(strict, _mm(kb.astype(bf16), kbf, NT) * lmask, 0.0))
    tbs = [t.astype(bf16) for t in _unit_tri_inverse(ms, lev)]
    ecs = [jnp.exp(cs) for cs in css]
    us = [_mm(tbs[h], vbs[h]) for h in heads]
    wks = [_mm(tbs[h], (kbs[h] * ecs[h]).astype(bf16)).astype(bf16) for h in heads]
    a_ins = [(_mm(qs[h].astype(bf16), kbfs[h], NT) * lmasks[h]).astype(bf16) for h in heads]
    qds = [(qs[h] * ecs[h]).astype(bf16) for h in heads]

    Svs = [S[h] for h in heads]
    nch = BLK // CHUNK
    outs = [[None] * nch for _ in heads]
    for c in (range(nch - 1, -1, -1) if rev else range(nch)):
        lo = c * CHUNK
        li = lo if rev else lo + CHUNK - 1
        Sbs = [Sv.astype(bf16) for Sv in Svs]
        v_news = [(us[h][lo:lo + CHUNK] - _mm(wks[h][lo:lo + CHUNK], Sbs[h])).astype(bf16) for h in heads]
        for h in heads:
            outs[h][c] = (_mm(qds[h][lo:lo + CHUNK], Sbs[h])
                          + _mm(a_ins[h][lo:lo + CHUNK, lo:lo + CHUNK], v_news[h]))
        for h in heads:
            cl = css[h][li:li + 1, :]
            kd = ks[h][lo:lo + CHUNK] * jnp.exp(cl - css[h][lo:lo + CHUNK])
            Svs[h] = Svs[h] * jnp.exp(cl) + _mm(kd.astype(bf16), v_news[h], TN)

    for hh in heads:
        S[hh] = Svs[hh]
        o = jnp.concatenate(outs[hh], axis=0)
        cols = slice(hh * dv, (hh + 1) * dv)
        if rev:
            o = _norm_gate(yf_ref[:, cols] + o, z_ref[:, cols].astype(f32), nw_ref[...], gate_first=False)
        y_ref[:, cols] = o.astype(y_ref.dtype)

    @pl.when(is_ctx)
    def _():
        for hh in heads:
            st_ref[0, hh] = Svs[hh]


def _gdn_scan(proj, abp, conv_w, dtb, alog, norm_w, h0, yf, geo, rev):
    H, hp = 8, GDN_HP
    nu = H // hp
    d = 1 if rev else 0
    xab = np.zeros((nu, 128, 128), np.float32)
    for h in range(H):
        xab[h // hp, d * H + h, 2 * (h % hp)] = 1.0
        xab[h // hp, 2 * H + d * H + h, 2 * (h % hp) + 1] = 1.0
    idx = np.arange(BLK)
    x = idx[:, None] ^ idx[None, :]
    lev = np.where(x == 0, -1, np.floor(np.log2(np.maximum(x, 1)))).astype(np.int32)
    cw, vw = 512 * hp, 256 * hp
    in_specs = [_tok_spec(cw, lambda u: u, geo, rev), *_halo_specs(cw, lambda u: u, geo, rev),
                _param_spec(3, cw, lambda u: u),
                _tok_spec(SMALL, lambda u: 0, geo, rev),
                _param_spec(1, SMALL, lambda u: 0), _param_spec(1, SMALL, lambda u: 0),
                pl.BlockSpec((1, 128, 128), lambda u, i: (u, 0, 0)),
                pl.BlockSpec((BLK, BLK), lambda u, i: (0, 0)),
                _h0_spec(geo, rev, hp)]
    args = [proj, proj, proj, conv_w, abp, dtb, alog, jnp.asarray(xab, bf16), jnp.asarray(lev), h0]
    if rev:
        in_specs += [_tok_spec(vw, lambda u: 2 * nu + u, geo, rev), _param_spec(1, 256, lambda u: 0),
                     _tok_spec(vw, lambda u: u, geo, rev)]
        args += [proj, norm_w, yf]
    return _scan_call(functools.partial(_gdn_kernel, rev=rev, geo=geo, hp=hp), "gdn_bwd" if rev else "gdn_fwd",
                      nu, in_specs, args, bf16 if rev else f32, geo, rev, 2048, hp)


def _pad_cols(w, n=SMALL):
    return jnp.pad(w.astype(f32), ((0, 0), (0, n - w.shape[1])))


def _pad_row(v, n=SMALL):
    v = v.reshape(1, -1).astype(f32)
    return jnp.pad(v, ((0, 0), (0, n - v.shape[1])))


def _ssd_perm():
    perm = []
    for g in range(SSM_GROUPS):
        perm += list(range(g * 256, (g + 1) * 256))
        perm += list(range(2048 + g * 128, 2048 + (g + 1) * 128))
        perm += list(range(3072 + g * 128, 3072 + (g + 1) * 128))
    return np.asarray(perm, np.int32)


def _gdn_perm(hp):
    perm = []
    for u in range(8 // hp):
        perm += list(range(u * hp * 128, (u + 1) * hp * 128))
        perm += list(range(1024 + u * hp * 128, 1024 + (u + 1) * hp * 128))
        perm += list(range(2048 + u * hp * 256, 2048 + (u + 1) * hp * 256))
    return np.asarray(perm, np.int32)


def kernel(x_prompt, x_sample, state_ssm, state_gla, state_gdn, c, c_ctx, norm_mix_w, norm_ffn_w, ada_w, ada_b, ffn_in_w, ffn_conv_w, ffn_conv_b, ffn_out_w, ssm_in_w, ssm_conv_w, ssm_conv_b, ssm_dt_bias, ssm_a_log, ssm_d, ssm_norm_w, ssm_out_w, gla_in_w, gla_gate_w1, gla_gate_w2, gla_gate_b, gla_norm_w, gla_out_w, gdn_in_w, gdn_conv_w, gdn_dt_bias, gdn_a_log, gdn_norm_w, gdn_out_w, final_norm_w):
    Bc, SEQ, D = x_prompt.shape
    Bs, LS, _ = x_sample.shape
    geo = Geo(D=D, Tc=Bc * SEQ, Ts=Bs * LS, SEQ=SEQ, LS=LS, Bc=Bc, Bs=Bs)
    assert SEQ % BLK == 0 and LS % BLK == 0 and Bs < 8
    depth = ada_w.shape[0]
    G, P, N = SSM_GROUPS, SSM_HEADDIM, SSM_STATE

    x = jnp.concatenate([x_prompt.reshape(geo.Tc, D), x_sample.reshape(geo.Ts, D)], axis=0)
    cond8 = jnp.zeros((8, D), f32).at[0].set(c_ctx).at[1:1 + Bs].set(c)
    mod_all = _ada_table(cond8, ada_w, ada_b)[:, :1 + Bs].reshape(depth, 1 + Bs, 1, 6 * D)

    new_ssm, new_gla, new_gdn = [], [], []
    for i in range(depth):
        mod = mod_all[i]
        kind, j = i % 3, i // 3
        if kind == 0:
            w_in = ssm_in_w[j]
            di = ssm_out_w.shape[1]
            perm = _ssd_perm()
            w_main = jnp.concatenate([w_in[:, di:di + 4096][:, perm], w_in[:, :di]], axis=1).astype(bf16)
            proj, dtp = _in_proj(x, norm_mix_w[i], mod, 0, w_main, _pad_cols(w_in[:, di + 4096:]), geo)
            dtb, alog = _pad_row(ssm_dt_bias[j]), _pad_row(ssm_a_log[j])
            dskip = jnp.repeat(ssm_d[j], P).reshape(1, di)
            nw = ssm_norm_w[j].reshape(1, di)
            cw, cb = ssm_conv_w[j][:, perm], ssm_conv_b[j][perm].reshape(1, -1)
            h0 = state_ssm[:, j].reshape(Bs, 2, G, 4, N, P).transpose(1, 0, 2, 4, 3, 5).reshape(2, Bs, G, N, 4 * P)
            yf, st_f = _ssd_scan(proj, dtp, cw, cb, dtb, alog, None, None, h0[0], None, geo, False)
            u, st_b = _ssd_scan(proj, dtp, cw, cb, dtb, alog, dskip, nw, h0[1], yf, geo, True)
            st = jnp.stack([st_f, st_b], axis=1)
            st = st.reshape(Bc, 2, G, N, 4, P).transpose(0, 1, 2, 4, 3, 5).reshape(Bc, 2, G * 4, N, P)
            new_ssm.append(st)
            w_out = ssm_out_w[j]
        elif kind == 1:
            w_small = _pad_cols(jnp.concatenate([gla_gate_w1[j, 0], gla_gate_w1[j, 1]], axis=1))
            proj, lrp = _in_proj(x, norm_mix_w[i], mod, 0, gla_in_w[j].astype(bf16), w_small, geo)
            w2 = gla_gate_w2[j]
            gb = gla_gate_b[j].reshape(2, 1, -1)
            nw = gla_norm_w[j].reshape(1, -1)
            yf, st_f = _gla_scan(proj, lrp, w2, gb, None, state_gla[:, j, 0], None, geo, False)
            u, st_b = _gla_scan(proj, lrp, w2, gb, nw, state_gla[:, j, 1], yf, geo, True)
            new_gla.append(jnp.stack([st_f, st_b], axis=1))
            w_out = gla_out_w[j]
        else:
            w_in = gdn_in_w[j]
            perm = _gdn_perm(GDN_HP)
            w_main = jnp.concatenate([w_in[:, :4096][:, perm], w_in[:, 4096:6144]], axis=1).astype(bf16)
            proj, abp = _in_proj(x, norm_mix_w[i], mod, 0, w_main, _pad_cols(w_in[:, 6144:]), geo)
            dtb, alog = _pad_row(gdn_dt_bias[j]), _pad_row(gdn_a_log[j])
            nw = gdn_norm_w[j].reshape(1, -1)
            cw = gdn_conv_w[j][:, perm]
            yf, st_f = _gdn_scan(proj, abp, cw, dtb, alog, None, state_gdn[:, j, 0], None, geo, False)
            u, st_b = _gdn_scan(proj, abp, cw, dtb, alog, nw, state_gdn[:, j, 1], yf, geo, True)
            new_gdn.append(jnp.stack([st_f, st_b], axis=1))
            w_out = gdn_out_w[j]
        x = _out_proj(u, w_out.astype(bf16), x, mod, 2, geo)

        proj, _ = _in_proj(x, norm_ffn_w[i], mod, 3, ffn_in_w[i].astype(bf16), None, geo)
        u = _ffn_conv(proj, ffn_conv_w[i].reshape(9, -1), ffn_conv_b[i], geo)
        x = _out_proj(u, ffn_out_w[i].astype(bf16), x, mod, 5, geo,
                      final_w=final_norm_w if i == depth - 1 else None)

    y_prompt = x[:geo.Tc].reshape(Bc, SEQ, D)
    y_sample = x[geo.Tc:].reshape(Bs, LS, D)
    return (y_prompt, y_sample, jnp.stack(new_ssm, axis=1), jnp.stack(new_gla, axis=1),
            jnp.stack(new_gdn, axis=1))
```

```python
import functools
from typing import NamedTuple

import numpy as np
import jax
import jax.numpy as jnp
from jax import lax
from jax.experimental import pallas as pl
from jax.experimental.pallas import tpu as pltpu

f32, bf16 = jnp.float32, jnp.bfloat16

BLK = 256
CHUNK = 64
SSD_CHUNK = 128
HALO = 16
GRID_W = 64
NORM_EPS = 1e-6
SMALL = 128
GLA_RANK = 16
GLA_NORMALIZER = 16.0
SSM_GROUPS = 8
SSM_HEADDIM = 64
SSM_STATE = 128
SSD_GP = 2
GDN_HP = 4
VMEM_LIMIT = 48 * 1024 * 1024

NN = (((1,), (0,)), ((), ()))
NT = (((1,), (1,)), ((), ()))
TN = (((0,), (0,)), ((), ()))


class Geo(NamedTuple):
    D: int
    Tc: int
    Ts: int
    SEQ: int
    LS: int
    Bc: int
    Bs: int

    @property
    def T(self):
        return self.Tc + self.Ts

    @property
    def nb(self):
        return self.T // BLK

    @property
    def ncb(self):
        return self.Tc // BLK

    @property
    def bpc(self):
        return self.SEQ // BLK

    @property
    def bps(self):
        return self.LS // BLK


def _mm(a, b, dims=NN):
    return lax.dot_general(a, b, dims, preferred_element_type=f32)


def _split3(a):
    hi = a.astype(bf16)
    r1 = a - hi.astype(f32)
    mid = r1.astype(bf16)
    lo = (r1 - mid.astype(f32)).astype(bf16)
    return hi, mid, lo


def _mm_x(a, b, dims=NN):
    if isinstance(a, tuple):
        ps = [_mm(p, b, dims) for p in a]
    elif a.dtype == bf16:
        ps = [_mm(a, p, dims) for p in _split3(b)]
    else:
        ps = [_mm(p, b, dims) for p in _split3(a)]
    return ps[0] + ps[1] + ps[2]


def _mm_hi(a, b):
    ah = a.astype(bf16)
    al = (a - ah.astype(f32)).astype(bf16)
    bh = b.astype(bf16)
    bl = (b - bh.astype(f32)).astype(bf16)
    return _mm(ah, bh) + _mm(al, bh) + _mm(ah, bl)


def _silu(x):
    return x * jax.nn.sigmoid(x)


def _tri_mask(n, upper, chunk=None, strict=False):
    r = lax.broadcasted_iota(jnp.int32, (n, n), 0)
    c = lax.broadcasted_iota(jnp.int32, (n, n), 1)
    if upper:
        m = (c > r) if strict else (c >= r)
    else:
        m = (c < r) if strict else (c <= r)
    if chunk is not None and chunk < n:
        sh = int(np.log2(chunk))
        same = lax.shift_right_logical(r, sh) == lax.shift_right_logical(c, sh)
        m = jnp.logical_and(m, same)
    return m


def _as_bf16(mask):
    return jnp.where(mask, 1.0, 0.0).astype(bf16)


def _ada_kernel(c_ref, w_ref, b_ref, o_ref):
    o_ref[0] = _mm_hi(_silu(c_ref[...]), w_ref[0]) + b_ref[0]


def _ada_table(cond8, ada_w, ada_b):
    depth, D, N6 = ada_w.shape
    tn = N6 // 4
    return pl.pallas_call(
        _ada_kernel,
        out_shape=jax.ShapeDtypeStruct((depth, 8, N6), f32),
        grid=(depth, N6 // tn),
        in_specs=[pl.BlockSpec((8, D), lambda l, j: (0, 0)),
                  pl.BlockSpec((1, D, tn), lambda l, j: (l, 0, j)),
                  pl.BlockSpec((1, 1, tn), lambda l, j: (l, 0, j))],
        out_specs=pl.BlockSpec((1, 8, tn), lambda l, j: (l, 0, j)),
        compiler_params=pltpu.CompilerParams(
            dimension_semantics=("parallel", "parallel"), vmem_limit_bytes=VMEM_LIMIT),
        name="ada_table",
    )(cond8, ada_w, ada_b.reshape(depth, 1, N6))


def _in_proj_kernel(*refs, has_small, tn):
    if has_small:
        x_ref, nw_ref, sh_ref, sc_ref, w_ref, ws_ref, o_ref, os_ref = refs
    else:
        x_ref, nw_ref, sh_ref, sc_ref, w_ref, o_ref = refs
    x = x_ref[...]
    h = x * lax.rsqrt(jnp.mean(x * x, axis=-1, keepdims=True) + NORM_EPS) * nw_ref[...]
    h = h * (1.0 + sc_ref[0]) + sh_ref[0]
    hb = h.astype(bf16)
    if has_small:
        ws = ws_ref[...]
        wh = ws.astype(bf16)
        wl = (ws - wh.astype(f32)).astype(bf16)
        hl = (h - hb.astype(f32)).astype(bf16)
        os_ref[...] = _mm(hb, wh) + _mm(hl, wh) + _mm(hb, wl)
    for n in range(w_ref.shape[1] // tn):
        o_ref[:, n * tn:(n + 1) * tn] = _mm(hb, w_ref[:, n * tn:(n + 1) * tn]).astype(o_ref.dtype)


def _row_tile(geo, pref):
    for tm in (pref, 512, 256):
        if tm <= pref and geo.Tc % tm == 0 and geo.LS % tm == 0:
            return tm
    raise ValueError("token counts must be multiples of 256")


def _cond_of_tile(i, geo, tm):
    nct = geo.Tc // tm
    return jnp.where(i < nct, 0, 1 + jnp.maximum(i - nct, 0) // (geo.LS // tm))


def _in_proj(x, norm_w, mod, shift_idx, w_main, w_small, geo, tn=512):
    T, D = x.shape
    N = w_main.shape[1]
    tm = _row_tile(geo, 512)
    assert N % tn == 0
    has_small = w_small is not None
    cond = functools.partial(_cond_of_tile, geo=geo, tm=tm)
    in_specs = [pl.BlockSpec((tm, D), lambda i: (i, 0)),
                pl.BlockSpec((1, D), lambda i: (0, 0)),
                pl.BlockSpec((1, 1, D), lambda i: (cond(i), 0, shift_idx)),
                pl.BlockSpec((1, 1, D), lambda i: (cond(i), 0, shift_idx + 1)),
                pl.BlockSpec((D, N), lambda i: (0, 0), pipeline_mode=pl.Buffered(1))]
    args = [x, norm_w.reshape(1, D), mod, mod, w_main]
    out_shape = [jax.ShapeDtypeStruct((T, N), bf16)]
    out_specs = [pl.BlockSpec((tm, N), lambda i: (i, 0))]
    if has_small:
        in_specs.append(pl.BlockSpec((D, SMALL), lambda i: (0, 0)))
        args.append(w_small)
        out_shape.append(jax.ShapeDtypeStruct((T, SMALL), f32))
        out_specs.append(pl.BlockSpec((tm, SMALL), lambda i: (i, 0)))
    res = pl.pallas_call(
        functools.partial(_in_proj_kernel, has_small=has_small, tn=tn),
        out_shape=out_shape,
        grid=(T // tm,),
        in_specs=in_specs,
        out_specs=out_specs,
        compiler_params=pltpu.CompilerParams(
            dimension_semantics=("parallel",), vmem_limit_bytes=VMEM_LIMIT),
        name="in_proj",
    )(*args)
    return res if has_small else (res[0], None)


def _out_proj_kernel(*refs, final):
    if final:
        u_ref, w_ref, x_ref, g_ref, fw_ref, o_ref = refs
    else:
        u_ref, w_ref, x_ref, g_ref, o_ref = refs
    r = x_ref[...] + g_ref[0] * _mm(u_ref[...], w_ref[...])
    if final:
        r = r * lax.rsqrt(jnp.mean(r * r, axis=-1, keepdims=True) + NORM_EPS) * fw_ref[...]
    o_ref[...] = r


def _out_proj(u, w, x, mod, gate_idx, geo, final_w=None):
    T, K = u.shape
    D = w.shape[1]
    tm = _row_tile(geo, 512)
    cond = functools.partial(_cond_of_tile, geo=geo, tm=tm)
    final = final_w is not None
    in_specs = [pl.BlockSpec((tm, K), lambda i: (i, 0)),
                pl.BlockSpec((K, D), lambda i: (0, 0)),
                pl.BlockSpec((tm, D), lambda i: (i, 0)),
                pl.BlockSpec((1, 1, D), lambda i: (cond(i), 0, gate_idx))]
    args = [u, w, x, mod]
    if final:
        in_specs.append(pl.BlockSpec((1, D), lambda i: (0, 0)))
        args.append(final_w.reshape(1, D))
    return pl.pallas_call(
        functools.partial(_out_proj_kernel, final=final),
        out_shape=jax.ShapeDtypeStruct((T, D), f32),
        grid=(T // tm,),
        in_specs=in_specs,
        out_specs=pl.BlockSpec((tm, D), lambda i: (i, 0)),
        compiler_params=pltpu.CompilerParams(
            dimension_semantics=("parallel",), vmem_limit_bytes=VMEM_LIMIT),
        name="out_proj",
    )(*args)


def _ffn_conv_kernel(a_ref, v_ref, w_ref, b_ref, o_ref, pad_ref, *, n_ctx_img, IB, SEQ):
    PADR = 128
    tf = a_ref.shape[1]
    pad_ref[0:PADR, :] = jnp.zeros((PADR, tf), f32)
    pad_ref[PADR + IB:PADR + IB + PADR, :] = jnp.zeros((PADR, tf), f32)
    pad_ref[PADR:PADR + IB, :] = a_ref[...].astype(f32)
    w = w_ref[...]
    b = b_ref[...]
    row = lax.broadcasted_iota(jnp.int32, (BLK, tf), 0)

    def taps(s, pos, width, kr):
        left = jnp.where(pos > 0, pad_ref[s - 1:s - 1 + BLK, :], 0.0)
        right = jnp.where(pos < width - 1, pad_ref[s + 1:s + 1 + BLK, :], 0.0)
        return (left * w[3 * kr:3 * kr + 1] + pad_ref[s:s + BLK, :] * w[3 * kr + 1:3 * kr + 2]
                + right * w[3 * kr + 2:3 * kr + 3])

    def finish(c, acc):
        v = v_ref[c * BLK:(c + 1) * BLK, :].astype(f32)
        o_ref[c * BLK:(c + 1) * BLK, :] = (_silu(acc + b) * v).astype(bf16)

    @pl.when(pl.program_id(0) < n_ctx_img)
    def _():
        for c in range(IB // BLK):
            pos = jnp.bitwise_and(row + (c * BLK) % SEQ, SEQ - 1)
            finish(c, taps(PADR + c * BLK, pos, SEQ, 1))

    @pl.when(pl.program_id(0) >= n_ctx_img)
    def _():
        col = jnp.bitwise_and(row, GRID_W - 1)
        for c in range(IB // BLK):
            s = PADR + c * BLK
            acc = taps(s - GRID_W, col, GRID_W, 0) + taps(s, col, GRID_W, 1) + taps(s + GRID_W, col, GRID_W, 2)
            finish(c, acc)


def _ffn_conv(proj, conv_w9, conv_b, geo, tf=256):
    T, F2 = proj.shape
    F = F2 // 2
    IB = geo.LS
    assert F % tf == 0 and geo.Tc % IB == 0 and IB % BLK == 0
    assert geo.SEQ & (geo.SEQ - 1) == 0 and BLK % GRID_W == 0
    nf = F // tf
    return pl.pallas_call(
        functools.partial(_ffn_conv_kernel, n_ctx_img=geo.Tc // IB, IB=IB, SEQ=geo.SEQ),
        out_shape=jax.ShapeDtypeStruct((T, F), bf16),
        grid=(T // IB, nf),
        in_specs=[pl.BlockSpec((IB, tf), lambda i, j: (i, j)),
                  pl.BlockSpec((IB, tf), lambda i, j: (i, nf + j)),
                  pl.BlockSpec((9, tf), lambda i, j: (0, j)),
                  pl.BlockSpec((1, tf), lambda i, j: (0, j))],
        out_specs=pl.BlockSpec((IB, tf), lambda i, j: (i, j)),
        scratch_shapes=[pltpu.VMEM((IB + 256, tf), f32)],
        compiler_params=pltpu.CompilerParams(
            dimension_semantics=("parallel", "parallel"), vmem_limit_bytes=VMEM_LIMIT),
        name="ffn_conv",
    )(proj, proj, conv_w9, conv_b.reshape(1, F))


def _blk_idx(i, geo, rev):
    return (geo.nb - 1 - i) if rev else i


def _tok_spec(cw, colfn, geo, rev):
    return pl.BlockSpec((BLK, cw), lambda u, i: (_blk_idx(i, geo, rev), colfn(u)))


def _halo_specs(cw, colfn, geo, rev):
    per = BLK // HALO
    last = geo.T // HALO - 1
    prev = pl.BlockSpec((HALO, cw), lambda u, i: (jnp.maximum(_blk_idx(i, geo, rev) * per - 1, 0), colfn(u)))
    nxt = pl.BlockSpec((HALO, cw), lambda u, i: (jnp.minimum((_blk_idx(i, geo, rev) + 1) * per, last), colfn(u)))
    return prev, nxt


def _h0_spec(geo, rev, hp):
    def idx(u, i):
        ip = _blk_idx(i, geo, rev)
        return (jnp.maximum(ip - geo.ncb, 0) // geo.bps, u, 0, 0)
    return pl.BlockSpec((1, hp, 128, 256), idx)


def _st_spec(geo, rev, layer, blk):
    d = 1 if rev else 0

    def idx(u, i):
        ip = _blk_idx(i, geo, rev)
        return (jnp.minimum(ip // geo.bpc, geo.Bc - 1), layer, d, u, 0, 0)
    return pl.BlockSpec((1, 1, 1) + blk, idx)


def _param_spec(rows, cw, colfn):
    return pl.BlockSpec((rows, cw), lambda u, i: (0, colfn(u)))


def _flags(geo, rev):
    ip = _blk_idx(pl.program_id(1), geo, rev)
    is_ctx = ip < geo.ncb
    pos = jnp.where(is_ctx, lax.rem(ip, geo.bpc), lax.rem(jnp.maximum(ip - geo.ncb, 0), geo.bps))
    n = jnp.where(is_ctx, geo.bpc, geo.bps)
    first = pos == 0
    last = pos == n - 1
    return is_ctx, first, last, (last if rev else first)


def _conv3_silu(cur_ref, prev_ref, next_ref, w_ref, b_ref, first, last):
    cur = cur_ref[...].astype(f32)
    n = cur.shape[0]
    row = lax.broadcasted_iota(jnp.int32, cur.shape, 0)
    pr = jnp.where(first, 0.0, prev_ref[HALO - 1:HALO, :].astype(f32))
    nx = jnp.where(last, 0.0, next_ref[0:1, :].astype(f32))
    xm = jnp.where(row == 0, pr, pltpu.roll(cur, 1, axis=0))
    xp = jnp.where(row == n - 1, nx, pltpu.roll(cur, n - 1, axis=0))
    w = w_ref[...]
    y = xm * w[0:1] + cur * w[1:2] + xp * w[2:3]
    if b_ref is not None:
        y = y + b_ref[...]
    return _silu(y)


def _init_state(S, h0_ref, is_ctx, start):
    @pl.when(start)
    def _():
        S[...] = jnp.where(is_ctx, 0.0, h0_ref[0])


def _norm_gate(o, gate, nw, gate_first):
    g = _silu(gate)
    if gate_first:
        o = o * g
    o = o * lax.rsqrt(jnp.mean(o * o, axis=-1, keepdims=True) + NORM_EPS) * nw
    if not gate_first:
        o = o * g
    return o


def _scan_call(kernel, name, grid_u, in_specs, args, outs, st, st_blk, layer, geo, rev, hp):
    out_shape = [jax.ShapeDtypeStruct((geo.T, c), dt) for c, dt, _ in outs]
    out_specs = [_tok_spec(bc, lambda u: u, geo, rev) for _, _, bc in outs]
    return pl.pallas_call(
        kernel,
        out_shape=out_shape + [jax.ShapeDtypeStruct(st.shape, st.dtype)],
        grid=(grid_u, geo.nb),
        in_specs=in_specs + [pl.BlockSpec(memory_space=pl.ANY)],
        out_specs=out_specs + [_st_spec(geo, rev, layer, st_blk)],
        scratch_shapes=[pltpu.VMEM((hp, 128, 256), f32)],
        input_output_aliases={len(args): len(outs)},
        compiler_params=pltpu.CompilerParams(
            dimension_semantics=("parallel", "arbitrary"), vmem_limit_bytes=VMEM_LIMIT),
        name=name,
    )(*args, st)


def _ssd_kernel(*refs, rev, geo, gp):
    if rev:
        (pc_ref, dt_ref, dtb_ref, alog_ref, xsel_ref, xh_ref, h0_ref,
         dsk_ref, z_ref, nw_ref, yf_ref, _, y_ref, st_ref, S) = refs
    else:
        (xbc_c, xbc_p, xbc_n, dt_ref, cw_ref, cb_ref, dtb_ref, alog_ref, xsel_ref, xh_ref, h0_ref,
         _, y_ref, pc_ref, st_ref, S) = refs
    is_ctx, first, last, start = _flags(geo, rev)
    _init_state(S, h0_ref, is_ctx, start)
    Q = SSD_CHUNK
    nch = BLK // Q
    groups = range(gp)

    if rev:
        xbc = pc_ref[...].astype(f32)
    else:
        xbc = _conv3_silu(xbc_c, xbc_p, xbc_n, cw_ref, cb_ref, first, last)
        pc_ref[...] = xbc.astype(bf16)
    dt = jax.nn.softplus(dt_ref[...] + dtb_ref[...])
    la = dt * (-jnp.exp(alog_ref[...]))
    tri = _tri_mask(BLK, rev, chunk=Q)
    cs3 = _split3(_mm_x(_as_bf16(tri), la))
    dtb16 = dt.astype(bf16)
    tri_q = _tri_mask(Q, rev)

    xs, xvs, bms, cms, cses, csts = [], [], [], [], [], []
    for g in groups:
        x = xbc[:, g * 512:g * 512 + 256]
        xs.append(x)
        bms.append(xbc[:, g * 512 + 256:g * 512 + 384].astype(bf16))
        cms.append(xbc[:, g * 512 + 384:g * 512 + 512].astype(bf16))
        xsel = xsel_ref[g]
        xvs.append(x * _mm(dtb16, xsel))
        cses.append(_mm_x(cs3, xsel))
        csts.append(_mm_x(cs3, xh_ref[g]).T)
    Svs = [S[g] for g in groups]
    youts = [[None] * nch for _ in groups]
    for c in (range(nch - 1, -1, -1) if rev else range(nch)):
        lo = c * Q
        li = lo if rev else lo + Q - 1
        for g in groups:
            cs_e = cses[g][lo:lo + Q]
            xv = xvs[g][lo:lo + Q]
            xvb = xv.astype(bf16)
            cmb = cms[g][lo:lo + Q]
            bmb = bms[g][lo:lo + Q]
            gm = _mm(cmb, bmb, NT)
            ys = []
            for r in range(4):
                seg = cs_e[:, r * 64:r * 64 + 1] - csts[g][r:r + 1, lo:lo + Q]
                m = jnp.where(tri_q, gm * jnp.exp(seg), 0.0)
                ys.append(_mm(m.astype(bf16), xvb[:, r * 64:(r + 1) * 64]))
            Sv = Svs[g]
            y = jnp.concatenate(ys, axis=1) + _mm(cmb, Sv.astype(bf16)) * jnp.exp(cs_e)
            cl = cses[g][li:li + 1, :]
            xw = xv * jnp.exp(cl - cs_e)
            Svs[g] = Sv * jnp.exp(cl) + _mm(bmb, xw.astype(bf16), TN)
            youts[g][c] = y

    for g in groups:
        S[g] = Svs[g]
        y = jnp.concatenate(youts[g], axis=0)
        if rev:
            cols = slice(g * 256, (g + 1) * 256)
            o = yf_ref[:, cols] + y + xs[g] * dsk_ref[:, cols]
            y = _norm_gate(o, z_ref[:, cols].astype(f32), nw_ref[:, cols], gate_first=True)
        y_ref[:, g * 256:(g + 1) * 256] = y.astype(y_ref.dtype)

    @pl.when(is_ctx)
    def _():
        for g in groups:
            for r in range(4):
                st_ref[0, 0, 0, g * 4 + r] = Svs[g][:, r * 64:(r + 1) * 64]


def _ssd_scan(proj, pc, dtp, conv_w, conv_b, dtb, alog, dskip, norm_w, h0, yf, st, layer, geo, rev):
    G, gp = SSM_GROUPS, SSD_GP
    nu = G // gp
    d = 1 if rev else 0
    xsel = np.zeros((G, 128, 256), np.float32)
    xh = np.zeros((G, 128, 128), np.float32)
    for g in range(G):
        for r in range(4):
            xsel[g, d * 32 + 4 * g + r, r * 64:(r + 1) * 64] = 1.0
            xh[g, d * 32 + 4 * g + r, r] = 1.0
    cw, zw = 512 * gp, 256 * gp
    head_specs = [_param_spec(1, SMALL, lambda u: 0), _param_spec(1, SMALL, lambda u: 0),
                  pl.BlockSpec((gp, 128, 256), lambda u, i: (u, 0, 0)),
                  pl.BlockSpec((gp, 128, 128), lambda u, i: (u, 0, 0)),
                  _h0_spec(geo, rev, gp)]
    head_args = [dtb, alog, jnp.asarray(xsel, bf16), jnp.asarray(xh, bf16), h0]
    if rev:
        in_specs = [_tok_spec(cw, lambda u: u, geo, rev), _tok_spec(SMALL, lambda u: 0, geo, rev), *head_specs,
                    _param_spec(1, zw, lambda u: u), _tok_spec(zw, lambda u: 2 * nu + u, geo, rev),
                    _param_spec(1, zw, lambda u: u), _tok_spec(zw, lambda u: u, geo, rev)]
        args = [pc, dtp, *head_args, dskip, proj, norm_w, yf]
        outs = [(2048, bf16, zw)]
    else:
        in_specs = [_tok_spec(cw, lambda u: u, geo, rev), *_halo_specs(cw, lambda u: u, geo, rev),
                    _tok_spec(SMALL, lambda u: 0, geo, rev),
                    _param_spec(3, cw, lambda u: u), _param_spec(1, cw, lambda u: u), *head_specs]
        args = [proj, proj, proj, dtp, conv_w, conv_b, *head_args]
        outs = [(2048, f32, zw), (4096, bf16, cw)]
    return _scan_call(functools.partial(_ssd_kernel, rev=rev, geo=geo, gp=gp), "ssd_bwd" if rev else "ssd_fwd",
                      nu, in_specs, args, outs, st, (4 * gp, SSM_STATE, SSM_HEADDIM), layer, geo, rev, gp)


def _gla_kernel(*refs, rev, geo, nh):
    qkv_ref, lr_ref, w2_ref, gb_ref, h0_ref = refs[:5]
    if rev:
        r_ref, nw_ref, yf_ref, _, y_ref, st_ref, S = refs[5:]
    else:
        _, y_ref, st_ref, S = refs[5:]
    is_ctx, first, last, start = _flags(geo, rev)
    _init_state(S, h0_ref, is_ctx, start)
    d = 1 if rev else 0
    dk, dv = 128, 256
    heads = range(nh)
    K = nh * dk

    q_all = qkv_ref[:, 0:K].astype(f32) * dk ** -0.5
    k_all = qkv_ref[:, K:2 * K].astype(f32)
    vb_all = qkv_ref[:, 2 * K:2 * K + nh * dv]
    lr = lr_ref[...][:, d * GLA_RANK:(d + 1) * GLA_RANK]
    logit = _mm_hi(lr, w2_ref[0]) + gb_ref[0]
    lg = jax.nn.log_sigmoid(logit) * (1.0 / GLA_NORMALIZER)
    tri = _tri_mask(BLK, rev, chunk=CHUNK)
    cs = _mm_x(_as_bf16(tri), lg)
    cs_t = _mm_x(lg, _as_bf16(_tri_mask(BLK, not rev, chunk=CHUNK)), TN)
    qgb = (q_all * jnp.exp(cs)).astype(bf16)
    kgb = (k_all * jnp.exp(-cs)).astype(bf16)

    ys = []
    for h in heads:
        sc = jnp.where(tri, _mm(qgb[:, h * dk:(h + 1) * dk], kgb[:, h * dk:(h + 1) * dk], NT), 0.0)
        ys.append(_mm(sc.astype(bf16), vb_all[:, h * dv:(h + 1) * dv]))
    Svs = [S[h] for h in heads]
    nch = BLK // CHUNK
    yin = [[None] * nch for _ in heads]
    for c in (range(nch - 1, -1, -1) if rev else range(nch)):
        lo = c * CHUNK
        li = lo if rev else lo + CHUNK - 1
        kw_all = (k_all[lo:lo + CHUNK] * jnp.exp(cs[li:li + 1, :] - cs[lo:lo + CHUNK])).astype(bf16)
        for h in heads:
            yin[h][c] = _mm(qgb[lo:lo + CHUNK, h * dk:(h + 1) * dk], Svs[h].astype(bf16))
        for h in heads:
            Svs[h] = (Svs[h] * jnp.exp(cs_t[h * dk:(h + 1) * dk, li:li + 1])
                      + _mm(kw_all[:, h * dk:(h + 1) * dk], vb_all[lo:lo + CHUNK, h * dv:(h + 1) * dv], TN))

    for h in heads:
        S[h] = Svs[h]
        y = ys[h] + jnp.concatenate(yin[h], axis=0)
        cols = slice(h * dv, (h + 1) * dv)
        if rev:
            y = _norm_gate(yf_ref[:, cols] + y, r_ref[:, cols].astype(f32), nw_ref[...], gate_first=False)
        y_ref[:, cols] = y.astype(y_ref.dtype)

    @pl.when(is_ctx)
    def _():
        for h in heads:
            st_ref[0, 0, 0, h] = Svs[h]


def _gla_scan(proj, lrp, w2, gb, norm_w, h0, yf, st, layer, geo, rev):
    H = 4
    d = 1 if rev else 0
    in_specs = [_tok_spec(2048, lambda u: 0, geo, rev), _tok_spec(SMALL, lambda u: 0, geo, rev),
                pl.BlockSpec((1, GLA_RANK, 512), lambda u, i: (d, 0, 0)),
                pl.BlockSpec((1, 1, 512), lambda u, i: (d, 0, 0)),
                _h0_spec(geo, rev, H)]
    args = [proj, lrp, w2, gb, h0]
    if rev:
        in_specs += [_tok_spec(1024, lambda u: 2, geo, rev), _param_spec(1, 256, lambda u: 0),
                     _tok_spec(1024, lambda u: 0, geo, rev)]
        args += [proj, norm_w, yf]
    return _scan_call(functools.partial(_gla_kernel, rev=rev, geo=geo, nh=H), "gla_bwd" if rev else "gla_fwd",
                      1, in_specs, args, [(1024, bf16 if rev else f32, 1024)], st, (H, 128, 256), layer, geo, rev, H)


def _unit_tri_inverse(ms, lev):
    eye = jnp.where(lev == -1, 1.0, 0.0)
    dinv = [eye - jnp.where(lev == 0, m, 0.0) for m in ms]
    for b in range(1, 6):
        ts = [_mm(jnp.where(lev == b, m, 0.0).astype(bf16), d.astype(bf16)).astype(bf16)
              for m, d in zip(ms, dinv)]
        dinv = [d - _mm(d.astype(bf16), t) for d, t in zip(dinv, ts)]
    return dinv


def _gdn_kernel(*refs, rev, geo, hp):
    if rev:
        (pc_ref, ab_ref, dtb_ref, alog_ref, xab_ref, lev_ref, h0_ref,
         z_ref, nw_ref, yf_ref, _, y_ref, st_ref, S) = refs
    else:
        (qkv_c, qkv_p, qkv_n, cw_ref, ab_ref, dtb_ref, alog_ref, xab_ref, lev_ref, h0_ref,
         _, y_ref, pc_ref, st_ref, S) = refs
    is_ctx, first, last, start = _flags(geo, rev)
    _init_state(S, h0_ref, is_ctx, start)
    dk, dv = 128, 256
    heads = range(hp)

    if rev:
        qkv = pc_ref[...].astype(f32)
    else:
        qkv = _conv3_silu(qkv_c, qkv_p, qkv_n, cw_ref, None, first, last)

    ab = ab_ref[...]
    gall = -jnp.exp(alog_ref[...]) * jax.nn.softplus(ab + dtb_ref[...])
    lane = lax.broadcasted_iota(jnp.int32, ab.shape, 1)
    sel = _mm_x(jnp.where(lane < 16, gall, jax.nn.sigmoid(ab)), xab_ref[0])
    lev = lev_ref[...]
    r_i = lax.broadcasted_iota(jnp.int32, (BLK, BLK), 0)
    c_i = lax.broadcasted_iota(jnp.int32, (BLK, BLK), 1)
    in_chunk = lev < 6
    incl = jnp.logical_and(in_chunk, (c_i >= r_i) if rev else (c_i <= r_i))
    incl_t = jnp.logical_and(in_chunk, (c_i <= r_i) if rev else (c_i >= r_i))
    strict = jnp.logical_and(incl, lev >= 0)
    cs_all = _mm_x(_as_bf16(incl), sel)
    cs_rows = _mm_x(sel, _as_bf16(incl_t), TN)

    qs, ks, kbfs, kbs, vbs, css, lmasks, ms = [], [], [], [], [], [], [], []
    for hh in heads:
        q = qkv[:, hh * dk:(hh + 1) * dk]
        k = qkv[:, (hp + hh) * dk:(hp + hh + 1) * dk]
        v = qkv[:, 2 * hp * dk + hh * dv:2 * hp * dk + (hh + 1) * dv]
        if not rev:
            q = q * lax.rsqrt(jnp.sum(q * q, axis=-1, keepdims=True) + 1e-6) * dk ** -0.5
            k = k * lax.rsqrt(jnp.sum(k * k, axis=-1, keepdims=True) + 1e-6)
            pc_ref[:, hh * dk:(hh + 1) * dk] = q.astype(bf16)
            pc_ref[:, (hp + hh) * dk:(hp + hh + 1) * dk] = k.astype(bf16)
            pc_ref[:, 2 * hp * dk + hh * dv:2 * hp * dk + (hh + 1) * dv] = v.astype(bf16)
        beta = sel[:, 2 * hh + 1:2 * hh + 2]
        cs = cs_all[:, 2 * hh:2 * hh + 1]
        lmask = jnp.where(incl, jnp.exp(cs - cs_rows[2 * hh:2 * hh + 1, :]), 0.0)
        kb = k * beta
        kbf = k.astype(bf16)
        qs.append(q)
        ks.append(k)
        kbfs.append(kbf)
        kbs.append(kb)
        vbs.append((v * beta).astype(bf16))
        css.append(cs)
        lmasks.append(lmask)
        ms.append(jnp.where(strict, _mm(kb.astype(bf16), kbf, NT) * lmask, 0.0))
    tbs = [t.astype(bf16) for t in _unit_tri_inverse(ms, lev)]
    ecs = [jnp.exp(cs) for cs in css]
    us = [_mm(tbs[h], vbs[h]) for h in heads]
    wks = [_mm(tbs[h], (kbs[h] * ecs[h]).astype(bf16)).astype(bf16) for h in heads]
    a_ins = [(_mm(qs[h].astype(bf16), kbfs[h], NT) * lmasks[h]).astype(bf16) for h in heads]
    qds = [(qs[h] * ecs[h]).astype(bf16) for h in heads]

    Svs = [S[h] for h in heads]
    nch = BLK // CHUNK
    outs = [[None] * nch for _ in heads]
    for c in (range(nch - 1, -1, -1) if rev else range(nch)):
        lo = c * CHUNK
        li = lo if rev else lo + CHUNK - 1
        Sbs = [Sv.astype(bf16) for Sv in Svs]
        v_news = [(us[h][lo:lo + CHUNK] - _mm(wks[h][lo:lo + CHUNK], Sbs[h])).astype(bf16) for h in heads]
        for h in heads:
            outs[h][c] = (_mm(qds[h][lo:lo + CHUNK], Sbs[h])
                          + _mm(a_ins[h][lo:lo + CHUNK, lo:lo + CHUNK], v_news[h]))
        for h in heads:
            cl = css[h][li:li + 1, :]
            kd = ks[h][lo:lo + CHUNK] * jnp.exp(cl - css[h][lo:lo + CHUNK])
            Svs[h] = Svs[h] * jnp.exp(cl) + _mm(kd.astype(bf16), v_news[h], TN)

    for hh in heads:
        S[hh] = Svs[hh]
        o = jnp.concatenate(outs[hh], axis=0)
        cols = slice(hh * dv, (hh + 1) * dv)
        if rev:
            o = _norm_gate(yf_ref[:, cols] + o, z_ref[:, cols].astype(f32), nw_ref[...], gate_first=False)
        y_ref[:, cols] = o.astype(y_ref.dtype)

    @pl.when(is_ctx)
    def _():
        for hh in heads:
            st_ref[0, 0, 0, hh] = Svs[hh]


def _gdn_scan(proj, pc, abp, conv_w, dtb, alog, norm_w, h0, yf, st, layer, geo, rev):
    H, hp = 8, GDN_HP
    nu = H // hp
    d = 1 if rev else 0
    xab = np.zeros((nu, 128, 128), np.float32)
    for h in range(H):
        xab[h // hp, d * H + h, 2 * (h % hp)] = 1.0
        xab[h // hp, 2 * H + d * H + h, 2 * (h % hp) + 1] = 1.0
    idx = np.arange(BLK)
    x = idx[:, None] ^ idx[None, :]
    lev = np.where(x == 0, -1, np.floor(np.log2(np.maximum(x, 1)))).astype(np.int32)
    cw, vw = 512 * hp, 256 * hp
    head_specs = [_tok_spec(SMALL, lambda u: 0, geo, rev),
                  _param_spec(1, SMALL, lambda u: 0), _param_spec(1, SMALL, lambda u: 0),
                  pl.BlockSpec((1, 128, 128), lambda u, i: (u, 0, 0)),
                  pl.BlockSpec((BLK, BLK), lambda u, i: (0, 0)),
                  _h0_spec(geo, rev, hp)]
    head_args = [abp, dtb, alog, jnp.asarray(xab, bf16), jnp.asarray(lev), h0]
    if rev:
        in_specs = [_tok_spec(cw, lambda u: u, geo, rev), *head_specs,
                    _tok_spec(vw, lambda u: 2 * nu + u, geo, rev), _param_spec(1, 256, lambda u: 0),
                    _tok_spec(vw, lambda u: u, geo, rev)]
        args = [pc, *head_args, proj, norm_w, yf]
        outs = [(2048, bf16, vw)]
    else:
        in_specs = [_tok_spec(cw, lambda u: u, geo, rev), *_halo_specs(cw, lambda u: u, geo, rev),
                    _param_spec(3, cw, lambda u: u), *head_specs]
        args = [proj, proj, proj, conv_w, *head_args]
        outs = [(2048, f32, vw), (4096, bf16, cw)]
    return _scan_call(functools.partial(_gdn_kernel, rev=rev, geo=geo, hp=hp), "gdn_bwd" if rev else "gdn_fwd",
                      nu, in_specs, args, outs, st, (hp, 128, 256), layer, geo, rev, hp)


def _pad_cols(w, n=SMALL):
    return jnp.pad(w.astype(f32), ((0, 0), (0, n - w.shape[1])))


def _pad_row(v, n=SMALL):
    v = v.reshape(1, -1).astype(f32)
    return jnp.pad(v, ((0, 0), (0, n - v.shape[1])))


def _ssd_perm():
    perm = []
    for g in range(SSM_GROUPS):
        perm += list(range(g * 256, (g + 1) * 256))
        perm += list(range(2048 + g * 128, 2048 + (g + 1) * 128))
        perm += list(range(3072 + g * 128, 3072 + (g + 1) * 128))
    return np.asarray(perm, np.int32)


def _gdn_perm(hp):
    perm = []
    for u in range(8 // hp):
        perm += list(range(u * hp * 128, (u + 1) * hp * 128))
        perm += list(range(1024 + u * hp * 128, 1024 + (u + 1) * hp * 128))
        perm += list(range(2048 + u * hp * 256, 2048 + (u + 1) * hp * 256))
    return np.asarray(perm, np.int32)


def kernel(x_prompt, x_sample, state_ssm, state_gla, state_gdn, c, c_ctx, norm_mix_w, norm_ffn_w, ada_w, ada_b, ffn_in_w, ffn_conv_w, ffn_conv_b, ffn_out_w, ssm_in_w, ssm_conv_w, ssm_conv_b, ssm_dt_bias, ssm_a_log, ssm_d, ssm_norm_w, ssm_out_w, gla_in_w, gla_gate_w1, gla_gate_w2, gla_gate_b, gla_norm_w, gla_out_w, gdn_in_w, gdn_conv_w, gdn_dt_bias, gdn_a_log, gdn_norm_w, gdn_out_w, final_norm_w):
    Bc, SEQ, D = x_prompt.shape
    Bs, LS, _ = x_sample.shape
    geo = Geo(D=D, Tc=Bc * SEQ, Ts=Bs * LS, SEQ=SEQ, LS=LS, Bc=Bc, Bs=Bs)
    assert SEQ % BLK == 0 and LS % BLK == 0 and Bs < 8
    depth = ada_w.shape[0]
    G, P, N = SSM_GROUPS, SSM_HEADDIM, SSM_STATE

    x = jnp.concatenate([x_prompt.reshape(geo.Tc, D), x_sample.reshape(geo.Ts, D)], axis=0)
    cond8 = jnp.zeros((8, D), f32).at[0].set(c_ctx).at[1:1 + Bs].set(c)
    mod_all = _ada_table(cond8, ada_w, ada_b)[:, :1 + Bs].reshape(depth, 1 + Bs, 1, 6 * D)

    st_ssm = jnp.zeros((Bc, ssm_in_w.shape[0], 2, G * 4, N, P), f32)
    st_gla = jnp.zeros((Bc, gla_in_w.shape[0], 2) + state_gla.shape[3:], f32)
    st_gdn = jnp.zeros((Bc, gdn_in_w.shape[0], 2) + state_gdn.shape[3:], f32)
    for i in range(depth):
        mod = mod_all[i]
        kind, j = i % 3, i // 3
        if kind == 0:
            w_in = ssm_in_w[j]
            di = ssm_out_w.shape[1]
            perm = _ssd_perm()
            w_main = w_in[:, np.concatenate([di + perm, np.arange(di)])].astype(bf16)
            proj, dtp = _in_proj(x, norm_mix_w[i], mod, 0, w_main, _pad_cols(w_in[:, di + 4096:]), geo)
            dtb, alog = _pad_row(ssm_dt_bias[j]), _pad_row(ssm_a_log[j])
            dskip = jnp.repeat(ssm_d[j], P).reshape(1, di)
            nw = ssm_norm_w[j].reshape(1, di)
            cw, cb = ssm_conv_w[j][:, perm], ssm_conv_b[j][perm].reshape(1, -1)
            h0 = state_ssm[:, j].reshape(Bs, 2, G, 4, N, P).transpose(1, 0, 2, 4, 3, 5).reshape(2, Bs, G, N, 4 * P)
            yf, pc, st_ssm = _ssd_scan(proj, None, dtp, cw, cb, dtb, alog, None, None, h0[0], None,
                                       st_ssm, j, geo, False)
            u, st_ssm = _ssd_scan(proj, pc, dtp, None, None, dtb, alog, dskip, nw, h0[1], yf,
                                  st_ssm, j, geo, True)
            w_out = ssm_out_w[j]
        elif kind == 1:
            w_small = _pad_cols(jnp.concatenate([gla_gate_w1[j, 0], gla_gate_w1[j, 1]], axis=1))
            proj, lrp = _in_proj(x, norm_mix_w[i], mod, 0, gla_in_w[j].astype(bf16), w_small, geo)
            w2 = gla_gate_w2[j]
            gb = gla_gate_b[j].reshape(2, 1, -1)
            nw = gla_norm_w[j].reshape(1, -1)
            yf, st_gla = _gla_scan(proj, lrp, w2, gb, None, state_gla[:, j, 0], None, st_gla, j, geo, False)
            u, st_gla = _gla_scan(proj, lrp, w2, gb, nw, state_gla[:, j, 1], yf, st_gla, j, geo, True)
            w_out = gla_out_w[j]
        else:
            w_in = gdn_in_w[j]
            perm = _gdn_perm(GDN_HP)
            w_main = w_in[:, np.concatenate([perm, 4096 + np.arange(2048)])].astype(bf16)
            proj, abp = _in_proj(x, norm_mix_w[i], mod, 0, w_main, _pad_cols(w_in[:, 6144:]), geo)
            dtb, alog = _pad_row(gdn_dt_bias[j]), _pad_row(gdn_a_log[j])
            nw = gdn_norm_w[j].reshape(1, -1)
            cw = gdn_conv_w[j][:, perm]
            yf, pc, st_gdn = _gdn_scan(proj, None, abp, cw, dtb, alog, None, state_gdn[:, j, 0], None,
                                       st_gdn, j, geo, False)
            u, st_gdn = _gdn_scan(proj, pc, abp, None, dtb, alog, nw, state_gdn[:, j, 1], yf,
                                  st_gdn, j, geo, True)
            w_out = gdn_out_w[j]
        x = _out_proj(u, w_out.astype(bf16), x, mod, 2, geo)

        proj, _ = _in_proj(x, norm_ffn_w[i], mod, 3, ffn_in_w[i].astype(bf16), None, geo)
        u = _ffn_conv(proj, ffn_conv_w[i].reshape(9, -1), ffn_conv_b[i], geo)
        x = _out_proj(u, ffn_out_w[i].astype(bf16), x, mod, 5, geo,
                      final_w=final_norm_w if i == depth - 1 else None)

    y_prompt = x[:geo.Tc].reshape(Bc, SEQ, D)
    y_sample = x[geo.Tc:].reshape(Bs, LS, D)
    return (y_prompt, y_sample, st_ssm, st_gla, st_gdn)
```

```python
import functools
from typing import NamedTuple

import numpy as np
import jax
import jax.numpy as jnp
from jax import lax
from jax.experimental import pallas as pl
from jax.experimental.pallas import tpu as pltpu

f32, bf16 = jnp.float32, jnp.bfloat16

BLK = 256
CHUNK = 64
SSD_CHUNK = 128
HALO = 16
GRID_W = 64
NORM_EPS = 1e-6
SMALL = 128
GLA_RANK = 16
GLA_NORMALIZER = 16.0
SSM_GROUPS = 8
SSM_HEADDIM = 64
SSM_STATE = 128
SSD_GP = 4
GDN_HP = 8
VMEM_LIMIT = 48 * 1024 * 1024

NN = (((1,), (0,)), ((), ()))
NT = (((1,), (1,)), ((), ()))
TN = (((0,), (0,)), ((), ()))


class Geo(NamedTuple):
    D: int
    Tc: int
    Ts: int
    SEQ: int
    LS: int
    Bc: int
    Bs: int

    @property
    def T(self):
        return self.Tc + self.Ts

    @property
    def nb(self):
        return self.T // BLK

    @property
    def ncb(self):
        return self.Tc // BLK

    @property
    def bpc(self):
        return self.SEQ // BLK

    @property
    def bps(self):
        return self.LS // BLK


def _mm(a, b, dims=NN):
    return lax.dot_general(a, b, dims, preferred_element_type=f32)


def _split3(a):
    hi = a.astype(bf16)
    r1 = a - hi.astype(f32)
    mid = r1.astype(bf16)
    lo = (r1 - mid.astype(f32)).astype(bf16)
    return hi, mid, lo


def _mm_x(a, b, dims=NN):
    if isinstance(a, tuple):
        ps = [_mm(p, b, dims) for p in a]
    elif a.dtype == bf16:
        ps = [_mm(a, p, dims) for p in _split3(b)]
    else:
        ps = [_mm(p, b, dims) for p in _split3(a)]
    return ps[0] + ps[1] + ps[2]


def _mm_hi(a, b):
    ah = a.astype(bf16)
    al = (a - ah.astype(f32)).astype(bf16)
    bh = b.astype(bf16)
    bl = (b - bh.astype(f32)).astype(bf16)
    return _mm(ah, bh) + _mm(al, bh) + _mm(ah, bl)


def _silu(x):
    h = 0.5 * x
    return h + h * jnp.tanh(h)


def _tri_mask(n, upper, chunk=None, strict=False):
    r = lax.broadcasted_iota(jnp.int32, (n, n), 0)
    c = lax.broadcasted_iota(jnp.int32, (n, n), 1)
    if upper:
        m = (c > r) if strict else (c >= r)
    else:
        m = (c < r) if strict else (c <= r)
    if chunk is not None and chunk < n:
        sh = int(np.log2(chunk))
        same = lax.shift_right_logical(r, sh) == lax.shift_right_logical(c, sh)
        m = jnp.logical_and(m, same)
    return m


def _as_bf16(mask):
    return jnp.where(mask, 1.0, 0.0).astype(bf16)


def _ada_kernel(c_ref, w_ref, b_ref, o_ref):
    o_ref[0] = _mm_hi(_silu(c_ref[...]), w_ref[0]) + b_ref[0]


def _ada_table(cond8, ada_w, ada_b):
    depth, D, N6 = ada_w.shape
    tn = N6 // 4
    return pl.pallas_call(
        _ada_kernel,
        out_shape=jax.ShapeDtypeStruct((depth, 8, N6), f32),
        grid=(depth, N6 // tn),
        in_specs=[pl.BlockSpec((8, D), lambda l, j: (0, 0)),
                  pl.BlockSpec((1, D, tn), lambda l, j: (l, 0, j)),
                  pl.BlockSpec((1, 1, tn), lambda l, j: (l, 0, j))],
        out_specs=pl.BlockSpec((1, 8, tn), lambda l, j: (l, 0, j)),
        compiler_params=pltpu.CompilerParams(
            dimension_semantics=("parallel", "parallel"), vmem_limit_bytes=VMEM_LIMIT),
        name="ada_table",
    )(cond8, ada_w, ada_b.reshape(depth, 1, N6))


def _in_proj_kernel(*refs, has_small, tn, dst_blk):
    if has_small:
        x_ref, nw_ref, sh_ref, sc_ref, w_ref, ws_ref, o_ref, os_ref = refs
    else:
        x_ref, nw_ref, sh_ref, sc_ref, w_ref, o_ref = refs
    x = x_ref[...]
    h = x * lax.rsqrt(jnp.mean(x * x, axis=-1, keepdims=True) + NORM_EPS) * nw_ref[...]
    h = h * (1.0 + sc_ref[0]) + sh_ref[0]
    hb = h.astype(bf16)
    if has_small:
        ws = ws_ref[...]
        wh = ws.astype(bf16)
        wl = (ws - wh.astype(f32)).astype(bf16)
        hl = (h - hb.astype(f32)).astype(bf16)
        os_ref[...] = _mm(hb, wh) + _mm(hl, wh) + _mm(hb, wl)
    per = tn // 128
    for n in range(w_ref.shape[1] // tn):
        r = _mm(hb, w_ref[:, n * tn:(n + 1) * tn]).astype(o_ref.dtype)
        if dst_blk is None:
            o_ref[:, n * tn:(n + 1) * tn] = r
            continue
        k = 0
        while k < per:
            d0 = dst_blk[n * per + k]
            run = 1
            while k + run < per and dst_blk[n * per + k + run] == d0 + run:
                run += 1
            o_ref[:, d0 * 128:(d0 + run) * 128] = r[:, k * 128:(k + run) * 128]
            k += run


def _row_tile(geo, pref):
    for tm in (pref, 512, 256):
        if tm <= pref and geo.Tc % tm == 0 and geo.LS % tm == 0:
            return tm
    raise ValueError("token counts must be multiples of 256")


def _cond_of_tile(i, geo, tm):
    nct = geo.Tc // tm
    return jnp.where(i < nct, 0, 1 + jnp.maximum(i - nct, 0) // (geo.LS // tm))


def _in_proj(x, norm_w, mod, shift_idx, w_main, w_small, geo, tn=512, dst_blk=None):
    T, D = x.shape
    N = w_main.shape[1]
    tm = _row_tile(geo, 512)
    assert N % tn == 0
    has_small = w_small is not None
    cond = functools.partial(_cond_of_tile, geo=geo, tm=tm)
    in_specs = [pl.BlockSpec((tm, D), lambda i: (i, 0)),
                pl.BlockSpec((1, D), lambda i: (0, 0)),
                pl.BlockSpec((1, 1, D), lambda i: (cond(i), 0, shift_idx)),
                pl.BlockSpec((1, 1, D), lambda i: (cond(i), 0, shift_idx + 1)),
                pl.BlockSpec((D, N), lambda i: (0, 0), pipeline_mode=pl.Buffered(1))]
    args = [x, norm_w.reshape(1, D), mod, mod, w_main]
    out_shape = [jax.ShapeDtypeStruct((T, N), bf16)]
    out_specs = [pl.BlockSpec((tm, N), lambda i: (i, 0))]
    if has_small:
        in_specs.append(pl.BlockSpec((D, SMALL), lambda i: (0, 0)))
        args.append(w_small)
        out_shape.append(jax.ShapeDtypeStruct((T, SMALL), f32))
        out_specs.append(pl.BlockSpec((tm, SMALL), lambda i: (i, 0)))
    res = pl.pallas_call(
        functools.partial(_in_proj_kernel, has_small=has_small, tn=tn, dst_blk=dst_blk),
        out_shape=out_shape,
        grid=(T // tm,),
        in_specs=in_specs,
        out_specs=out_specs,
        compiler_params=pltpu.CompilerParams(
            dimension_semantics=("parallel",), vmem_limit_bytes=VMEM_LIMIT),
        name="in_proj",
    )(*args)
    return res if has_small else (res[0], None)


def _out_proj_kernel(*refs, n_ctx_tiles):
    final = n_ctx_tiles is not None
    if final:
        u_ref, w_ref, x_ref, g_ref, fw_ref, oc_ref, os_ref = refs
    else:
        u_ref, w_ref, x_ref, g_ref, o_ref = refs
    r = x_ref[...] + g_ref[0] * _mm(u_ref[...], w_ref[...])
    if not final:
        o_ref[...] = r
        return
    r = r * lax.rsqrt(jnp.mean(r * r, axis=-1, keepdims=True) + NORM_EPS) * fw_ref[...]

    @pl.when(pl.program_id(0) < n_ctx_tiles)
    def _():
        oc_ref[...] = r

    @pl.when(pl.program_id(0) >= n_ctx_tiles)
    def _():
        os_ref[...] = r


def _out_proj(u, w, x, mod, gate_idx, geo, final_w=None):
    T, K = u.shape
    D = w.shape[1]
    tm = _row_tile(geo, 512)
    cond = functools.partial(_cond_of_tile, geo=geo, tm=tm)
    final = final_w is not None
    nct = geo.Tc // tm
    in_specs = [pl.BlockSpec((tm, K), lambda i: (i, 0)),
                pl.BlockSpec((K, D), lambda i: (0, 0)),
                pl.BlockSpec((tm, D), lambda i: (i, 0)),
                pl.BlockSpec((1, 1, D), lambda i: (cond(i), 0, gate_idx))]
    args = [u, w, x, mod]
    if final:
        in_specs.append(pl.BlockSpec((1, D), lambda i: (0, 0)))
        args.append(final_w.reshape(1, D))
        out_shape = [jax.ShapeDtypeStruct((geo.Tc, D), f32), jax.ShapeDtypeStruct((geo.Ts, D), f32)]
        out_specs = [pl.BlockSpec((tm, D), lambda i: (jnp.minimum(i, nct - 1), 0)),
                     pl.BlockSpec((tm, D), lambda i: (jnp.maximum(i - nct, 0), 0))]
    else:
        out_shape = jax.ShapeDtypeStruct((T, D), f32)
        out_specs = pl.BlockSpec((tm, D), lambda i: (i, 0))
    return pl.pallas_call(
        functools.partial(_out_proj_kernel, n_ctx_tiles=nct if final else None),
        out_shape=out_shape,
        grid=(T // tm,),
        in_specs=in_specs,
        out_specs=out_specs,
        compiler_params=pltpu.CompilerParams(
            dimension_semantics=("arbitrary" if final else "parallel",), vmem_limit_bytes=VMEM_LIMIT),
        name="out_proj",
    )(*args)


def _ffn_conv_kernel(a_ref, v_ref, w_ref, b_ref, o_ref, pad_ref, *, n_ctx_img, IB, SEQ):
    PADR = 128
    tf = a_ref.shape[1]
    pad_ref[0:PADR, :] = jnp.zeros((PADR, tf), f32)
    pad_ref[PADR + IB:PADR + IB + PADR, :] = jnp.zeros((PADR, tf), f32)
    pad_ref[PADR:PADR + IB, :] = a_ref[...].astype(f32)
    w = w_ref[...]
    b = b_ref[...]
    row = lax.broadcasted_iota(jnp.int32, (BLK, tf), 0)

    def taps(s, pos, width, kr):
        left = jnp.where(pos > 0, pad_ref[s - 1:s - 1 + BLK, :], 0.0)
        right = jnp.where(pos < width - 1, pad_ref[s + 1:s + 1 + BLK, :], 0.0)
        return (left * w[3 * kr:3 * kr + 1] + pad_ref[s:s + BLK, :] * w[3 * kr + 1:3 * kr + 2]
                + right * w[3 * kr + 2:3 * kr + 3])

    def finish(c, acc):
        v = v_ref[c * BLK:(c + 1) * BLK, :].astype(f32)
        o_ref[c * BLK:(c + 1) * BLK, :] = (_silu(acc + b) * v).astype(bf16)

    @pl.when(pl.program_id(0) < n_ctx_img)
    def _():
        for c in range(IB // BLK):
            pos = jnp.bitwise_and(row + (c * BLK) % SEQ, SEQ - 1)
            finish(c, taps(PADR + c * BLK, pos, SEQ, 1))

    @pl.when(pl.program_id(0) >= n_ctx_img)
    def _():
        col = jnp.bitwise_and(row, GRID_W - 1)
        for c in range(IB // BLK):
            s = PADR + c * BLK
            acc = taps(s - GRID_W, col, GRID_W, 0) + taps(s, col, GRID_W, 1) + taps(s + GRID_W, col, GRID_W, 2)
            finish(c, acc)


def _ffn_conv(proj, conv_w9, conv_b, geo, tf=256):
    T, F2 = proj.shape
    F = F2 // 2
    IB = geo.LS
    assert F % tf == 0 and geo.Tc % IB == 0 and IB % BLK == 0
    assert geo.SEQ & (geo.SEQ - 1) == 0 and BLK % GRID_W == 0
    nf = F // tf
    return pl.pallas_call(
        functools.partial(_ffn_conv_kernel, n_ctx_img=geo.Tc // IB, IB=IB, SEQ=geo.SEQ),
        out_shape=jax.ShapeDtypeStruct((T, F), bf16),
        grid=(T // IB, nf),
        in_specs=[pl.BlockSpec((IB, tf), lambda i, j: (i, j)),
                  pl.BlockSpec((IB, tf), lambda i, j: (i, nf + j)),
                  pl.BlockSpec((9, tf), lambda i, j: (0, j)),
                  pl.BlockSpec((1, tf), lambda i, j: (0, j))],
        out_specs=pl.BlockSpec((IB, tf), lambda i, j: (i, j)),
        scratch_shapes=[pltpu.VMEM((IB + 256, tf), f32)],
        compiler_params=pltpu.CompilerParams(
            dimension_semantics=("parallel", "parallel"), vmem_limit_bytes=VMEM_LIMIT),
        name="ffn_conv",
    )(proj, proj, conv_w9, conv_b.reshape(1, F))


def _blk_idx(i, geo, rev):
    return (geo.nb - 1 - i) if rev else i


def _tok_spec(cw, colfn, geo, rev):
    return pl.BlockSpec((BLK, cw), lambda u, i: (_blk_idx(i, geo, rev), colfn(u)))


def _halo_specs(cw, colfn, geo, rev):
    per = BLK // HALO
    last = geo.T // HALO - 1
    prev = pl.BlockSpec((HALO, cw), lambda u, i: (jnp.maximum(_blk_idx(i, geo, rev) * per - 1, 0), colfn(u)))
    nxt = pl.BlockSpec((HALO, cw), lambda u, i: (jnp.minimum((_blk_idx(i, geo, rev) + 1) * per, last), colfn(u)))
    return prev, nxt


def _h0_spec(geo, rev, hp):
    def idx(u, i):
        ip = _blk_idx(i, geo, rev)
        return (jnp.maximum(ip - geo.ncb, 0) // geo.bps, u, 0, 0)
    return pl.BlockSpec((1, hp, 128, 256), idx)


def _st_spec(geo, rev, layer, blk):
    d = 1 if rev else 0

    def idx(u, i):
        ip = _blk_idx(i, geo, rev)
        return (jnp.minimum(ip // geo.bpc, geo.Bc - 1), layer, d, u, 0, 0)
    return pl.BlockSpec((1, 1, 1) + blk, idx)


def _param_spec(rows, cw, colfn):
    return pl.BlockSpec((rows, cw), lambda u, i: (0, colfn(u)))


def _flags(geo, rev):
    ip = _blk_idx(pl.program_id(1), geo, rev)
    is_ctx = ip < geo.ncb
    pos = jnp.where(is_ctx, lax.rem(ip, geo.bpc), lax.rem(jnp.maximum(ip - geo.ncb, 0), geo.bps))
    n = jnp.where(is_ctx, geo.bpc, geo.bps)
    first = pos == 0
    last = pos == n - 1
    return is_ctx, first, last, (last if rev else first)


def _conv3_silu(cur_ref, prev_ref, next_ref, w_ref, b_ref, first, last):
    cur = cur_ref[...].astype(f32)
    n = cur.shape[0]
    row = lax.broadcasted_iota(jnp.int32, cur.shape, 0)
    pr = jnp.where(first, 0.0, prev_ref[HALO - 1:HALO, :].astype(f32))
    nx = jnp.where(last, 0.0, next_ref[0:1, :].astype(f32))
    xm = jnp.where(row == 0, pr, pltpu.roll(cur, 1, axis=0))
    xp = jnp.where(row == n - 1, nx, pltpu.roll(cur, n - 1, axis=0))
    w = w_ref[...]
    y = xm * w[0:1] + cur * w[1:2] + xp * w[2:3]
    if b_ref is not None:
        y = y + b_ref[...]
    return _silu(y)


def _init_state(S, h0_ref, is_ctx, start):
    @pl.when(start)
    def _():
        S[...] = jnp.where(is_ctx, 0.0, h0_ref[0])


def _norm_gate(o, gate, nw, gate_first):
    g = _silu(gate)
    if gate_first:
        o = o * g
    o = o * lax.rsqrt(jnp.mean(o * o, axis=-1, keepdims=True) + NORM_EPS) * nw
    if not gate_first:
        o = o * g
    return o


def _scan_call(kernel, name, grid_u, in_specs, args, outs, st, st_blk, layer, geo, rev, hp):
    out_shape = [jax.ShapeDtypeStruct((geo.T, c), dt) for c, dt, _ in outs]
    out_specs = [_tok_spec(bc, lambda u: u, geo, rev) for _, _, bc in outs]
    return pl.pallas_call(
        kernel,
        out_shape=out_shape + [jax.ShapeDtypeStruct(st.shape, st.dtype)],
        grid=(grid_u, geo.nb),
        in_specs=in_specs + [pl.BlockSpec(memory_space=pl.ANY)],
        out_specs=out_specs + [_st_spec(geo, rev, layer, st_blk)],
        scratch_shapes=[pltpu.VMEM((hp, 128, 256), f32)],
        input_output_aliases={len(args): len(outs)},
        compiler_params=pltpu.CompilerParams(
            dimension_semantics=("parallel", "arbitrary"), vmem_limit_bytes=VMEM_LIMIT),
        name=name,
    )(*args, st)


def _ssd_kernel(*refs, rev, geo, gp):
    if rev:
        (pc_ref, dt_ref, dtb_ref, alog_ref, xsel_ref, xh_ref, h0_ref,
         dsk_ref, z_ref, nw_ref, yf_ref, _, y_ref, st_ref, S) = refs
    else:
        (xbc_c, xbc_p, xbc_n, dt_ref, cw_ref, cb_ref, dtb_ref, alog_ref, xsel_ref, xh_ref, h0_ref,
         _, y_ref, pc_ref, st_ref, S) = refs
    is_ctx, first, last, start = _flags(geo, rev)
    _init_state(S, h0_ref, is_ctx, start)
    Q = SSD_CHUNK
    nch = BLK // Q
    groups = range(gp)

    if rev:
        xbc = pc_ref[...].astype(f32)
    else:
        xbc = _conv3_silu(xbc_c, xbc_p, xbc_n, cw_ref, cb_ref, first, last)
        pc_ref[...] = xbc.astype(bf16)
    dt = jax.nn.softplus(dt_ref[...] + dtb_ref[...])
    la = dt * (-jnp.exp(alog_ref[...]))
    tri = _tri_mask(BLK, rev, chunk=Q)
    cs3 = _split3(_mm_x(_as_bf16(tri), la))
    dtb16 = dt.astype(bf16)
    tri_q = _tri_mask(Q, rev)

    xs, xvs, bms, cms, cses, csts = [], [], [], [], [], []
    for g in groups:
        x = xbc[:, g * 512:g * 512 + 256]
        xs.append(x)
        bms.append(xbc[:, g * 512 + 256:g * 512 + 384].astype(bf16))
        cms.append(xbc[:, g * 512 + 384:g * 512 + 512].astype(bf16))
        xsel = xsel_ref[g]
        xvs.append(x * _mm(dtb16, xsel))
        cses.append(_mm_x(cs3, xsel))
        csts.append(_mm_x(cs3, xh_ref[g]).T)
    Svs = [S[g] for g in groups]
    youts = [[None] * nch for _ in groups]
    for c in (range(nch - 1, -1, -1) if rev else range(nch)):
        lo = c * Q
        li = lo if rev else lo + Q - 1
        for g in groups:
            cs_e = cses[g][lo:lo + Q]
            xv = xvs[g][lo:lo + Q]
            xvb = xv.astype(bf16)
            cmb = cms[g][lo:lo + Q]
            bmb = bms[g][lo:lo + Q]
            gm = _mm(cmb, bmb, NT)
            ys = []
            for r in range(4):
                seg = cs_e[:, r * 64:r * 64 + 1] - csts[g][r:r + 1, lo:lo + Q]
                m = jnp.where(tri_q, gm * jnp.exp(seg), 0.0)
                ys.append(_mm(m.astype(bf16), xvb[:, r * 64:(r + 1) * 64]))
            Sv = Svs[g]
            y = jnp.concatenate(ys, axis=1) + _mm(cmb, Sv.astype(bf16)) * jnp.exp(cs_e)
            cl = cses[g][li:li + 1, :]
            xw = xv * jnp.exp(cl - cs_e)
            Svs[g] = Sv * jnp.exp(cl) + _mm(bmb, xw.astype(bf16), TN)
            youts[g][c] = y

    for g in groups:
        S[g] = Svs[g]
        y = jnp.concatenate(youts[g], axis=0)
        if rev:
            cols = slice(g * 256, (g + 1) * 256)
            o = yf_ref[:, cols] + y + xs[g] * dsk_ref[:, cols]
            y = _norm_gate(o, z_ref[:, cols].astype(f32), nw_ref[:, cols], gate_first=True)
        y_ref[:, g * 256:(g + 1) * 256] = y.astype(y_ref.dtype)

    @pl.when(is_ctx)
    def _():
        for g in groups:
            for r in range(4):
                st_ref[0, 0, 0, g * 4 + r] = Svs[g][:, r * 64:(r + 1) * 64]


def _ssd_scan(proj, pc, dtp, conv_w, conv_b, dtb, alog, dskip, norm_w, h0, yf, st, layer, geo, rev):
    G, gp = SSM_GROUPS, SSD_GP
    nu = G // gp
    d = 1 if rev else 0
    xsel = np.zeros((G, 128, 256), np.float32)
    xh = np.zeros((G, 128, 128), np.float32)
    for g in range(G):
        for r in range(4):
            xsel[g, d * 32 + 4 * g + r, r * 64:(r + 1) * 64] = 1.0
            xh[g, d * 32 + 4 * g + r, r] = 1.0
    cw, zw = 512 * gp, 256 * gp
    head_specs = [_param_spec(1, SMALL, lambda u: 0), _param_spec(1, SMALL, lambda u: 0),
                  pl.BlockSpec((gp, 128, 256), lambda u, i: (u, 0, 0)),
                  pl.BlockSpec((gp, 128, 128), lambda u, i: (u, 0, 0)),
                  _h0_spec(geo, rev, gp)]
    head_args = [dtb, alog, jnp.asarray(xsel, bf16), jnp.asarray(xh, bf16), h0]
    if rev:
        in_specs = [_tok_spec(cw, lambda u: u, geo, rev), _tok_spec(SMALL, lambda u: 0, geo, rev), *head_specs,
                    _param_spec(1, zw, lambda u: u), _tok_spec(zw, lambda u: 2 * nu + u, geo, rev),
                    _param_spec(1, zw, lambda u: u), _tok_spec(zw, lambda u: u, geo, rev)]
        args = [pc, dtp, *head_args, dskip, proj, norm_w, yf]
        outs = [(2048, bf16, zw)]
    else:
        in_specs = [_tok_spec(cw, lambda u: u, geo, rev), *_halo_specs(cw, lambda u: u, geo, rev),
                    _tok_spec(SMALL, lambda u: 0, geo, rev),
                    _param_spec(3, cw, lambda u: u), _param_spec(1, cw, lambda u: u), *head_specs]
        args = [proj, proj, proj, dtp, conv_w, conv_b, *head_args]
        outs = [(2048, f32, zw), (4096, bf16, cw)]
    return _scan_call(functools.partial(_ssd_kernel, rev=rev, geo=geo, gp=gp), "ssd_bwd" if rev else "ssd_fwd",
                      nu, in_specs, args, outs, st, (4 * gp, SSM_STATE, SSM_HEADDIM), layer, geo, rev, gp)


def _gla_kernel(*refs, rev, geo, nh):
    qkv_ref, lr_ref, w2_ref, gb_ref, h0_ref = refs[:5]
    if rev:
        r_ref, nw_ref, yf_ref, _, y_ref, st_ref, S = refs[5:]
    else:
        _, y_ref, st_ref, S = refs[5:]
    is_ctx, first, last, start = _flags(geo, rev)
    _init_state(S, h0_ref, is_ctx, start)
    d = 1 if rev else 0
    dk, dv = 128, 256
    heads = range(nh)
    K = nh * dk

    q_all = qkv_ref[:, 0:K].astype(f32) * dk ** -0.5
    k_all = qkv_ref[:, K:2 * K].astype(f32)
    vb_all = qkv_ref[:, 2 * K:2 * K + nh * dv]
    lr = lr_ref[...][:, d * GLA_RANK:(d + 1) * GLA_RANK]
    logit = _mm_hi(lr, w2_ref[0]) + gb_ref[0]
    lg = jax.nn.log_sigmoid(logit) * (1.0 / GLA_NORMALIZER)
    tri = _tri_mask(BLK, rev, chunk=CHUNK)
    cs = _mm_x(_as_bf16(tri), lg)
    cs_t = _mm_x(lg, _as_bf16(_tri_mask(BLK, not rev, chunk=CHUNK)), TN)
    qgb = (q_all * jnp.exp(cs)).astype(bf16)
    kgb = (k_all * jnp.exp(-cs)).astype(bf16)

    ys = []
    for h in heads:
        sc = jnp.where(tri, _mm(qgb[:, h * dk:(h + 1) * dk], kgb[:, h * dk:(h + 1) * dk], NT), 0.0)
        ys.append(_mm(sc.astype(bf16), vb_all[:, h * dv:(h + 1) * dv]))
    Svs = [S[h] for h in heads]
    nch = BLK // CHUNK
    yin = [[None] * nch for _ in heads]
    for c in (range(nch - 1, -1, -1) if rev else range(nch)):
        lo = c * CHUNK
        li = lo if rev else lo + CHUNK - 1
        kw_all = (k_all[lo:lo + CHUNK] * jnp.exp(cs[li:li + 1, :] - cs[lo:lo + CHUNK])).astype(bf16)
        for h in heads:
            yin[h][c] = _mm(qgb[lo:lo + CHUNK, h * dk:(h + 1) * dk], Svs[h].astype(bf16))
        for h in heads:
            Svs[h] = (Svs[h] * jnp.exp(cs_t[h * dk:(h + 1) * dk, li:li + 1])
                      + _mm(kw_all[:, h * dk:(h + 1) * dk], vb_all[lo:lo + CHUNK, h * dv:(h + 1) * dv], TN))

    for h in heads:
        S[h] = Svs[h]
        y = ys[h] + jnp.concatenate(yin[h], axis=0)
        cols = slice(h * dv, (h + 1) * dv)
        if rev:
            y = _norm_gate(yf_ref[:, cols] + y, r_ref[:, cols].astype(f32), nw_ref[...], gate_first=False)
        y_ref[:, cols] = y.astype(y_ref.dtype)

    @pl.when(is_ctx)
    def _():
        for h in heads:
            st_ref[0, 0, 0, h] = Svs[h]


def _gla_scan(proj, lrp, w2, gb, norm_w, h0, yf, st, layer, geo, rev):
    H = 4
    d = 1 if rev else 0
    in_specs = [_tok_spec(2048, lambda u: 0, geo, rev), _tok_spec(SMALL, lambda u: 0, geo, rev),
                pl.BlockSpec((1, GLA_RANK, 512), lambda u, i: (d, 0, 0)),
                pl.BlockSpec((1, 1, 512), lambda u, i: (d, 0, 0)),
                _h0_spec(geo, rev, H)]
    args = [proj, lrp, w2, gb, h0]
    if rev:
        in_specs += [_tok_spec(1024, lambda u: 2, geo, rev), _param_spec(1, 256, lambda u: 0),
                     _tok_spec(1024, lambda u: 0, geo, rev)]
        args += [proj, norm_w, yf]
    return _scan_call(functools.partial(_gla_kernel, rev=rev, geo=geo, nh=H), "gla_bwd" if rev else "gla_fwd",
                      1, in_specs, args, [(1024, bf16 if rev else f32, 1024)], st, (H, 128, 256), layer, geo, rev, H)


def _unit_tri_inverse(ms, lev):
    eye = jnp.where(lev == -1, 1.0, 0.0)
    dinv = [eye - jnp.where(lev == 0, m, 0.0) for m in ms]
    for b in range(1, 6):
        ts = [_mm(jnp.where(lev == b, m, 0.0).astype(bf16), d.astype(bf16)).astype(bf16)
              for m, d in zip(ms, dinv)]
        dinv = [d - _mm(d.astype(bf16), t) for d, t in zip(dinv, ts)]
    return dinv


def _gdn_kernel(*refs, rev, geo, hp):
    if rev:
        (pc_ref, ab_ref, dtb_ref, alog_ref, xab_ref, lev_ref, h0_ref,
         z_ref, nw_ref, yf_ref, _, y_ref, st_ref, S) = refs
    else:
        (qkv_c, qkv_p, qkv_n, cw_ref, ab_ref, dtb_ref, alog_ref, xab_ref, lev_ref, h0_ref,
         _, y_ref, pc_ref, st_ref, S) = refs
    is_ctx, first, last, start = _flags(geo, rev)
    _init_state(S, h0_ref, is_ctx, start)
    dk, dv = 128, 256
    heads = range(hp)

    if rev:
        qkv = pc_ref[...].astype(f32)
    else:
        qkv = _conv3_silu(qkv_c, qkv_p, qkv_n, cw_ref, None, first, last)

    ab = ab_ref[...]
    gall = -jnp.exp(alog_ref[...]) * jax.nn.softplus(ab + dtb_ref[...])
    lane = lax.broadcasted_iota(jnp.int32, ab.shape, 1)
    sel = _mm_x(jnp.where(lane < 16, gall, jax.nn.sigmoid(ab)), xab_ref[0])
    lev = lev_ref[...]
    r_i = lax.broadcasted_iota(jnp.int32, (BLK, BLK), 0)
    c_i = lax.broadcasted_iota(jnp.int32, (BLK, BLK), 1)
    in_chunk = lev < 6
    incl = jnp.logical_and(in_chunk, (c_i >= r_i) if rev else (c_i <= r_i))
    incl_t = jnp.logical_and(in_chunk, (c_i <= r_i) if rev else (c_i >= r_i))
    strict = jnp.logical_and(incl, lev >= 0)
    cs_all = _mm_x(_as_bf16(incl), sel)
    cs_rows = _mm_x(sel, _as_bf16(incl_t), TN)

    qs, ks, kbfs, kbs, vbs, css, lmasks, ms = [], [], [], [], [], [], [], []
    for hh in heads:
        q = qkv[:, hh * dk:(hh + 1) * dk]
        k = qkv[:, (hp + hh) * dk:(hp + hh + 1) * dk]
        v = qkv[:, 2 * hp * dk + hh * dv:2 * hp * dk + (hh + 1) * dv]
        if not rev:
            q = q * lax.rsqrt(jnp.sum(q * q, axis=-1, keepdims=True) + 1e-6) * dk ** -0.5
            k = k * lax.rsqrt(jnp.sum(k * k, axis=-1, keepdims=True) + 1e-6)
            pc_ref[:, hh * dk:(hh + 1) * dk] = q.astype(bf16)
            pc_ref[:, (hp + hh) * dk:(hp + hh + 1) * dk] = k.astype(bf16)
            pc_ref[:, 2 * hp * dk + hh * dv:2 * hp * dk + (hh + 1) * dv] = v.astype(bf16)
        beta = sel[:, 2 * hh + 1:2 * hh + 2]
        cs = cs_all[:, 2 * hh:2 * hh + 1]
        lmask = jnp.where(incl, jnp.exp(cs - cs_rows[2 * hh:2 * hh + 1, :]), 0.0)
        kb = k * beta
        kbf = k.astype(bf16)
        qs.append(q)
        ks.append(k)
        kbfs.append(kbf)
        kbs.append(kb)
        vbs.append((v * beta).astype(bf16))
        css.append(cs)
        lmasks.append(lmask)
        ms.append(jnp.where(strict, _mm(kb.astype(bf16), kbf, NT) * lmask, 0.0))
    tbs = [t.astype(bf16) for t in _unit_tri_inverse(ms, lev)]
    ecs = [jnp.exp(cs) for cs in css]
    us = [_mm(tbs[h], vbs[h]) for h in heads]
    wks = [_mm(tbs[h], (kbs[h] * ecs[h]).astype(bf16)).astype(bf16) for h in heads]
    a_ins = [(_mm(qs[h].astype(bf16), kbfs[h], NT) * lmasks[h]).astype(bf16) for h in heads]
    qds = [(qs[h] * ecs[h]).astype(bf16) for h in heads]

    Svs = [S[h] for h in heads]
    nch = BLK // CHUNK
    outs = [[None] * nch for _ in heads]
    for c in (range(nch - 1, -1, -1) if rev else range(nch)):
        lo = c * CHUNK
        li = lo if rev else lo + CHUNK - 1
        Sbs = [Sv.astype(bf16) for Sv in Svs]
        v_news = [(us[h][lo:lo + CHUNK] - _mm(wks[h][lo:lo + CHUNK], Sbs[h])).astype(bf16) for h in heads]
        for h in heads:
            outs[h][c] = (_mm(qds[h][lo:lo + CHUNK], Sbs[h])
                          + _mm(a_ins[h][lo:lo + CHUNK, lo:lo + CHUNK], v_news[h]))
        for h in heads:
            cl = css[h][li:li + 1, :]
            kd = ks[h][lo:lo + CHUNK] * jnp.exp(cl - css[h][lo:lo + CHUNK])
            Svs[h] = Svs[h] * jnp.exp(cl) + _mm(kd.astype(bf16), v_news[h], TN)

    for hh in heads:
        S[hh] = Svs[hh]
        o = jnp.concatenate(outs[hh], axis=0)
        cols = slice(hh * dv, (hh + 1) * dv)
        if rev:
            o = _norm_gate(yf_ref[:, cols] + o, z_ref[:, cols].astype(f32), nw_ref[...], gate_first=False)
        y_ref[:, cols] = o.astype(y_ref.dtype)

    @pl.when(is_ctx)
    def _():
        for hh in heads:
            st_ref[0, 0, 0, hh] = Svs[hh]


def _gdn_scan(proj, pc, abp, conv_w, dtb, alog, norm_w, h0, yf, st, layer, geo, rev):
    H, hp = 8, GDN_HP
    nu = H // hp
    d = 1 if rev else 0
    xab = np.zeros((nu, 128, 128), np.float32)
    for h in range(H):
        xab[h // hp, d * H + h, 2 * (h % hp)] = 1.0
        xab[h // hp, 2 * H + d * H + h, 2 * (h % hp) + 1] = 1.0
    idx = np.arange(BLK)
    x = idx[:, None] ^ idx[None, :]
    lev = np.where(x == 0, -1, np.floor(np.log2(np.maximum(x, 1)))).astype(np.int32)
    cw, vw = 512 * hp, 256 * hp
    head_specs = [_tok_spec(SMALL, lambda u: 0, geo, rev),
                  _param_spec(1, SMALL, lambda u: 0), _param_spec(1, SMALL, lambda u: 0),
                  pl.BlockSpec((1, 128, 128), lambda u, i: (u, 0, 0)),
                  pl.BlockSpec((BLK, BLK), lambda u, i: (0, 0)),
                  _h0_spec(geo, rev, hp)]
    head_args = [abp, dtb, alog, jnp.asarray(xab, bf16), jnp.asarray(lev), h0]
    if rev:
        in_specs = [_tok_spec(cw, lambda u: u, geo, rev), *head_specs,
                    _tok_spec(vw, lambda u: 2 * nu + u, geo, rev), _param_spec(1, 256, lambda u: 0),
                    _tok_spec(vw, lambda u: u, geo, rev)]
        args = [pc, *head_args, proj, norm_w, yf]
        outs = [(2048, bf16, vw)]
    else:
        in_specs = [_tok_spec(cw, lambda u: u, geo, rev), *_halo_specs(cw, lambda u: u, geo, rev),
                    _param_spec(3, cw, lambda u: u), *head_specs]
        args = [proj, proj, proj, conv_w, *head_args]
        outs = [(2048, f32, vw), (4096, bf16, cw)]
    return _scan_call(functools.partial(_gdn_kernel, rev=rev, geo=geo, hp=hp), "gdn_bwd" if rev else "gdn_fwd",
                      nu, in_specs, args, outs, st, (hp, 128, 256), layer, geo, rev, hp)


def _pad_cols(w, n=SMALL):
    return jnp.pad(w.astype(f32), ((0, 0), (0, n - w.shape[1])))


def _pad_row(v, n=SMALL):
    v = v.reshape(1, -1).astype(f32)
    return jnp.pad(v, ((0, 0), (0, n - v.shape[1])))


def _ssd_perm():
    perm = []
    for g in range(SSM_GROUPS):
        perm += list(range(g * 256, (g + 1) * 256))
        perm += list(range(2048 + g * 128, 2048 + (g + 1) * 128))
        perm += list(range(3072 + g * 128, 3072 + (g + 1) * 128))
    return np.asarray(perm, np.int32)


def _ssd_dst_blocks():
    dst = [32 + i for i in range(16)]
    dst += [4 * (i // 2) + i % 2 for i in range(16)]
    dst += [4 * g + 2 for g in range(SSM_GROUPS)]
    dst += [4 * g + 3 for g in range(SSM_GROUPS)]
    return tuple(dst)


def _gdn_dst_blocks(hp):
    inv = np.argsort(_gdn_perm(hp)[::128] // 128)
    return tuple(int(b) for b in inv) + tuple(range(32, 48))


def _gdn_perm(hp):
    perm = []
    for u in range(8 // hp):
        perm += list(range(u * hp * 128, (u + 1) * hp * 128))
        perm += list(range(1024 + u * hp * 128, 1024 + (u + 1) * hp * 128))
        perm += list(range(2048 + u * hp * 256, 2048 + (u + 1) * hp * 256))
    return np.asarray(perm, np.int32)


def kernel(x_prompt, x_sample, state_ssm, state_gla, state_gdn, c, c_ctx, norm_mix_w, norm_ffn_w, ada_w, ada_b, ffn_in_w, ffn_conv_w, ffn_conv_b, ffn_out_w, ssm_in_w, ssm_conv_w, ssm_conv_b, ssm_dt_bias, ssm_a_log, ssm_d, ssm_norm_w, ssm_out_w, gla_in_w, gla_gate_w1, gla_gate_w2, gla_gate_b, gla_norm_w, gla_out_w, gdn_in_w, gdn_conv_w, gdn_dt_bias, gdn_a_log, gdn_norm_w, gdn_out_w, final_norm_w):
    Bc, SEQ, D = x_prompt.shape
    Bs, LS, _ = x_sample.shape
    geo = Geo(D=D, Tc=Bc * SEQ, Ts=Bs * LS, SEQ=SEQ, LS=LS, Bc=Bc, Bs=Bs)
    assert SEQ % BLK == 0 and LS % BLK == 0 and Bs < 8
    depth = ada_w.shape[0]
    G, P, N = SSM_GROUPS, SSM_HEADDIM, SSM_STATE

    x = jnp.concatenate([x_prompt.reshape(geo.Tc, D), x_sample.reshape(geo.Ts, D)], axis=0)
    cond8 = jnp.zeros((8, D), f32).at[0].set(c_ctx).at[1:1 + Bs].set(c)
    mod_all = _ada_table(cond8, ada_w, ada_b)[:, :1 + Bs].reshape(depth, 1 + Bs, 1, 6 * D)

    st_ssm = jnp.zeros((Bc, ssm_in_w.shape[0], 2, G * 4, N, P), f32)
    st_gla = jnp.zeros((Bc, gla_in_w.shape[0], 2) + state_gla.shape[3:], f32)
    st_gdn = jnp.zeros((Bc, gdn_in_w.shape[0], 2) + state_gdn.shape[3:], f32)
    for i in range(depth):
        mod = mod_all[i]
        kind, j = i % 3, i // 3
        if kind == 0:
            w_in = ssm_in_w[j]
            di = ssm_out_w.shape[1]
            perm = _ssd_perm()
            proj, dtp = _in_proj(x, norm_mix_w[i], mod, 0, w_in[:, :di + 4096].astype(bf16),
                                 _pad_cols(w_in[:, di + 4096:]), geo, dst_blk=_ssd_dst_blocks())
            dtb, alog = _pad_row(ssm_dt_bias[j]), _pad_row(ssm_a_log[j])
            dskip = jnp.repeat(ssm_d[j], P).reshape(1, di)
            nw = ssm_norm_w[j].reshape(1, di)
            cw, cb = ssm_conv_w[j][:, perm], ssm_conv_b[j][perm].reshape(1, -1)
            h0 = state_ssm[:, j].reshape(Bs, 2, G, 4, N, P).transpose(1, 0, 2, 4, 3, 5).reshape(2, Bs, G, N, 4 * P)
            yf, pc, st_ssm = _ssd_scan(proj, None, dtp, cw, cb, dtb, alog, None, None, h0[0], None,
                                       st_ssm, j, geo, False)
            u, st_ssm = _ssd_scan(proj, pc, dtp, None, None, dtb, alog, dskip, nw, h0[1], yf,
                                  st_ssm, j, geo, True)
            w_out = ssm_out_w[j]
        elif kind == 1:
            w_small = _pad_cols(jnp.concatenate([gla_gate_w1[j, 0], gla_gate_w1[j, 1]], axis=1))
            proj, lrp = _in_proj(x, norm_mix_w[i], mod, 0, gla_in_w[j].astype(bf16), w_small, geo)
            w2 = gla_gate_w2[j]
            gb = gla_gate_b[j].reshape(2, 1, -1)
            nw = gla_norm_w[j].reshape(1, -1)
            yf, st_gla = _gla_scan(proj, lrp, w2, gb, None, state_gla[:, j, 0], None, st_gla, j, geo, False)
            u, st_gla = _gla_scan(proj, lrp, w2, gb, nw, state_gla[:, j, 1], yf, st_gla, j, geo, True)
            w_out = gla_out_w[j]
        else:
            w_in = gdn_in_w[j]
            perm = _gdn_perm(GDN_HP)
            proj, abp = _in_proj(x, norm_mix_w[i], mod, 0, w_in[:, :6144].astype(bf16),
                                 _pad_cols(w_in[:, 6144:]), geo, dst_blk=_gdn_dst_blocks(GDN_HP))
            dtb, alog = _pad_row(gdn_dt_bias[j]), _pad_row(gdn_a_log[j])
            nw = gdn_norm_w[j].reshape(1, -1)
            cw = gdn_conv_w[j][:, perm]
            yf, pc, st_gdn = _gdn_scan(proj, None, abp, cw, dtb, alog, None, state_gdn[:, j, 0], None,
                                       st_gdn, j, geo, False)
            u, st_gdn = _gdn_scan(proj, pc, abp, None, dtb, alog, nw, state_gdn[:, j, 1], yf,
                                  st_gdn, j, geo, True)
            w_out = gdn_out_w[j]
        x = _out_proj(u, w_out.astype(bf16), x, mod, 2, geo)

        proj, _ = _in_proj(x, norm_ffn_w[i], mod, 3, ffn_in_w[i].astype(bf16), None, geo)
        u = _ffn_conv(proj, ffn_conv_w[i].reshape(9, -1), ffn_conv_b[i], geo)
        x = _out_proj(u, ffn_out_w[i].astype(bf16), x, mod, 5, geo,
                      final_w=final_norm_w if i == depth - 1 else None)

    y_ctx, y_lat = x
    return (y_ctx.reshape(Bc, SEQ, D), y_lat.reshape(Bs, LS, D), st_ssm, st_gla, st_gdn)
```

```python
import functools
from typing import NamedTuple

import numpy as np
import jax
import jax.numpy as jnp
from jax import lax
from jax.experimental import pallas as pl
from jax.experimental.pallas import tpu as pltpu

f32, bf16 = jnp.float32, jnp.bfloat16

BLK = 256
CHUNK = 64
SSD_CHUNK = 128
HALO = 16
GRID_W = 64
NORM_EPS = 1e-6
SMALL = 128
GLA_RANK = 16
GLA_NORMALIZER = 16.0
SSM_GROUPS = 8
SSM_HEADDIM = 64
SSM_STATE = 128
SSD_GP = 8
GDN_HP = 8
VMEM_LIMIT = 48 * 1024 * 1024

NN = (((1,), (0,)), ((), ()))
NT = (((1,), (1,)), ((), ()))
TN = (((0,), (0,)), ((), ()))


class Geo(NamedTuple):
    D: int
    Tc: int
    Ts: int
    SEQ: int
    LS: int
    Bc: int
    Bs: int

    @property
    def T(self):
        return self.Tc + self.Ts

    @property
    def nb(self):
        return self.T // BLK

    @property
    def ncb(self):
        return self.Tc // BLK

    @property
    def bpc(self):
        return self.SEQ // BLK

    @property
    def bps(self):
        return self.LS // BLK


def _mm(a, b, dims=NN):
    return lax.dot_general(a, b, dims, preferred_element_type=f32)


def _split3(a):
    hi = a.astype(bf16)
    r1 = a - hi.astype(f32)
    mid = r1.astype(bf16)
    lo = (r1 - mid.astype(f32)).astype(bf16)
    return hi, mid, lo


def _mm_x(a, b, dims=NN):
    if isinstance(a, tuple):
        ps = [_mm(p, b, dims) for p in a]
    elif a.dtype == bf16:
        ps = [_mm(a, p, dims) for p in _split3(b)]
    else:
        ps = [_mm(p, b, dims) for p in _split3(a)]
    return ps[0] + ps[1] + ps[2]


def _mm_hi(a, b):
    ah = a.astype(bf16)
    al = (a - ah.astype(f32)).astype(bf16)
    bh = b.astype(bf16)
    bl = (b - bh.astype(f32)).astype(bf16)
    return _mm(ah, bh) + _mm(al, bh) + _mm(ah, bl)


def _silu(x):
    h = 0.5 * x
    return h + h * jnp.tanh(h)


def _tri_mask(n, upper, chunk=None, strict=False):
    r = lax.broadcasted_iota(jnp.int32, (n, n), 0)
    c = lax.broadcasted_iota(jnp.int32, (n, n), 1)
    if upper:
        m = (c > r) if strict else (c >= r)
    else:
        m = (c < r) if strict else (c <= r)
    if chunk is not None and chunk < n:
        sh = int(np.log2(chunk))
        same = lax.shift_right_logical(r, sh) == lax.shift_right_logical(c, sh)
        m = jnp.logical_and(m, same)
    return m


def _as_bf16(mask):
    return jnp.where(mask, 1.0, 0.0).astype(bf16)


def _ada_kernel(c_ref, w_ref, b_ref, o_ref):
    o_ref[0] = _mm_hi(_silu(c_ref[...]), w_ref[0]) + b_ref[0]


def _ada_table(cond8, ada_w, ada_b):
    depth, D, N6 = ada_w.shape
    tn = N6 // 4
    return pl.pallas_call(
        _ada_kernel,
        out_shape=jax.ShapeDtypeStruct((depth, 8, N6), f32),
        grid=(depth, N6 // tn),
        in_specs=[pl.BlockSpec((8, D), lambda l, j: (0, 0)),
                  pl.BlockSpec((1, D, tn), lambda l, j: (l, 0, j)),
                  pl.BlockSpec((1, 1, tn), lambda l, j: (l, 0, j))],
        out_specs=pl.BlockSpec((1, 8, tn), lambda l, j: (l, 0, j)),
        compiler_params=pltpu.CompilerParams(
            dimension_semantics=("parallel", "parallel"), vmem_limit_bytes=VMEM_LIMIT),
        name="ada_table",
    )(cond8, ada_w, ada_b.reshape(depth, 1, N6))


def _in_proj_kernel(*refs, has_small, tn, dst_blk):
    if has_small:
        x_ref, nw_ref, sh_ref, sc_ref, w_ref, ws_ref, o_ref, os_ref = refs
    else:
        x_ref, nw_ref, sh_ref, sc_ref, w_ref, o_ref = refs
    x = x_ref[...]
    h = x * lax.rsqrt(jnp.mean(x * x, axis=-1, keepdims=True) + NORM_EPS) * nw_ref[...]
    h = h * (1.0 + sc_ref[0]) + sh_ref[0]
    hb = h.astype(bf16)
    if has_small:
        ws = ws_ref[...]
        wh = ws.astype(bf16)
        wl = (ws - wh.astype(f32)).astype(bf16)
        hl = (h - hb.astype(f32)).astype(bf16)
        os_ref[...] = _mm(hb, wh) + _mm(hl, wh) + _mm(hb, wl)
    per = tn // 128
    for n in range(w_ref.shape[2] // tn):
        r = _mm(hb, w_ref[0, :, n * tn:(n + 1) * tn]).astype(o_ref.dtype)
        if dst_blk is None:
            o_ref[:, n * tn:(n + 1) * tn] = r
            continue
        k = 0
        while k < per:
            d0 = dst_blk[n * per + k]
            run = 1
            while k + run < per and dst_blk[n * per + k + run] == d0 + run:
                run += 1
            o_ref[:, d0 * 128:(d0 + run) * 128] = r[:, k * 128:(k + run) * 128]
            k += run


def _row_tile(geo, pref):
    for tm in (pref, 512, 256):
        if tm <= pref and geo.Tc % tm == 0 and geo.LS % tm == 0:
            return tm
    raise ValueError("token counts must be multiples of 256")


def _cond_of_tile(i, geo, tm):
    nct = geo.Tc // tm
    return jnp.where(i < nct, 0, 1 + jnp.maximum(i - nct, 0) // (geo.LS // tm))


def _in_proj(x, norm_w, mod, shift_idx, w_stack, layer, N, w_small, geo, tn=512, dst_blk=None):
    T, D = x.shape
    tm = _row_tile(geo, 512)
    assert N % tn == 0 and w_stack.dtype == bf16
    has_small = w_small is not None
    cond = functools.partial(_cond_of_tile, geo=geo, tm=tm)
    in_specs = [pl.BlockSpec((tm, D), lambda i: (i, 0)),
                pl.BlockSpec((1, D), lambda i: (0, 0)),
                pl.BlockSpec((1, 1, D), lambda i: (cond(i), 0, shift_idx)),
                pl.BlockSpec((1, 1, D), lambda i: (cond(i), 0, shift_idx + 1)),
                pl.BlockSpec((1, D, N), lambda i: (layer, 0, 0), pipeline_mode=pl.Buffered(1))]
    args = [x, norm_w.reshape(1, D), mod, mod, w_stack]
    out_shape = [jax.ShapeDtypeStruct((T, N), bf16)]
    out_specs = [pl.BlockSpec((tm, N), lambda i: (i, 0))]
    if has_small:
        in_specs.append(pl.BlockSpec((D, SMALL), lambda i: (0, 0)))
        args.append(w_small)
        out_shape.append(jax.ShapeDtypeStruct((T, SMALL), f32))
        out_specs.append(pl.BlockSpec((tm, SMALL), lambda i: (i, 0)))
    res = pl.pallas_call(
        functools.partial(_in_proj_kernel, has_small=has_small, tn=tn, dst_blk=dst_blk),
        out_shape=out_shape,
        grid=(T // tm,),
        in_specs=in_specs,
        out_specs=out_specs,
        compiler_params=pltpu.CompilerParams(
            dimension_semantics=("parallel",), vmem_limit_bytes=VMEM_LIMIT),
        name="in_proj",
    )(*args)
    return res if has_small else (res[0], None)


def _out_proj_kernel(*refs, n_ctx_tiles):
    final = n_ctx_tiles is not None
    if final:
        u_ref, w_ref, x_ref, g_ref, fw_ref, oc_ref, os_ref = refs
    else:
        u_ref, w_ref, x_ref, g_ref, o_ref = refs
    r = x_ref[...] + g_ref[0] * _mm(u_ref[...], w_ref[0])
    if not final:
        o_ref[...] = r
        return
    r = r * lax.rsqrt(jnp.mean(r * r, axis=-1, keepdims=True) + NORM_EPS) * fw_ref[...]

    @pl.when(pl.program_id(0) < n_ctx_tiles)
    def _():
        oc_ref[...] = r

    @pl.when(pl.program_id(0) >= n_ctx_tiles)
    def _():
        os_ref[...] = r


def _out_proj(u, w_stack, layer, x, mod, gate_idx, geo, final_w=None):
    T, K = u.shape
    D = w_stack.shape[2]
    tm = _row_tile(geo, 512)
    cond = functools.partial(_cond_of_tile, geo=geo, tm=tm)
    final = final_w is not None
    nct = geo.Tc // tm
    in_specs = [pl.BlockSpec((tm, K), lambda i: (i, 0)),
                pl.BlockSpec((1, K, D), lambda i: (layer, 0, 0)),
                pl.BlockSpec((tm, D), lambda i: (i, 0)),
                pl.BlockSpec((1, 1, D), lambda i: (cond(i), 0, gate_idx))]
    args = [u, w_stack, x, mod]
    if final:
        in_specs.append(pl.BlockSpec((1, D), lambda i: (0, 0)))
        args.append(final_w.reshape(1, D))
        out_shape = [jax.ShapeDtypeStruct((geo.Tc, D), f32), jax.ShapeDtypeStruct((geo.Ts, D), f32)]
        out_specs = [pl.BlockSpec((tm, D), lambda i: (jnp.minimum(i, nct - 1), 0)),
                     pl.BlockSpec((tm, D), lambda i: (jnp.maximum(i - nct, 0), 0))]
    else:
        out_shape = jax.ShapeDtypeStruct((T, D), f32)
        out_specs = pl.BlockSpec((tm, D), lambda i: (i, 0))
    return pl.pallas_call(
        functools.partial(_out_proj_kernel, n_ctx_tiles=nct if final else None),
        out_shape=out_shape,
        grid=(T // tm,),
        in_specs=in_specs,
        out_specs=out_specs,
        compiler_params=pltpu.CompilerParams(
            dimension_semantics=("arbitrary" if final else "parallel",), vmem_limit_bytes=VMEM_LIMIT),
        name="out_proj",
    )(*args)


def _ffn_conv_kernel(a_ref, v_ref, w_ref, b_ref, o_ref, pad_ref, *, n_ctx_img, IB, SEQ):
    PADR = 128
    tf = a_ref.shape[1]
    pad_ref[0:PADR, :] = jnp.zeros((PADR, tf), f32)
    pad_ref[PADR + IB:PADR + IB + PADR, :] = jnp.zeros((PADR, tf), f32)
    pad_ref[PADR:PADR + IB, :] = a_ref[...].astype(f32)
    w = w_ref[...]
    b = b_ref[...]
    row = lax.broadcasted_iota(jnp.int32, (BLK, tf), 0)

    def taps(s, pos, width, kr):
        left = jnp.where(pos > 0, pad_ref[s - 1:s - 1 + BLK, :], 0.0)
        right = jnp.where(pos < width - 1, pad_ref[s + 1:s + 1 + BLK, :], 0.0)
        return (left * w[3 * kr:3 * kr + 1] + pad_ref[s:s + BLK, :] * w[3 * kr + 1:3 * kr + 2]
                + right * w[3 * kr + 2:3 * kr + 3])

    def finish(c, acc):
        v = v_ref[c * BLK:(c + 1) * BLK, :].astype(f32)
        o_ref[c * BLK:(c + 1) * BLK, :] = (_silu(acc + b) * v).astype(bf16)

    @pl.when(pl.program_id(0) < n_ctx_img)
    def _():
        for c in range(IB // BLK):
            pos = jnp.bitwise_and(row + (c * BLK) % SEQ, SEQ - 1)
            finish(c, taps(PADR + c * BLK, pos, SEQ, 1))

    @pl.when(pl.program_id(0) >= n_ctx_img)
    def _():
        col = jnp.bitwise_and(row, GRID_W - 1)
        for c in range(IB // BLK):
            s = PADR + c * BLK
            acc = taps(s - GRID_W, col, GRID_W, 0) + taps(s, col, GRID_W, 1) + taps(s + GRID_W, col, GRID_W, 2)
            finish(c, acc)


def _ffn_conv(proj, conv_w9, conv_b, geo, tf=256):
    T, F2 = proj.shape
    F = F2 // 2
    IB = geo.LS
    assert F % tf == 0 and geo.Tc % IB == 0 and IB % BLK == 0
    assert geo.SEQ & (geo.SEQ - 1) == 0 and BLK % GRID_W == 0
    nf = F // tf
    return pl.pallas_call(
        functools.partial(_ffn_conv_kernel, n_ctx_img=geo.Tc // IB, IB=IB, SEQ=geo.SEQ),
        out_shape=jax.ShapeDtypeStruct((T, F), bf16),
        grid=(T // IB, nf),
        in_specs=[pl.BlockSpec((IB, tf), lambda i, j: (i, j)),
                  pl.BlockSpec((IB, tf), lambda i, j: (i, nf + j)),
                  pl.BlockSpec((9, tf), lambda i, j: (0, j)),
                  pl.BlockSpec((1, tf), lambda i, j: (0, j))],
        out_specs=pl.BlockSpec((IB, tf), lambda i, j: (i, j)),
        scratch_shapes=[pltpu.VMEM((IB + 256, tf), f32)],
        compiler_params=pltpu.CompilerParams(
            dimension_semantics=("parallel", "parallel"), vmem_limit_bytes=VMEM_LIMIT),
        name="ffn_conv",
    )(proj, proj, conv_w9, conv_b.reshape(1, F))


def _blk_idx(i, geo, rev):
    return (geo.nb - 1 - i) if rev else i


def _tok_spec(cw, colfn, geo, rev):
    return pl.BlockSpec((BLK, cw), lambda u, i: (_blk_idx(i, geo, rev), colfn(u)))


def _halo_specs(cw, colfn, geo, rev):
    per = BLK // HALO
    last = geo.T // HALO - 1
    prev = pl.BlockSpec((HALO, cw), lambda u, i: (jnp.maximum(_blk_idx(i, geo, rev) * per - 1, 0), colfn(u)))
    nxt = pl.BlockSpec((HALO, cw), lambda u, i: (jnp.minimum((_blk_idx(i, geo, rev) + 1) * per, last), colfn(u)))
    return prev, nxt


def _h0_spec(geo, rev, hp):
    def idx(u, i):
        ip = _blk_idx(i, geo, rev)
        return (jnp.maximum(ip - geo.ncb, 0) // geo.bps, u, 0, 0)
    return pl.BlockSpec((1, hp, 128, 256), idx)


def _st_spec(geo, rev, layer, blk):
    d = 1 if rev else 0

    def idx(u, i):
        ip = _blk_idx(i, geo, rev)
        return (jnp.minimum(ip // geo.bpc, geo.Bc - 1), layer, d, u, 0, 0)
    return pl.BlockSpec((1, 1, 1) + blk, idx)


def _param_spec(rows, cw, colfn):
    return pl.BlockSpec((rows, cw), lambda u, i: (0, colfn(u)))


def _flags(geo, rev):
    ip = _blk_idx(pl.program_id(1), geo, rev)
    is_ctx = ip < geo.ncb
    pos = jnp.where(is_ctx, lax.rem(ip, geo.bpc), lax.rem(jnp.maximum(ip - geo.ncb, 0), geo.bps))
    n = jnp.where(is_ctx, geo.bpc, geo.bps)
    first = pos == 0
    last = pos == n - 1
    return is_ctx, first, last, (last if rev else first)


def _conv3_silu(cur_ref, prev_ref, next_ref, w_ref, b_ref, first, last):
    cur = cur_ref[...].astype(f32)
    n = cur.shape[0]
    row = lax.broadcasted_iota(jnp.int32, cur.shape, 0)
    pr = jnp.where(first, 0.0, prev_ref[HALO - 1:HALO, :].astype(f32))
    nx = jnp.where(last, 0.0, next_ref[0:1, :].astype(f32))
    xm = jnp.where(row == 0, pr, pltpu.roll(cur, 1, axis=0))
    xp = jnp.where(row == n - 1, nx, pltpu.roll(cur, n - 1, axis=0))
    w = w_ref[...]
    y = xm * w[0:1] + cur * w[1:2] + xp * w[2:3]
    if b_ref is not None:
        y = y + b_ref[...]
    return _silu(y)


def _init_state(S, h0_ref, is_ctx, start):
    @pl.when(start)
    def _():
        S[...] = jnp.where(is_ctx, 0.0, h0_ref[0])


def _norm_gate(o, gate, nw, gate_first):
    g = _silu(gate)
    if gate_first:
        o = o * g
    o = o * lax.rsqrt(jnp.mean(o * o, axis=-1, keepdims=True) + NORM_EPS) * nw
    if not gate_first:
        o = o * g
    return o


def _scan_call(kernel, name, grid_u, in_specs, args, outs, st, st_blk, layer, geo, rev, hp):
    out_shape = [jax.ShapeDtypeStruct((geo.T, c), dt) for c, dt, _ in outs]
    out_specs = [_tok_spec(bc, lambda u: u, geo, rev) for _, _, bc in outs]
    fresh = isinstance(st, jax.ShapeDtypeStruct)
    extra_specs, extra_args, alias = [], [], {}
    if not fresh:
        extra_specs, extra_args, alias = [pl.BlockSpec(memory_space=pl.ANY)], [st], {len(args): len(outs)}
    return pl.pallas_call(
        kernel,
        out_shape=out_shape + [jax.ShapeDtypeStruct(st.shape, st.dtype)],
        grid=(grid_u, geo.nb),
        in_specs=in_specs + extra_specs,
        out_specs=out_specs + [_st_spec(geo, rev, layer, st_blk)],
        scratch_shapes=[pltpu.VMEM((hp, 128, 256), f32)],
        input_output_aliases=alias,
        compiler_params=pltpu.CompilerParams(
            dimension_semantics=("parallel", "arbitrary"), vmem_limit_bytes=VMEM_LIMIT),
        name=name,
    )(*args, *extra_args)


def _ssd_kernel(*refs, rev, geo, gp, aliased):
    skip = 1 if aliased else 0
    if rev:
        (pc_ref, dt_ref, dtb_ref, alog_ref, xsel_ref, xh_ref, h0_ref,
         dsk_ref, z_ref, nw_ref, yf_ref) = refs[:11]
        y_ref, st_ref, S = refs[11 + skip:]
    else:
        (xbc_c, xbc_p, xbc_n, dt_ref, cw_ref, cb_ref, dtb_ref, alog_ref, xsel_ref, xh_ref, h0_ref) = refs[:11]
        y_ref, pc_ref, st_ref, S = refs[11 + skip:]
    is_ctx, first, last, start = _flags(geo, rev)
    _init_state(S, h0_ref, is_ctx, start)
    Q = SSD_CHUNK
    nch = BLK // Q
    groups = range(gp)

    if rev:
        xbc = pc_ref[...].astype(f32)
    else:
        xbc = _conv3_silu(xbc_c, xbc_p, xbc_n, cw_ref, cb_ref, first, last)
        pc_ref[...] = xbc.astype(bf16)
    dt = jax.nn.softplus(dt_ref[...] + dtb_ref[...])
    la = dt * (-jnp.exp(alog_ref[...]))
    tri = _tri_mask(BLK, rev, chunk=Q)
    cs3 = _split3(_mm_x(_as_bf16(tri), la))
    dtb16 = dt.astype(bf16)
    tri_q = _tri_mask(Q, rev)

    xs, xvs, bms, cms, cses, csts = [], [], [], [], [], []
    for g in groups:
        x = xbc[:, g * 512:g * 512 + 256]
        xs.append(x)
        bms.append(xbc[:, g * 512 + 256:g * 512 + 384].astype(bf16))
        cms.append(xbc[:, g * 512 + 384:g * 512 + 512].astype(bf16))
        xsel = xsel_ref[g]
        xvs.append(x * _mm(dtb16, xsel))
        cses.append(_mm_x(cs3, xsel))
        csts.append(_mm_x(cs3, xh_ref[g]).T)
    Svs = [S[g] for g in groups]
    youts = [[None] * nch for _ in groups]
    for c in (range(nch - 1, -1, -1) if rev else range(nch)):
        lo = c * Q
        li = lo if rev else lo + Q - 1
        for g in groups:
            cs_e = cses[g][lo:lo + Q]
            xv = xvs[g][lo:lo + Q]
            xvb = xv.astype(bf16)
            cmb = cms[g][lo:lo + Q]
            bmb = bms[g][lo:lo + Q]
            gm = _mm(cmb, bmb, NT)
            ys = []
            for r in range(4):
                seg = cs_e[:, r * 64:r * 64 + 1] - csts[g][r:r + 1, lo:lo + Q]
                m = jnp.where(tri_q, gm * jnp.exp(seg), 0.0)
                ys.append(_mm(m.astype(bf16), xvb[:, r * 64:(r + 1) * 64]))
            Sv = Svs[g]
            y = jnp.concatenate(ys, axis=1) + _mm(cmb, Sv.astype(bf16)) * jnp.exp(cs_e)
            cl = cses[g][li:li + 1, :]
            xw = xv * jnp.exp(cl - cs_e)
            Svs[g] = Sv * jnp.exp(cl) + _mm(bmb, xw.astype(bf16), TN)
            youts[g][c] = y

    for g in groups:
        S[g] = Svs[g]
        y = jnp.concatenate(youts[g], axis=0)
        if rev:
            cols = slice(g * 256, (g + 1) * 256)
            o = yf_ref[:, cols] + y + xs[g] * dsk_ref[:, cols]
            y = _norm_gate(o, z_ref[:, cols].astype(f32), nw_ref[:, cols], gate_first=True)
        y_ref[:, g * 256:(g + 1) * 256] = y.astype(y_ref.dtype)

    @pl.when(is_ctx)
    def _():
        for g in groups:
            for r in range(4):
                st_ref[0, 0, 0, g * 4 + r] = Svs[g][:, r * 64:(r + 1) * 64]


def _ssd_scan(proj, pc, dtp, conv_w, conv_b, dtb, alog, dskip, norm_w, h0, yf, st, layer, geo, rev):
    G, gp = SSM_GROUPS, SSD_GP
    nu = G // gp
    d = 1 if rev else 0
    xsel = np.zeros((G, 128, 256), np.float32)
    xh = np.zeros((G, 128, 128), np.float32)
    for g in range(G):
        for r in range(4):
            xsel[g, d * 32 + 4 * g + r, r * 64:(r + 1) * 64] = 1.0
            xh[g, d * 32 + 4 * g + r, r] = 1.0
    cw, zw = 512 * gp, 256 * gp
    head_specs = [_param_spec(1, SMALL, lambda u: 0), _param_spec(1, SMALL, lambda u: 0),
                  pl.BlockSpec((gp, 128, 256), lambda u, i: (u, 0, 0)),
                  pl.BlockSpec((gp, 128, 128), lambda u, i: (u, 0, 0)),
                  _h0_spec(geo, rev, gp)]
    head_args = [dtb, alog, jnp.asarray(xsel, bf16), jnp.asarray(xh, bf16), h0]
    if rev:
        in_specs = [_tok_spec(cw, lambda u: u, geo, rev), _tok_spec(SMALL, lambda u: 0, geo, rev), *head_specs,
                    _param_spec(1, zw, lambda u: u), _tok_spec(zw, lambda u: 2 * nu + u, geo, rev),
                    _param_spec(1, zw, lambda u: u), _tok_spec(zw, lambda u: u, geo, rev)]
        args = [pc, dtp, *head_args, dskip, proj, norm_w, yf]
        outs = [(2048, bf16, zw)]
    else:
        in_specs = [_tok_spec(cw, lambda u: u, geo, rev), *_halo_specs(cw, lambda u: u, geo, rev),
                    _tok_spec(SMALL, lambda u: 0, geo, rev),
                    _param_spec(3, cw, lambda u: u), _param_spec(1, cw, lambda u: u), *head_specs]
        args = [proj, proj, proj, dtp, conv_w, conv_b, *head_args]
        outs = [(2048, f32, zw), (4096, bf16, cw)]
    kern = functools.partial(_ssd_kernel, rev=rev, geo=geo, gp=gp, aliased=not isinstance(st, jax.ShapeDtypeStruct))
    return _scan_call(kern, "ssd_bwd" if rev else "ssd_fwd",
                      nu, in_specs, args, outs, st, (4 * gp, SSM_STATE, SSM_HEADDIM), layer, geo, rev, gp)


def _gla_kernel(*refs, rev, geo, nh, aliased):
    qkv_ref, lr_ref, w2_ref, gb_ref, h0_ref = refs[:5]
    skip = 1 if aliased else 0
    if rev:
        r_ref, nw_ref, yf_ref = refs[5:8]
        y_ref, st_ref, S = refs[8 + skip:]
    else:
        y_ref, st_ref, S = refs[5 + skip:]
    is_ctx, first, last, start = _flags(geo, rev)
    _init_state(S, h0_ref, is_ctx, start)
    d = 1 if rev else 0
    dk, dv = 128, 256
    heads = range(nh)
    K = nh * dk

    q_all = qkv_ref[:, 0:K].astype(f32) * dk ** -0.5
    k_all = qkv_ref[:, K:2 * K].astype(f32)
    vb_all = qkv_ref[:, 2 * K:2 * K + nh * dv]
    lr = lr_ref[...][:, d * GLA_RANK:(d + 1) * GLA_RANK]
    logit = _mm_hi(lr, w2_ref[0]) + gb_ref[0]
    lg = jax.nn.log_sigmoid(logit) * (1.0 / GLA_NORMALIZER)
    tri = _tri_mask(BLK, rev, chunk=CHUNK)
    cs = _mm_x(_as_bf16(tri), lg)
    cs_t = _mm_x(lg, _as_bf16(_tri_mask(BLK, not rev, chunk=CHUNK)), TN)
    qgb = (q_all * jnp.exp(cs)).astype(bf16)
    kgb = (k_all * jnp.exp(-cs)).astype(bf16)

    ys = []
    for h in heads:
        sc = jnp.where(tri, _mm(qgb[:, h * dk:(h + 1) * dk], kgb[:, h * dk:(h + 1) * dk], NT), 0.0)
        ys.append(_mm(sc.astype(bf16), vb_all[:, h * dv:(h + 1) * dv]))
    Svs = [S[h] for h in heads]
    nch = BLK // CHUNK
    yin = [[None] * nch for _ in heads]
    for c in (range(nch - 1, -1, -1) if rev else range(nch)):
        lo = c * CHUNK
        li = lo if rev else lo + CHUNK - 1
        kw_all = (k_all[lo:lo + CHUNK] * jnp.exp(cs[li:li + 1, :] - cs[lo:lo + CHUNK])).astype(bf16)
        for h in heads:
            yin[h][c] = _mm(qgb[lo:lo + CHUNK, h * dk:(h + 1) * dk], Svs[h].astype(bf16))
        for h in heads:
            Svs[h] = (Svs[h] * jnp.exp(cs_t[h * dk:(h + 1) * dk, li:li + 1])
                      + _mm(kw_all[:, h * dk:(h + 1) * dk], vb_all[lo:lo + CHUNK, h * dv:(h + 1) * dv], TN))

    for h in heads:
        S[h] = Svs[h]
        y = ys[h] + jnp.concatenate(yin[h], axis=0)
        cols = slice(h * dv, (h + 1) * dv)
        if rev:
            y = _norm_gate(yf_ref[:, cols] + y, r_ref[:, cols].astype(f32), nw_ref[...], gate_first=False)
        y_ref[:, cols] = y.astype(y_ref.dtype)

    @pl.when(is_ctx)
    def _():
        for h in heads:
            st_ref[0, 0, 0, h] = Svs[h]


def _gla_scan(proj, lrp, w2, gb, norm_w, h0, yf, st, layer, geo, rev):
    H = 4
    d = 1 if rev else 0
    in_specs = [_tok_spec(2048, lambda u: 0, geo, rev), _tok_spec(SMALL, lambda u: 0, geo, rev),
                pl.BlockSpec((1, GLA_RANK, 512), lambda u, i: (d, 0, 0)),
                pl.BlockSpec((1, 1, 512), lambda u, i: (d, 0, 0)),
                _h0_spec(geo, rev, H)]
    args = [proj, lrp, w2, gb, h0]
    if rev:
        in_specs += [_tok_spec(1024, lambda u: 2, geo, rev), _param_spec(1, 256, lambda u: 0),
                     _tok_spec(1024, lambda u: 0, geo, rev)]
        args += [proj, norm_w, yf]
    kern = functools.partial(_gla_kernel, rev=rev, geo=geo, nh=H, aliased=not isinstance(st, jax.ShapeDtypeStruct))
    return _scan_call(kern, "gla_bwd" if rev else "gla_fwd",
                      1, in_specs, args, [(1024, bf16 if rev else f32, 1024)], st, (H, 128, 256), layer, geo, rev, H)


def _unit_tri_inverse(ms, lev):
    eye = jnp.where(lev == -1, 1.0, 0.0)
    dinv = [eye - jnp.where(lev == 0, m, 0.0) for m in ms]
    for b in range(1, 6):
        ts = [_mm(jnp.where(lev == b, m, 0.0).astype(bf16), d.astype(bf16)).astype(bf16)
              for m, d in zip(ms, dinv)]
        dinv = [d - _mm(d.astype(bf16), t) for d, t in zip(dinv, ts)]
    return dinv


def _gdn_kernel(*refs, rev, geo, hp, aliased):
    skip = 1 if aliased else 0
    if rev:
        (pc_ref, ab_ref, dtb_ref, alog_ref, xab_ref, lev_ref, h0_ref, z_ref, nw_ref, yf_ref) = refs[:10]
        y_ref, st_ref, S = refs[10 + skip:]
    else:
        (qkv_c, qkv_p, qkv_n, cw_ref, ab_ref, dtb_ref, alog_ref, xab_ref, lev_ref, h0_ref) = refs[:10]
        y_ref, pc_ref, st_ref, S = refs[10 + skip:]
    is_ctx, first, last, start = _flags(geo, rev)
    _init_state(S, h0_ref, is_ctx, start)
    dk, dv = 128, 256
    heads = range(hp)

    if rev:
        qkv = pc_ref[...].astype(f32)
    else:
        qkv = _conv3_silu(qkv_c, qkv_p, qkv_n, cw_ref, None, first, last)

    ab = ab_ref[...]
    gall = -jnp.exp(alog_ref[...]) * jax.nn.softplus(ab + dtb_ref[...])
    lane = lax.broadcasted_iota(jnp.int32, ab.shape, 1)
    sel = _mm_x(jnp.where(lane < 16, gall, jax.nn.sigmoid(ab)), xab_ref[0])
    lev = lev_ref[...]
    r_i = lax.broadcasted_iota(jnp.int32, (BLK, BLK), 0)
    c_i = lax.broadcasted_iota(jnp.int32, (BLK, BLK), 1)
    in_chunk = lev < 6
    incl = jnp.logical_and(in_chunk, (c_i >= r_i) if rev else (c_i <= r_i))
    incl_t = jnp.logical_and(in_chunk, (c_i <= r_i) if rev else (c_i >= r_i))
    strict = jnp.logical_and(incl, lev >= 0)
    cs_all = _mm_x(_as_bf16(incl), sel)
    cs_rows = _mm_x(sel, _as_bf16(incl_t), TN)

    qs, ks, kbfs, kbs, vbs, css, lmasks, ms = [], [], [], [], [], [], [], []
    for hh in heads:
        q = qkv[:, hh * dk:(hh + 1) * dk]
        k = qkv[:, (hp + hh) * dk:(hp + hh + 1) * dk]
        v = qkv[:, 2 * hp * dk + hh * dv:2 * hp * dk + (hh + 1) * dv]
        if not rev:
            q = q * lax.rsqrt(jnp.sum(q * q, axis=-1, keepdims=True) + 1e-6) * dk ** -0.5
            k = k * lax.rsqrt(jnp.sum(k * k, axis=-1, keepdims=True) + 1e-6)
            pc_ref[:, hh * dk:(hh + 1) * dk] = q.astype(bf16)
            pc_ref[:, (hp + hh) * dk:(hp + hh + 1) * dk] = k.astype(bf16)
            pc_ref[:, 2 * hp * dk + hh * dv:2 * hp * dk + (hh + 1) * dv] = v.astype(bf16)
        beta = sel[:, 2 * hh + 1:2 * hh + 2]
        cs = cs_all[:, 2 * hh:2 * hh + 1]
        lmask = jnp.where(incl, jnp.exp(cs - cs_rows[2 * hh:2 * hh + 1, :]), 0.0)
        kb = k * beta
        kbf = k.astype(bf16)
        qs.append(q)
        ks.append(k)
        kbfs.append(kbf)
        kbs.append(kb)
        vbs.append((v * beta).astype(bf16))
        css.append(cs)
        lmasks.append(lmask)
        ms.append(jnp.where(strict, _mm(kb.astype(bf16), kbf, NT) * lmask, 0.0))
    tbs = [t.astype(bf16) for t in _unit_tri_inverse(ms, lev)]
    ecs = [jnp.exp(cs) for cs in css]
    us = [_mm(tbs[h], vbs[h]) for h in heads]
    wks = [_mm(tbs[h], (kbs[h] * ecs[h]).astype(bf16)).astype(bf16) for h in heads]
    a_ins = [(_mm(qs[h].astype(bf16), kbfs[h], NT) * lmasks[h]).astype(bf16) for h in heads]
    qds = [(qs[h] * ecs[h]).astype(bf16) for h in heads]

    Svs = [S[h] for h in heads]
    nch = BLK // CHUNK
    outs = [[None] * nch for _ in heads]
    for c in (range(nch - 1, -1, -1) if rev else range(nch)):
        lo = c * CHUNK
        li = lo if rev else lo + CHUNK - 1
        Sbs = [Sv.astype(bf16) for Sv in Svs]
        v_news = [(us[h][lo:lo + CHUNK] - _mm(wks[h][lo:lo + CHUNK], Sbs[h])).astype(bf16) for h in heads]
        for h in heads:
            outs[h][c] = (_mm(qds[h][lo:lo + CHUNK], Sbs[h])
                          + _mm(a_ins[h][lo:lo + CHUNK, lo:lo + CHUNK], v_news[h]))
        for h in heads:
            cl = css[h][li:li + 1, :]
            kd = ks[h][lo:lo + CHUNK] * jnp.exp(cl - css[h][lo:lo + CHUNK])
            Svs[h] = Svs[h] * jnp.exp(cl) + _mm(kd.astype(bf16), v_news[h], TN)

    for hh in heads:
        S[hh] = Svs[hh]
        o = jnp.concatenate(outs[hh], axis=0)
        cols = slice(hh * dv, (hh + 1) * dv)
        if rev:
            o = _norm_gate(yf_ref[:, cols] + o, z_ref[:, cols].astype(f32), nw_ref[...], gate_first=False)
        y_ref[:, cols] = o.astype(y_ref.dtype)

    @pl.when(is_ctx)
    def _():
        for hh in heads:
            st_ref[0, 0, 0, hh] = Svs[hh]


def _gdn_scan(proj, pc, abp, conv_w, dtb, alog, norm_w, h0, yf, st, layer, geo, rev):
    H, hp = 8, GDN_HP
    nu = H // hp
    d = 1 if rev else 0
    xab = np.zeros((nu, 128, 128), np.float32)
    for h in range(H):
        xab[h // hp, d * H + h, 2 * (h % hp)] = 1.0
        xab[h // hp, 2 * H + d * H + h, 2 * (h % hp) + 1] = 1.0
    idx = np.arange(BLK)
    x = idx[:, None] ^ idx[None, :]
    lev = np.where(x == 0, -1, np.floor(np.log2(np.maximum(x, 1)))).astype(np.int32)
    cw, vw = 512 * hp, 256 * hp
    head_specs = [_tok_spec(SMALL, lambda u: 0, geo, rev),
                  _param_spec(1, SMALL, lambda u: 0), _param_spec(1, SMALL, lambda u: 0),
                  pl.BlockSpec((1, 128, 128), lambda u, i: (u, 0, 0)),
                  pl.BlockSpec((BLK, BLK), lambda u, i: (0, 0)),
                  _h0_spec(geo, rev, hp)]
    head_args = [abp, dtb, alog, jnp.asarray(xab, bf16), jnp.asarray(lev), h0]
    if rev:
        in_specs = [_tok_spec(cw, lambda u: u, geo, rev), *head_specs,
                    _tok_spec(vw, lambda u: 2 * nu + u, geo, rev), _param_spec(1, 256, lambda u: 0),
                    _tok_spec(vw, lambda u: u, geo, rev)]
        args = [pc, *head_args, proj, norm_w, yf]
        outs = [(2048, bf16, vw)]
    else:
        in_specs = [_tok_spec(cw, lambda u: u, geo, rev), *_halo_specs(cw, lambda u: u, geo, rev),
                    _param_spec(3, cw, lambda u: u), *head_specs]
        args = [proj, proj, proj, conv_w, *head_args]
        outs = [(2048, f32, vw), (4096, bf16, cw)]
    kern = functools.partial(_gdn_kernel, rev=rev, geo=geo, hp=hp, aliased=not isinstance(st, jax.ShapeDtypeStruct))
    return _scan_call(kern, "gdn_bwd" if rev else "gdn_fwd",
                      nu, in_specs, args, outs, st, (hp, 128, 256), layer, geo, rev, hp)


def _pad_cols(w, n=SMALL):
    return jnp.pad(w.astype(f32), ((0, 0), (0, n - w.shape[1])))


def _pad_row(v, n=SMALL):
    v = v.reshape(1, -1).astype(f32)
    return jnp.pad(v, ((0, 0), (0, n - v.shape[1])))


def _ssd_perm():
    perm = []
    for g in range(SSM_GROUPS):
        perm += list(range(g * 256, (g + 1) * 256))
        perm += list(range(2048 + g * 128, 2048 + (g + 1) * 128))
        perm += list(range(3072 + g * 128, 3072 + (g + 1) * 128))
    return np.asarray(perm, np.int32)


def _ssd_dst_blocks():
    dst = [32 + i for i in range(16)]
    dst += [4 * (i // 2) + i % 2 for i in range(16)]
    dst += [4 * g + 2 for g in range(SSM_GROUPS)]
    dst += [4 * g + 3 for g in range(SSM_GROUPS)]
    return tuple(dst)


def _gdn_dst_blocks(hp):
    inv = np.argsort(_gdn_perm(hp)[::128] // 128)
    return tuple(int(b) for b in inv) + tuple(range(32, 48))


def _gdn_perm(hp):
    perm = []
    for u in range(8 // hp):
        perm += list(range(u * hp * 128, (u + 1) * hp * 128))
        perm += list(range(1024 + u * hp * 128, 1024 + (u + 1) * hp * 128))
        perm += list(range(2048 + u * hp * 256, 2048 + (u + 1) * hp * 256))
    return np.asarray(perm, np.int32)


def kernel(x_prompt, x_sample, state_ssm, state_gla, state_gdn, c, c_ctx, norm_mix_w, norm_ffn_w, ada_w, ada_b, ffn_in_w, ffn_conv_w, ffn_conv_b, ffn_out_w, ssm_in_w, ssm_conv_w, ssm_conv_b, ssm_dt_bias, ssm_a_log, ssm_d, ssm_norm_w, ssm_out_w, gla_in_w, gla_gate_w1, gla_gate_w2, gla_gate_b, gla_norm_w, gla_out_w, gdn_in_w, gdn_conv_w, gdn_dt_bias, gdn_a_log, gdn_norm_w, gdn_out_w, final_norm_w):
    Bc, SEQ, D = x_prompt.shape
    Bs, LS, _ = x_sample.shape
    geo = Geo(D=D, Tc=Bc * SEQ, Ts=Bs * LS, SEQ=SEQ, LS=LS, Bc=Bc, Bs=Bs)
    assert SEQ % BLK == 0 and LS % BLK == 0 and Bs < 8
    depth = ada_w.shape[0]
    G, P, N = SSM_GROUPS, SSM_HEADDIM, SSM_STATE

    x = jnp.concatenate([x_prompt.reshape(geo.Tc, D), x_sample.reshape(geo.Ts, D)], axis=0)
    cond8 = jnp.zeros((8, D), f32).at[0].set(c_ctx).at[1:1 + Bs].set(c)
    mod_all = _ada_table(cond8, ada_w, ada_b)[:, :1 + Bs].reshape(depth, 1 + Bs, 1, 6 * D)

    wb = dict(ssm_in=ssm_in_w.astype(bf16), ssm_out=ssm_out_w.astype(bf16), gla_in=gla_in_w.astype(bf16),
              gla_out=gla_out_w.astype(bf16), gdn_in=gdn_in_w.astype(bf16), gdn_out=gdn_out_w.astype(bf16),
              ffn_in=ffn_in_w.astype(bf16), ffn_out=ffn_out_w.astype(bf16))
    st_ssm = jax.ShapeDtypeStruct((Bc, ssm_in_w.shape[0], 2, G * 4, N, P), f32)
    st_gla = jax.ShapeDtypeStruct((Bc, gla_in_w.shape[0], 2) + state_gla.shape[3:], f32)
    st_gdn = jax.ShapeDtypeStruct((Bc, gdn_in_w.shape[0], 2) + state_gdn.shape[3:], f32)
    for i in range(depth):
        mod = mod_all[i]
        kind, j = i % 3, i // 3
        if kind == 0:
            w_in = ssm_in_w[j]
            di = ssm_out_w.shape[1]
            perm = _ssd_perm()
            proj, dtp = _in_proj(x, norm_mix_w[i], mod, 0, wb['ssm_in'], j, di + 4096,
                                 _pad_cols(w_in[:, di + 4096:]), geo, dst_blk=_ssd_dst_blocks())
            dtb, alog = _pad_row(ssm_dt_bias[j]), _pad_row(ssm_a_log[j])
            dskip = jnp.repeat(ssm_d[j], P).reshape(1, di)
            nw = ssm_norm_w[j].reshape(1, di)
            cw, cb = ssm_conv_w[j][:, perm], ssm_conv_b[j][perm].reshape(1, -1)
            h0 = state_ssm[:, j].reshape(Bs, 2, G, 4, N, P).transpose(1, 0, 2, 4, 3, 5).reshape(2, Bs, G, N, 4 * P)
            yf, pc, st_ssm = _ssd_scan(proj, None, dtp, cw, cb, dtb, alog, None, None, h0[0], None,
                                       st_ssm, j, geo, False)
            u, st_ssm = _ssd_scan(proj, pc, dtp, None, None, dtb, alog, dskip, nw, h0[1], yf,
                                  st_ssm, j, geo, True)
            w_out = wb['ssm_out']
        elif kind == 1:
            w_small = _pad_cols(jnp.concatenate([gla_gate_w1[j, 0], gla_gate_w1[j, 1]], axis=1))
            proj, lrp = _in_proj(x, norm_mix_w[i], mod, 0, wb['gla_in'], j, gla_in_w.shape[2], w_small, geo)
            w2 = gla_gate_w2[j]
            gb = gla_gate_b[j].reshape(2, 1, -1)
            nw = gla_norm_w[j].reshape(1, -1)
            yf, st_gla = _gla_scan(proj, lrp, w2, gb, None, state_gla[:, j, 0], None, st_gla, j, geo, False)
            u, st_gla = _gla_scan(proj, lrp, w2, gb, nw, state_gla[:, j, 1], yf, st_gla, j, geo, True)
            w_out = wb['gla_out']
        else:
            w_in = gdn_in_w[j]
            perm = _gdn_perm(GDN_HP)
            proj, abp = _in_proj(x, norm_mix_w[i], mod, 0, wb['gdn_in'], j, 6144,
                                 _pad_cols(w_in[:, 6144:]), geo, dst_blk=_gdn_dst_blocks(GDN_HP))
            dtb, alog = _pad_row(gdn_dt_bias[j]), _pad_row(gdn_a_log[j])
            nw = gdn_norm_w[j].reshape(1, -1)
            cw = gdn_conv_w[j][:, perm]
            yf, pc, st_gdn = _gdn_scan(proj, None, abp, cw, dtb, alog, None, state_gdn[:, j, 0], None,
                                       st_gdn, j, geo, False)
            u, st_gdn = _gdn_scan(proj, pc, abp, None, dtb, alog, nw, state_gdn[:, j, 1], yf,
                                  st_gdn, j, geo, True)
            w_out = wb['gdn_out']
        x = _out_proj(u, w_out, j, x, mod, 2, geo)

        proj, _ = _in_proj(x, norm_ffn_w[i], mod, 3, wb['ffn_in'], i, ffn_in_w.shape[2], None, geo)
        u = _ffn_conv(proj, ffn_conv_w[i].reshape(9, -1), ffn_conv_b[i], geo)
        x = _out_proj(u, wb['ffn_out'], i, x, mod, 5, geo,
                      final_w=final_norm_w if i == depth - 1 else None)

    y_ctx, y_lat = x
    return (y_ctx.reshape(Bc, SEQ, D), y_lat.reshape(Bs, LS, D), st_ssm, st_gla, st_gdn)
```

```python
import functools
from typing import NamedTuple

import numpy as np
import jax
import jax.numpy as jnp
from jax import lax
from jax.experimental import pallas as pl
from jax.experimental.pallas import tpu as pltpu

f32, bf16 = jnp.float32, jnp.bfloat16

BLK = 256
CHUNK = 64
SSD_CHUNK = 128
HALO = 16
GRID_W = 64
NORM_EPS = 1e-6
SMALL = 128
GLA_RANK = 16
GLA_NORMALIZER = 16.0
SSM_GROUPS = 8
SSM_HEADDIM = 64
SSM_STATE = 128
SSD_GP = 8
GDN_HP = 8
VMEM_LIMIT = 48 * 1024 * 1024

NN = (((1,), (0,)), ((), ()))
NT = (((1,), (1,)), ((), ()))
TN = (((0,), (0,)), ((), ()))


class Geo(NamedTuple):
    D: int
    Tc: int
    Ts: int
    SEQ: int
    LS: int
    Bc: int
    Bs: int

    @property
    def T(self):
        return self.Tc + self.Ts

    @property
    def nb(self):
        return self.T // BLK

    @property
    def ncb(self):
        return self.Tc // BLK

    @property
    def bpc(self):
        return self.SEQ // BLK

    @property
    def bps(self):
        return self.LS // BLK


def _mm(a, b, dims=NN):
    return lax.dot_general(a, b, dims, preferred_element_type=f32)


def _split3(a):
    hi = a.astype(bf16)
    r1 = a - hi.astype(f32)
    mid = r1.astype(bf16)
    lo = (r1 - mid.astype(f32)).astype(bf16)
    return hi, mid, lo


def _mm_x(a, b, dims=NN):
    if isinstance(a, tuple):
        ps = [_mm(p, b, dims) for p in a]
    elif a.dtype == bf16:
        ps = [_mm(a, p, dims) for p in _split3(b)]
    else:
        ps = [_mm(p, b, dims) for p in _split3(a)]
    return ps[0] + ps[1] + ps[2]


def _mm_hi(a, b):
    ah = a.astype(bf16)
    al = (a - ah.astype(f32)).astype(bf16)
    bh = b.astype(bf16)
    bl = (b - bh.astype(f32)).astype(bf16)
    return _mm(ah, bh) + _mm(al, bh) + _mm(ah, bl)


def _silu(x):
    h = 0.5 * x
    return h + h * jnp.tanh(h)


def _tri_mask(n, upper, chunk=None, strict=False):
    r = lax.broadcasted_iota(jnp.int32, (n, n), 0)
    c = lax.broadcasted_iota(jnp.int32, (n, n), 1)
    if upper:
        m = (c > r) if strict else (c >= r)
    else:
        m = (c < r) if strict else (c <= r)
    if chunk is not None and chunk < n:
        sh = int(np.log2(chunk))
        same = lax.shift_right_logical(r, sh) == lax.shift_right_logical(c, sh)
        m = jnp.logical_and(m, same)
    return m


def _as_bf16(mask):
    return jnp.where(mask, 1.0, 0.0).astype(bf16)


def _ada_kernel(c_ref, w_ref, b_ref, o_ref):
    o_ref[0] = _mm_hi(_silu(c_ref[...]), w_ref[0]) + b_ref[0]


def _ada_table(cond8, ada_w, ada_b):
    depth, D, N6 = ada_w.shape
    tn = N6 // 4
    return pl.pallas_call(
        _ada_kernel,
        out_shape=jax.ShapeDtypeStruct((depth, 8, N6), f32),
        grid=(depth, N6 // tn),
        in_specs=[pl.BlockSpec((8, D), lambda l, j: (0, 0)),
                  pl.BlockSpec((1, D, tn), lambda l, j: (l, 0, j)),
                  pl.BlockSpec((1, 1, tn), lambda l, j: (l, 0, j))],
        out_specs=pl.BlockSpec((1, 8, tn), lambda l, j: (l, 0, j)),
        compiler_params=pltpu.CompilerParams(
            dimension_semantics=("parallel", "parallel"), vmem_limit_bytes=VMEM_LIMIT),
        name="ada_table",
    )(cond8, ada_w, ada_b.reshape(depth, 1, N6))


def _in_proj_kernel(*refs, has_small, tn, dst_blk):
    if has_small:
        x_ref, nw_ref, sh_ref, sc_ref, w_ref, ws_ref, o_ref, os_ref = refs
    else:
        x_ref, nw_ref, sh_ref, sc_ref, w_ref, o_ref = refs
    x = x_ref[...]
    h = x * lax.rsqrt(jnp.mean(x * x, axis=-1, keepdims=True) + NORM_EPS) * nw_ref[...]
    h = h * (1.0 + sc_ref[0]) + sh_ref[0]
    hb = h.astype(bf16)
    if has_small:
        ws = ws_ref[...]
        wh = ws.astype(bf16)
        wl = (ws - wh.astype(f32)).astype(bf16)
        hl = (h - hb.astype(f32)).astype(bf16)
        os_ref[...] = _mm(hb, wh) + _mm(hl, wh) + _mm(hb, wl)
    per = tn // 128
    for n in range(w_ref.shape[2] // tn):
        r = _mm(hb, w_ref[0, :, n * tn:(n + 1) * tn]).astype(o_ref.dtype)
        if dst_blk is None:
            o_ref[:, n * tn:(n + 1) * tn] = r
            continue
        k = 0
        while k < per:
            d0 = dst_blk[n * per + k]
            run = 1
            while k + run < per and dst_blk[n * per + k + run] == d0 + run:
                run += 1
            o_ref[:, d0 * 128:(d0 + run) * 128] = r[:, k * 128:(k + run) * 128]
            k += run


def _row_tile(geo, pref):
    for tm in (pref, 512, 256):
        if tm <= pref and geo.Tc % tm == 0 and geo.LS % tm == 0:
            return tm
    raise ValueError("token counts must be multiples of 256")


def _cond_of_tile(i, geo, tm):
    nct = geo.Tc // tm
    return jnp.where(i < nct, 0, 1 + jnp.maximum(i - nct, 0) // (geo.LS // tm))


def _in_proj(x, norm_w, mod, shift_idx, w_stack, layer, N, w_small, geo, tn=512, dst_blk=None):
    T, D = x.shape
    tm = _row_tile(geo, 512)
    assert N % tn == 0 and w_stack.dtype == bf16
    has_small = w_small is not None
    cond = functools.partial(_cond_of_tile, geo=geo, tm=tm)
    in_specs = [pl.BlockSpec((tm, D), lambda i: (i, 0)),
                pl.BlockSpec((1, D), lambda i: (0, 0)),
                pl.BlockSpec((1, 1, D), lambda i: (cond(i), 0, shift_idx)),
                pl.BlockSpec((1, 1, D), lambda i: (cond(i), 0, shift_idx + 1)),
                pl.BlockSpec((1, D, N), lambda i: (layer, 0, 0), pipeline_mode=pl.Buffered(1))]
    args = [x, norm_w.reshape(1, D), mod, mod, w_stack]
    out_shape = [jax.ShapeDtypeStruct((T, N), bf16)]
    out_specs = [pl.BlockSpec((tm, N), lambda i: (i, 0))]
    if has_small:
        in_specs.append(pl.BlockSpec((D, SMALL), lambda i: (0, 0)))
        args.append(w_small)
        out_shape.append(jax.ShapeDtypeStruct((T, SMALL), f32))
        out_specs.append(pl.BlockSpec((tm, SMALL), lambda i: (i, 0)))
    res = pl.pallas_call(
        functools.partial(_in_proj_kernel, has_small=has_small, tn=tn, dst_blk=dst_blk),
        out_shape=out_shape,
        grid=(T // tm,),
        in_specs=in_specs,
        out_specs=out_specs,
        compiler_params=pltpu.CompilerParams(
            dimension_semantics=("parallel",), vmem_limit_bytes=VMEM_LIMIT),
        name="in_proj",
    )(*args)
    return res if has_small else (res[0], None)


CONV_HALO = 128
CONV_TF = 256


def _conv_glu_matmul(a_ref, ap_ref, an_ref, v_ref, cw_ref, cb_ref, w_ref, pad_ref, u_ref, acc_ref,
                     *, n_ctx_tiles, tiles_per_img, SEQ):
    i = pl.program_id(0)
    tm, F = a_ref.shape
    H = CONV_HALO
    pos_img = lax.rem(jnp.maximum(i - n_ctx_tiles, 0), tiles_per_img)
    top_ok = jnp.logical_and(i >= n_ctx_tiles, pos_img > 0)
    bot_ok = jnp.logical_and(i >= n_ctx_tiles, pos_img < tiles_per_img - 1)
    pad_ref[0:H, :] = jnp.where(top_ok, ap_ref[...].astype(f32), 0.0)
    pad_ref[H + tm:H + tm + H, :] = jnp.where(bot_ok, an_ref[...].astype(f32), 0.0)
    pad_ref[H:H + tm, :] = a_ref[...].astype(f32)
    tf = CONV_TF
    row = lax.broadcasted_iota(jnp.int32, (BLK, tf), 0)

    def run(grid):
        acc = None
        for n in range(F // tf):
            cs = slice(n * tf, (n + 1) * tf)
            w = cw_ref[:, cs]
            b = cb_ref[:, cs]

            def taps(s, pos, width, kr):
                left = jnp.where(pos > 0, pad_ref[s - 1:s - 1 + BLK, cs], 0.0)
                right = jnp.where(pos < width - 1, pad_ref[s + 1:s + 1 + BLK, cs], 0.0)
                return (left * w[3 * kr:3 * kr + 1] + pad_ref[s:s + BLK, cs] * w[3 * kr + 1:3 * kr + 2]
                        + right * w[3 * kr + 2:3 * kr + 3])

            for c in range(tm // BLK):
                s = H + c * BLK
                if grid:
                    col = jnp.bitwise_and(row, GRID_W - 1)
                    conv = (taps(s - GRID_W, col, GRID_W, 0) + taps(s, col, GRID_W, 1)
                            + taps(s + GRID_W, col, GRID_W, 2))
                else:
                    pos = jnp.bitwise_and(row + (c * BLK) % SEQ, SEQ - 1)
                    conv = taps(s, pos, SEQ, 1)
                v = v_ref[c * BLK:(c + 1) * BLK, cs].astype(f32)
                u_ref[c * BLK:(c + 1) * BLK, cs] = (_silu(conv + b) * v).astype(bf16)
            part = _mm(u_ref[:, cs], w_ref[0, cs, :])
            acc = part if acc is None else acc + part
        acc_ref[...] = acc

    @pl.when(i < n_ctx_tiles)
    def _():
        run(False)

    @pl.when(i >= n_ctx_tiles)
    def _():
        run(True)

    return acc_ref[...]


def _out_proj_kernel(*refs, n_ctx_tiles, final, conv):
    if conv is not None:
        a_ref, ap_ref, an_ref, v_ref, cw_ref, cb_ref = refs[:6]
        refs = refs[6:]
    else:
        u_ref = refs[0]
        refs = refs[1:]
    w_ref, x_ref, g_ref = refs[:3]
    refs = refs[3:]
    if final:
        fw_ref, oc_ref, os_ref = refs[:3]
        scratch = refs[3:]
    else:
        o_ref = refs[0]
        scratch = refs[1:]
    if conv is not None:
        acc = _conv_glu_matmul(a_ref, ap_ref, an_ref, v_ref, cw_ref, cb_ref, w_ref, *scratch,
                               n_ctx_tiles=n_ctx_tiles, **conv)
    else:
        acc = _mm(u_ref[...], w_ref[0])
    r = x_ref[...] + g_ref[0] * acc
    if not final:
        o_ref[...] = r
        return
    r = r * lax.rsqrt(jnp.mean(r * r, axis=-1, keepdims=True) + NORM_EPS) * fw_ref[...]

    @pl.when(pl.program_id(0) < n_ctx_tiles)
    def _():
        oc_ref[...] = r

    @pl.when(pl.program_id(0) >= n_ctx_tiles)
    def _():
        os_ref[...] = r


def _out_proj(u, w_stack, layer, x, mod, gate_idx, geo, final_w=None, conv=None):
    T = u.shape[0]
    K, D = w_stack.shape[1:]
    tm = _row_tile(geo, 512)
    cond = functools.partial(_cond_of_tile, geo=geo, tm=tm)
    final = final_w is not None
    nct = geo.Tc // tm
    if conv is None:
        in_specs = [pl.BlockSpec((tm, K), lambda i: (i, 0))]
        args = [u]
        scratch, conv_cfg = [], None
    else:
        w9, cb = conv
        H = CONV_HALO
        per, lastb = tm // H, T // H - 1
        assert geo.LS % tm == 0 and tm % geo.SEQ == 0 and geo.SEQ & (geo.SEQ - 1) == 0 and K % CONV_TF == 0
        assert BLK % GRID_W == 0 and H > GRID_W
        in_specs = [pl.BlockSpec((tm, K), lambda i: (i, 0)),
                    pl.BlockSpec((H, K), lambda i: (jnp.maximum(i * per - 1, 0), 0)),
                    pl.BlockSpec((H, K), lambda i: (jnp.minimum((i + 1) * per, lastb), 0)),
                    pl.BlockSpec((tm, K), lambda i: (i, 1)),
                    pl.BlockSpec((9, K), lambda i: (0, 0)),
                    pl.BlockSpec((1, K), lambda i: (0, 0))]
        args = [u, u, u, u, w9, cb.reshape(1, K)]
        scratch = [pltpu.VMEM((tm + 2 * H, K), f32), pltpu.VMEM((tm, K), bf16), pltpu.VMEM((tm, D), f32)]
        conv_cfg = dict(tiles_per_img=geo.LS // tm, SEQ=geo.SEQ)
    in_specs += [pl.BlockSpec((1, K, D), lambda i: (layer, 0, 0), pipeline_mode=pl.Buffered(1)),
                 pl.BlockSpec((tm, D), lambda i: (i, 0)),
                 pl.BlockSpec((1, 1, D), lambda i: (cond(i), 0, gate_idx))]
    args += [w_stack, x, mod]
    if final:
        in_specs.append(pl.BlockSpec((1, D), lambda i: (0, 0)))
        args.append(final_w.reshape(1, D))
        out_shape = [jax.ShapeDtypeStruct((geo.Tc, D), f32), jax.ShapeDtypeStruct((geo.Ts, D), f32)]
        out_specs = [pl.BlockSpec((tm, D), lambda i: (jnp.minimum(i, nct - 1), 0)),
                     pl.BlockSpec((tm, D), lambda i: (jnp.maximum(i - nct, 0), 0))]
    else:
        out_shape = jax.ShapeDtypeStruct((T, D), f32)
        out_specs = pl.BlockSpec((tm, D), lambda i: (i, 0))
    return pl.pallas_call(
        functools.partial(_out_proj_kernel, n_ctx_tiles=nct, final=final, conv=conv_cfg),
        out_shape=out_shape,
        grid=(T // tm,),
        in_specs=in_specs,
        out_specs=out_specs,
        scratch_shapes=scratch,
        compiler_params=pltpu.CompilerParams(
            dimension_semantics=("arbitrary" if final else "parallel",), vmem_limit_bytes=56 * 1024 * 1024),
        name="ffn_out" if conv is not None else "out_proj",
    )(*args)


def _blk_idx(i, geo, rev):
    return (geo.nb - 1 - i) if rev else i


def _tok_spec(cw, colfn, geo, rev):
    return pl.BlockSpec((BLK, cw), lambda u, i: (_blk_idx(i, geo, rev), colfn(u)))


def _halo_specs(cw, colfn, geo, rev):
    per = BLK // HALO
    last = geo.T // HALO - 1
    prev = pl.BlockSpec((HALO, cw), lambda u, i: (jnp.maximum(_blk_idx(i, geo, rev) * per - 1, 0), colfn(u)))
    nxt = pl.BlockSpec((HALO, cw), lambda u, i: (jnp.minimum((_blk_idx(i, geo, rev) + 1) * per, last), colfn(u)))
    return prev, nxt


def _h0_spec(geo, rev, hp):
    def idx(u, i):
        ip = _blk_idx(i, geo, rev)
        return (jnp.maximum(ip - geo.ncb, 0) // geo.bps, u, 0, 0)
    return pl.BlockSpec((1, hp, 128, 256), idx)


def _st_spec(geo, rev, layer, blk):
    d = 1 if rev else 0

    def idx(u, i):
        ip = _blk_idx(i, geo, rev)
        return (jnp.minimum(ip // geo.bpc, geo.Bc - 1), layer, d, u, 0, 0)
    return pl.BlockSpec((1, 1, 1) + blk, idx)


def _param_spec(rows, cw, colfn):
    return pl.BlockSpec((rows, cw), lambda u, i: (0, colfn(u)))


def _flags(geo, rev):
    ip = _blk_idx(pl.program_id(1), geo, rev)
    is_ctx = ip < geo.ncb
    pos = jnp.where(is_ctx, lax.rem(ip, geo.bpc), lax.rem(jnp.maximum(ip - geo.ncb, 0), geo.bps))
    n = jnp.where(is_ctx, geo.bpc, geo.bps)
    first = pos == 0
    last = pos == n - 1
    return is_ctx, first, last, (last if rev else first)


def _conv3_silu(cur_ref, prev_ref, next_ref, w_ref, b_ref, first, last):
    cur = cur_ref[...].astype(f32)
    n = cur.shape[0]
    row = lax.broadcasted_iota(jnp.int32, cur.shape, 0)
    pr = jnp.where(first, 0.0, prev_ref[HALO - 1:HALO, :].astype(f32))
    nx = jnp.where(last, 0.0, next_ref[0:1, :].astype(f32))
    xm = jnp.where(row == 0, pr, pltpu.roll(cur, 1, axis=0))
    xp = jnp.where(row == n - 1, nx, pltpu.roll(cur, n - 1, axis=0))
    w = w_ref[...]
    y = xm * w[0:1] + cur * w[1:2] + xp * w[2:3]
    if b_ref is not None:
        y = y + b_ref[...]
    return _silu(y)


def _init_state(S, h0_ref, is_ctx, start):
    @pl.when(start)
    def _():
        S[...] = jnp.where(is_ctx, 0.0, h0_ref[0])


def _norm_gate(o, gate, nw, gate_first):
    g = _silu(gate)
    if gate_first:
        o = o * g
    o = o * lax.rsqrt(jnp.mean(o * o, axis=-1, keepdims=True) + NORM_EPS) * nw
    if not gate_first:
        o = o * g
    return o


def _scan_call(kernel, name, grid_u, in_specs, args, outs, st, st_blk, layer, geo, rev, hp):
    out_shape = [jax.ShapeDtypeStruct((geo.T, c), dt) for c, dt, _ in outs]
    out_specs = [_tok_spec(bc, lambda u: u, geo, rev) for _, _, bc in outs]
    fresh = isinstance(st, jax.ShapeDtypeStruct)
    extra_specs, extra_args, alias = [], [], {}
    if not fresh:
        extra_specs, extra_args, alias = [pl.BlockSpec(memory_space=pl.ANY)], [st], {len(args): len(outs)}
    return pl.pallas_call(
        kernel,
        out_shape=out_shape + [jax.ShapeDtypeStruct(st.shape, st.dtype)],
        grid=(grid_u, geo.nb),
        in_specs=in_specs + extra_specs,
        out_specs=out_specs + [_st_spec(geo, rev, layer, st_blk)],
        scratch_shapes=[pltpu.VMEM((hp, 128, 256), f32)],
        input_output_aliases=alias,
        compiler_params=pltpu.CompilerParams(
            dimension_semantics=("parallel", "arbitrary"), vmem_limit_bytes=VMEM_LIMIT),
        name=name,
    )(*args, *extra_args)


def _ssd_kernel(*refs, rev, geo, gp, aliased):
    skip = 1 if aliased else 0
    if rev:
        (pc_ref, dt_ref, dtb_ref, alog_ref, xsel_ref, xh_ref, h0_ref,
         dsk_ref, z_ref, nw_ref, yf_ref) = refs[:11]
        y_ref, st_ref, S = refs[11 + skip:]
    else:
        (xbc_c, xbc_p, xbc_n, dt_ref, cw_ref, cb_ref, dtb_ref, alog_ref, xsel_ref, xh_ref, h0_ref) = refs[:11]
        y_ref, pc_ref, st_ref, S = refs[11 + skip:]
    is_ctx, first, last, start = _flags(geo, rev)
    _init_state(S, h0_ref, is_ctx, start)
    Q = SSD_CHUNK
    nch = BLK // Q
    groups = range(gp)

    if rev:
        xbc = pc_ref[...].astype(f32)
    else:
        xbc = _conv3_silu(xbc_c, xbc_p, xbc_n, cw_ref, cb_ref, first, last)
        pc_ref[...] = xbc.astype(bf16)
    dt = jax.nn.softplus(dt_ref[...] + dtb_ref[...])
    la = dt * (-jnp.exp(alog_ref[...]))
    tri = _tri_mask(BLK, rev, chunk=Q)
    cs3 = _split3(_mm_x(_as_bf16(tri), la))
    dtb16 = dt.astype(bf16)
    tri_q = _tri_mask(Q, rev)

    xs, xvs, bms, cms, cses, csts = [], [], [], [], [], []
    for g in groups:
        x = xbc[:, g * 512:g * 512 + 256]
        xs.append(x)
        bms.append(xbc[:, g * 512 + 256:g * 512 + 384].astype(bf16))
        cms.append(xbc[:, g * 512 + 384:g * 512 + 512].astype(bf16))
        xsel = xsel_ref[g]
        xvs.append(x * _mm(dtb16, xsel))
        cses.append(_mm_x(cs3, xsel))
        csts.append(_mm_x(cs3, xh_ref[g]).T)
    Svs = [S[g] for g in groups]
    youts = [[None] * nch for _ in groups]
    for c in (range(nch - 1, -1, -1) if rev else range(nch)):
        lo = c * Q
        li = lo if rev else lo + Q - 1
        for g in groups:
            cs_e = cses[g][lo:lo + Q]
            xv = xvs[g][lo:lo + Q]
            xvb = xv.astype(bf16)
            cmb = cms[g][lo:lo + Q]
            bmb = bms[g][lo:lo + Q]
            gm = _mm(cmb, bmb, NT)
            ys = []
            for r in range(4):
                seg = cs_e[:, r * 64:r * 64 + 1] - csts[g][r:r + 1, lo:lo + Q]
                m = jnp.where(tri_q, gm * jnp.exp(seg), 0.0)
                ys.append(_mm(m.astype(bf16), xvb[:, r * 64:(r + 1) * 64]))
            Sv = Svs[g]
            y = jnp.concatenate(ys, axis=1) + _mm(cmb, Sv.astype(bf16)) * jnp.exp(cs_e)
            cl = cses[g][li:li + 1, :]
            xw = xv * jnp.exp(cl - cs_e)
            Svs[g] = Sv * jnp.exp(cl) + _mm(bmb, xw.astype(bf16), TN)
            youts[g][c] = y

    for g in groups:
        S[g] = Svs[g]
        y = jnp.concatenate(youts[g], axis=0)
        if rev:
            cols = slice(g * 256, (g + 1) * 256)
            o = yf_ref[:, cols] + y + xs[g] * dsk_ref[:, cols]
            y = _norm_gate(o, z_ref[:, cols].astype(f32), nw_ref[:, cols], gate_first=True)
        y_ref[:, g * 256:(g + 1) * 256] = y.astype(y_ref.dtype)

    @pl.when(is_ctx)
    def _():
        for g in groups:
            for r in range(4):
                st_ref[0, 0, 0, g * 4 + r] = Svs[g][:, r * 64:(r + 1) * 64]


def _ssd_scan(proj, pc, dtp, conv_w, conv_b, dtb, alog, dskip, norm_w, h0, yf, st, layer, geo, rev):
    G, gp = SSM_GROUPS, SSD_GP
    nu = G // gp
    d = 1 if rev else 0
    xsel = np.zeros((G, 128, 256), np.float32)
    xh = np.zeros((G, 128, 128), np.float32)
    for g in range(G):
        for r in range(4):
            xsel[g, d * 32 + 4 * g + r, r * 64:(r + 1) * 64] = 1.0
            xh[g, d * 32 + 4 * g + r, r] = 1.0
    cw, zw = 512 * gp, 256 * gp
    head_specs = [_param_spec(1, SMALL, lambda u: 0), _param_spec(1, SMALL, lambda u: 0),
                  pl.BlockSpec((gp, 128, 256), lambda u, i: (u, 0, 0)),
                  pl.BlockSpec((gp, 128, 128), lambda u, i: (u, 0, 0)),
                  _h0_spec(geo, rev, gp)]
    head_args = [dtb, alog, jnp.asarray(xsel, bf16), jnp.asarray(xh, bf16), h0]
    if rev:
        in_specs = [_tok_spec(cw, lambda u: u, geo, rev), _tok_spec(SMALL, lambda u: 0, geo, rev), *head_specs,
                    _param_spec(1, zw, lambda u: u), _tok_spec(zw, lambda u: 2 * nu + u, geo, rev),
                    _param_spec(1, zw, lambda u: u), _tok_spec(zw, lambda u: u, geo, rev)]
        args = [pc, dtp, *head_args, dskip, proj, norm_w, yf]
        outs = [(2048, bf16, zw)]
    else:
        in_specs = [_tok_spec(cw, lambda u: u, geo, rev), *_halo_specs(cw, lambda u: u, geo, rev),
                    _tok_spec(SMALL, lambda u: 0, geo, rev),
                    _param_spec(3, cw, lambda u: u), _param_spec(1, cw, lambda u: u), *head_specs]
        args = [proj, proj, proj, dtp, conv_w, conv_b, *head_args]
        outs = [(2048, f32, zw), (4096, bf16, cw)]
    kern = functools.partial(_ssd_kernel, rev=rev, geo=geo, gp=gp, aliased=not isinstance(st, jax.ShapeDtypeStruct))
    return _scan_call(kern, "ssd_bwd" if rev else "ssd_fwd",
                      nu, in_specs, args, outs, st, (4 * gp, SSM_STATE, SSM_HEADDIM), layer, geo, rev, gp)


def _gla_kernel(*refs, rev, geo, nh, aliased):
    qkv_ref, lr_ref, w2_ref, gb_ref, h0_ref = refs[:5]
    skip = 1 if aliased else 0
    if rev:
        r_ref, nw_ref, yf_ref = refs[5:8]
        y_ref, st_ref, S = refs[8 + skip:]
    else:
        y_ref, st_ref, S = refs[5 + skip:]
    is_ctx, first, last, start = _flags(geo, rev)
    _init_state(S, h0_ref, is_ctx, start)
    d = 1 if rev else 0
    dk, dv = 128, 256
    heads = range(nh)
    K = nh * dk

    q_all = qkv_ref[:, 0:K].astype(f32) * dk ** -0.5
    k_all = qkv_ref[:, K:2 * K].astype(f32)
    vb_all = qkv_ref[:, 2 * K:2 * K + nh * dv]
    lr = lr_ref[...][:, d * GLA_RANK:(d + 1) * GLA_RANK]
    logit = _mm_hi(lr, w2_ref[0]) + gb_ref[0]
    lg = jax.nn.log_sigmoid(logit) * (1.0 / GLA_NORMALIZER)
    tri = _tri_mask(BLK, rev, chunk=CHUNK)
    cs = _mm_x(_as_bf16(tri), lg)
    cs_t = _mm_x(lg, _as_bf16(_tri_mask(BLK, not rev, chunk=CHUNK)), TN)
    qgb = (q_all * jnp.exp(cs)).astype(bf16)
    kgb = (k_all * jnp.exp(-cs)).astype(bf16)

    ys = []
    for h in heads:
        sc = jnp.where(tri, _mm(qgb[:, h * dk:(h + 1) * dk], kgb[:, h * dk:(h + 1) * dk], NT), 0.0)
        ys.append(_mm(sc.astype(bf16), vb_all[:, h * dv:(h + 1) * dv]))
    Svs = [S[h] for h in heads]
    nch = BLK // CHUNK
    yin = [[None] * nch for _ in heads]
    for c in (range(nch - 1, -1, -1) if rev else range(nch)):
        lo = c * CHUNK
        li = lo if rev else lo + CHUNK - 1
        kw_all = (k_all[lo:lo + CHUNK] * jnp.exp(cs[li:li + 1, :] - cs[lo:lo + CHUNK])).astype(bf16)
        for h in heads:
            yin[h][c] = _mm(qgb[lo:lo + CHUNK, h * dk:(h + 1) * dk], Svs[h].astype(bf16))
        for h in heads:
            Svs[h] = (Svs[h] * jnp.exp(cs_t[h * dk:(h + 1) * dk, li:li + 1])
                      + _mm(kw_all[:, h * dk:(h + 1) * dk], vb_all[lo:lo + CHUNK, h * dv:(h + 1) * dv], TN))

    for h in heads:
        S[h] = Svs[h]
        y = ys[h] + jnp.concatenate(yin[h], axis=0)
        cols = slice(h * dv, (h + 1) * dv)
        if rev:
            y = _norm_gate(yf_ref[:, cols] + y, r_ref[:, cols].astype(f32), nw_ref[...], gate_first=False)
        y_ref[:, cols] = y.astype(y_ref.dtype)

    @pl.when(is_ctx)
    def _():
        for h in heads:
            st_ref[0, 0, 0, h] = Svs[h]


def _gla_scan(proj, lrp, w2, gb, norm_w, h0, yf, st, layer, geo, rev):
    H = 4
    d = 1 if rev else 0
    in_specs = [_tok_spec(2048, lambda u: 0, geo, rev), _tok_spec(SMALL, lambda u: 0, geo, rev),
                pl.BlockSpec((1, GLA_RANK, 512), lambda u, i: (d, 0, 0)),
                pl.BlockSpec((1, 1, 512), lambda u, i: (d, 0, 0)),
                _h0_spec(geo, rev, H)]
    args = [proj, lrp, w2, gb, h0]
    if rev:
        in_specs += [_tok_spec(1024, lambda u: 2, geo, rev), _param_spec(1, 256, lambda u: 0),
                     _tok_spec(1024, lambda u: 0, geo, rev)]
        args += [proj, norm_w, yf]
    kern = functools.partial(_gla_kernel, rev=rev, geo=geo, nh=H, aliased=not isinstance(st, jax.ShapeDtypeStruct))
    return _scan_call(kern, "gla_bwd" if rev else "gla_fwd",
                      1, in_specs, args, [(1024, bf16 if rev else f32, 1024)], st, (H, 128, 256), layer, geo, rev, H)


def _unit_tri_inverse(ms, lev):
    eye = jnp.where(lev == -1, 1.0, 0.0)
    dinv = [eye - jnp.where(lev == 0, m, 0.0) for m in ms]
    for b in range(1, 6):
        ts = [_mm(jnp.where(lev == b, m, 0.0).astype(bf16), d.astype(bf16)).astype(bf16)
              for m, d in zip(ms, dinv)]
        dinv = [d - _mm(d.astype(bf16), t) for d, t in zip(dinv, ts)]
    return dinv


def _gdn_kernel(*refs, rev, geo, hp, aliased):
    skip = 1 if aliased else 0
    if rev:
        (pc_ref, ab_ref, dtb_ref, alog_ref, xab_ref, lev_ref, h0_ref, z_ref, nw_ref, yf_ref) = refs[:10]
        y_ref, st_ref, S = refs[10 + skip:]
    else:
        (qkv_c, qkv_p, qkv_n, cw_ref, ab_ref, dtb_ref, alog_ref, xab_ref, lev_ref, h0_ref) = refs[:10]
        y_ref, pc_ref, st_ref, S = refs[10 + skip:]
    is_ctx, first, last, start = _flags(geo, rev)
    _init_state(S, h0_ref, is_ctx, start)
    dk, dv = 128, 256
    heads = range(hp)

    if rev:
        qkv = pc_ref[...].astype(f32)
    else:
        qkv = _conv3_silu(qkv_c, qkv_p, qkv_n, cw_ref, None, first, last)

    ab = ab_ref[...]
    gall = -jnp.exp(alog_ref[...]) * jax.nn.softplus(ab + dtb_ref[...])
    lane = lax.broadcasted_iota(jnp.int32, ab.shape, 1)
    sel = _mm_x(jnp.where(lane < 16, gall, jax.nn.sigmoid(ab)), xab_ref[0])
    lev = lev_ref[...]
    r_i = lax.broadcasted_iota(jnp.int32, (BLK, BLK), 0)
    c_i = lax.broadcasted_iota(jnp.int32, (BLK, BLK), 1)
    in_chunk = lev < 6
    incl = jnp.logical_and(in_chunk, (c_i >= r_i) if rev else (c_i <= r_i))
    incl_t = jnp.logical_and(in_chunk, (c_i <= r_i) if rev else (c_i >= r_i))
    strict = jnp.logical_and(incl, lev >= 0)
    cs_all = _mm_x(_as_bf16(incl), sel)
    cs_rows = _mm_x(sel, _as_bf16(incl_t), TN)

    qs, ks, kbfs, kbs, vbs, css, lmasks, ms = [], [], [], [], [], [], [], []
    for hh in heads:
        q = qkv[:, hh * dk:(hh + 1) * dk]
        k = qkv[:, (hp + hh) * dk:(hp + hh + 1) * dk]
        v = qkv[:, 2 * hp * dk + hh * dv:2 * hp * dk + (hh + 1) * dv]
        if not rev:
            q = q * lax.rsqrt(jnp.sum(q * q, axis=-1, keepdims=True) + 1e-6) * dk ** -0.5
            k = k * lax.rsqrt(jnp.sum(k * k, axis=-1, keepdims=True) + 1e-6)
            pc_ref[:, hh * dk:(hh + 1) * dk] = q.astype(bf16)
            pc_ref[:, (hp + hh) * dk:(hp + hh + 1) * dk] = k.astype(bf16)
            pc_ref[:, 2 * hp * dk + hh * dv:2 * hp * dk + (hh + 1) * dv] = v.astype(bf16)
        beta = sel[:, 2 * hh + 1:2 * hh + 2]
        cs = cs_all[:, 2 * hh:2 * hh + 1]
        lmask = jnp.where(incl, jnp.exp(cs - cs_rows[2 * hh:2 * hh + 1, :]), 0.0)
        kb = k * beta
        kbf = k.astype(bf16)
        qs.append(q)
        ks.append(k)
        kbfs.append(kbf)
        kbs.append(kb)
        vbs.append((v * beta).astype(bf16))
        css.append(cs)
        lmasks.append(lmask)
        ms.append(jnp.where(strict, _mm(kb.astype(bf16), kbf, NT) * lmask, 0.0))
    tbs = [t.astype(bf16) for t in _unit_tri_inverse(ms, lev)]
    ecs = [jnp.exp(cs) for cs in css]
    us = [_mm(tbs[h], vbs[h]) for h in heads]
    wks = [_mm(tbs[h], (kbs[h] * ecs[h]).astype(bf16)).astype(bf16) for h in heads]
    a_ins = [(_mm(qs[h].astype(bf16), kbfs[h], NT) * lmasks[h]).astype(bf16) for h in heads]
    qds = [(qs[h] * ecs[h]).astype(bf16) for h in heads]

    Svs = [S[h] for h in heads]
    nch = BLK // CHUNK
    outs = [[None] * nch for _ in heads]
    for c in (range(nch - 1, -1, -1) if rev else range(nch)):
        lo = c * CHUNK
        li = lo if rev else lo + CHUNK - 1
        Sbs = [Sv.astype(bf16) for Sv in Svs]
        v_news = [(us[h][lo:lo + CHUNK] - _mm(wks[h][lo:lo + CHUNK], Sbs[h])).astype(bf16) for h in heads]
        for h in heads:
            outs[h][c] = (_mm(qds[h][lo:lo + CHUNK], Sbs[h])
                          + _mm(a_ins[h][lo:lo + CHUNK, lo:lo + CHUNK], v_news[h]))
        for h in heads:
            cl = css[h][li:li + 1, :]
            kd = ks[h][lo:lo + CHUNK] * jnp.exp(cl - css[h][lo:lo + CHUNK])
            Svs[h] = Svs[h] * jnp.exp(cl) + _mm(kd.astype(bf16), v_news[h], TN)

    for hh in heads:
        S[hh] = Svs[hh]
        o = jnp.concatenate(outs[hh], axis=0)
        cols = slice(hh * dv, (hh + 1) * dv)
        if rev:
            o = _norm_gate(yf_ref[:, cols] + o, z_ref[:, cols].astype(f32), nw_ref[...], gate_first=False)
        y_ref[:, cols] = o.astype(y_ref.dtype)

    @pl.when(is_ctx)
    def _():
        for hh in heads:
            st_ref[0, 0, 0, hh] = Svs[hh]


def _gdn_scan(proj, pc, abp, conv_w, dtb, alog, norm_w, h0, yf, st, layer, geo, rev):
    H, hp = 8, GDN_HP
    nu = H // hp
    d = 1 if rev else 0
    xab = np.zeros((nu, 128, 128), np.float32)
    for h in range(H):
        xab[h // hp, d * H + h, 2 * (h % hp)] = 1.0
        xab[h // hp, 2 * H + d * H + h, 2 * (h % hp) + 1] = 1.0
    idx = np.arange(BLK)
    x = idx[:, None] ^ idx[None, :]
    lev = np.where(x == 0, -1, np.floor(np.log2(np.maximum(x, 1)))).astype(np.int32)
    cw, vw = 512 * hp, 256 * hp
    head_specs = [_tok_spec(SMALL, lambda u: 0, geo, rev),
                  _param_spec(1, SMALL, lambda u: 0), _param_spec(1, SMALL, lambda u: 0),
                  pl.BlockSpec((1, 128, 128), lambda u, i: (u, 0, 0)),
                  pl.BlockSpec((BLK, BLK), lambda u, i: (0, 0)),
                  _h0_spec(geo, rev, hp)]
    head_args = [abp, dtb, alog, jnp.asarray(xab, bf16), jnp.asarray(lev), h0]
    if rev:
        in_specs = [_tok_spec(cw, lambda u: u, geo, rev), *head_specs,
                    _tok_spec(vw, lambda u: 2 * nu + u, geo, rev), _param_spec(1, 256, lambda u: 0),
                    _tok_spec(vw, lambda u: u, geo, rev)]
        args = [pc, *head_args, proj, norm_w, yf]
        outs = [(2048, bf16, vw)]
    else:
        in_specs = [_tok_spec(cw, lambda u: u, geo, rev), *_halo_specs(cw, lambda u: u, geo, rev),
                    _param_spec(3, cw, lambda u: u), *head_specs]
        args = [proj, proj, proj, conv_w, *head_args]
        outs = [(2048, f32, vw), (4096, bf16, cw)]
    kern = functools.partial(_gdn_kernel, rev=rev, geo=geo, hp=hp, aliased=not isinstance(st, jax.ShapeDtypeStruct))
    return _scan_call(kern, "gdn_bwd" if rev else "gdn_fwd",
                      nu, in_specs, args, outs, st, (hp, 128, 256), layer, geo, rev, hp)


def _pad_cols(w, n=SMALL):
    return jnp.pad(w.astype(f32), ((0, 0), (0, n - w.shape[1])))


def _pad_row(v, n=SMALL):
    v = v.reshape(1, -1).astype(f32)
    return jnp.pad(v, ((0, 0), (0, n - v.shape[1])))


def _ssd_perm():
    perm = []
    for g in range(SSM_GROUPS):
        perm += list(range(g * 256, (g + 1) * 256))
        perm += list(range(2048 + g * 128, 2048 + (g + 1) * 128))
        perm += list(range(3072 + g * 128, 3072 + (g + 1) * 128))
    return np.asarray(perm, np.int32)


def _ssd_dst_blocks():
    dst = [32 + i for i in range(16)]
    dst += [4 * (i // 2) + i % 2 for i in range(16)]
    dst += [4 * g + 2 for g in range(SSM_GROUPS)]
    dst += [4 * g + 3 for g in range(SSM_GROUPS)]
    return tuple(dst)


def _gdn_dst_blocks(hp):
    inv = np.argsort(_gdn_perm(hp)[::128] // 128)
    return tuple(int(b) for b in inv) + tuple(range(32, 48))


def _gdn_perm(hp):
    perm = []
    for u in range(8 // hp):
        perm += list(range(u * hp * 128, (u + 1) * hp * 128))
        perm += list(range(1024 + u * hp * 128, 1024 + (u + 1) * hp * 128))
        perm += list(range(2048 + u * hp * 256, 2048 + (u + 1) * hp * 256))
    return np.asarray(perm, np.int32)


def kernel(x_prompt, x_sample, state_ssm, state_gla, state_gdn, c, c_ctx, norm_mix_w, norm_ffn_w, ada_w, ada_b, ffn_in_w, ffn_conv_w, ffn_conv_b, ffn_out_w, ssm_in_w, ssm_conv_w, ssm_conv_b, ssm_dt_bias, ssm_a_log, ssm_d, ssm_norm_w, ssm_out_w, gla_in_w, gla_gate_w1, gla_gate_w2, gla_gate_b, gla_norm_w, gla_out_w, gdn_in_w, gdn_conv_w, gdn_dt_bias, gdn_a_log, gdn_norm_w, gdn_out_w, final_norm_w):
    Bc, SEQ, D = x_prompt.shape
    Bs, LS, _ = x_sample.shape
    geo = Geo(D=D, Tc=Bc * SEQ, Ts=Bs * LS, SEQ=SEQ, LS=LS, Bc=Bc, Bs=Bs)
    assert SEQ % BLK == 0 and LS % BLK == 0 and Bs < 8
    depth = ada_w.shape[0]
    G, P, N = SSM_GROUPS, SSM_HEADDIM, SSM_STATE

    x = jnp.concatenate([x_prompt.reshape(geo.Tc, D), x_sample.reshape(geo.Ts, D)], axis=0)
    cond8 = jnp.zeros((8, D), f32).at[0].set(c_ctx).at[1:1 + Bs].set(c)
    mod_all = _ada_table(cond8, ada_w, ada_b)[:, :1 + Bs].reshape(depth, 1 + Bs, 1, 6 * D)

    n_ssm = ssm_out_w.shape[1] + 4096
    n_gdn = 6144
    wb = dict(ssm_in=ssm_in_w[:, :, :n_ssm].astype(bf16), ssm_out=ssm_out_w.astype(bf16),
              gla_in=gla_in_w.astype(bf16), gla_out=gla_out_w.astype(bf16),
              gdn_in=gdn_in_w[:, :, :n_gdn].astype(bf16), gdn_out=gdn_out_w.astype(bf16),
              ffn_in=ffn_in_w.astype(bf16), ffn_out=ffn_out_w.astype(bf16))
    st_ssm = jax.ShapeDtypeStruct((Bc, ssm_in_w.shape[0], 2, G * 4, N, P), f32)
    st_gla = jax.ShapeDtypeStruct((Bc, gla_in_w.shape[0], 2) + state_gla.shape[3:], f32)
    st_gdn = jax.ShapeDtypeStruct((Bc, gdn_in_w.shape[0], 2) + state_gdn.shape[3:], f32)
    for i in range(depth):
        mod = mod_all[i]
        kind, j = i % 3, i // 3
        if kind == 0:
            di = ssm_out_w.shape[1]
            perm = _ssd_perm()
            proj, dtp = _in_proj(x, norm_mix_w[i], mod, 0, wb['ssm_in'], j, n_ssm,
                                 _pad_cols(ssm_in_w[j, :, n_ssm:]), geo, dst_blk=_ssd_dst_blocks())
            dtb, alog = _pad_row(ssm_dt_bias[j]), _pad_row(ssm_a_log[j])
            dskip = jnp.repeat(ssm_d[j], P).reshape(1, di)
            nw = ssm_norm_w[j].reshape(1, di)
            cw, cb = ssm_conv_w[j][:, perm], ssm_conv_b[j][perm].reshape(1, -1)
            h0 = state_ssm[:, j].reshape(Bs, 2, G, 4, N, P).transpose(1, 0, 2, 4, 3, 5).reshape(2, Bs, G, N, 4 * P)
            yf, pc, st_ssm = _ssd_scan(proj, None, dtp, cw, cb, dtb, alog, None, None, h0[0], None,
                                       st_ssm, j, geo, False)
            u, st_ssm = _ssd_scan(proj, pc, dtp, None, None, dtb, alog, dskip, nw, h0[1], yf,
                                  st_ssm, j, geo, True)
            w_out = wb['ssm_out']
        elif kind == 1:
            w_small = _pad_cols(jnp.concatenate([gla_gate_w1[j, 0], gla_gate_w1[j, 1]], axis=1))
            proj, lrp = _in_proj(x, norm_mix_w[i], mod, 0, wb['gla_in'], j, gla_in_w.shape[2], w_small, geo)
            w2 = gla_gate_w2[j]
            gb = gla_gate_b[j].reshape(2, 1, -1)
            nw = gla_norm_w[j].reshape(1, -1)
            yf, st_gla = _gla_scan(proj, lrp, w2, gb, None, state_gla[:, j, 0], None, st_gla, j, geo, False)
            u, st_gla = _gla_scan(proj, lrp, w2, gb, nw, state_gla[:, j, 1], yf, st_gla, j, geo, True)
            w_out = wb['gla_out']
        else:
            perm = _gdn_perm(GDN_HP)
            proj, abp = _in_proj(x, norm_mix_w[i], mod, 0, wb['gdn_in'], j, n_gdn,
                                 _pad_cols(gdn_in_w[j, :, n_gdn:]), geo, dst_blk=_gdn_dst_blocks(GDN_HP))
            dtb, alog = _pad_row(gdn_dt_bias[j]), _pad_row(gdn_a_log[j])
            nw = gdn_norm_w[j].reshape(1, -1)
            cw = gdn_conv_w[j][:, perm]
            yf, pc, st_gdn = _gdn_scan(proj, None, abp, cw, dtb, alog, None, state_gdn[:, j, 0], None,
                                       st_gdn, j, geo, False)
            u, st_gdn = _gdn_scan(proj, pc, abp, None, dtb, alog, nw, state_gdn[:, j, 1], yf,
                                  st_gdn, j, geo, True)
            w_out = wb['gdn_out']
        x = _out_proj(u, w_out, j, x, mod, 2, geo)

        proj, _ = _in_proj(x, norm_ffn_w[i], mod, 3, wb['ffn_in'], i, ffn_in_w.shape[2], None, geo)
        x = _out_proj(proj, wb['ffn_out'], i, x, mod, 5, geo,
                      final_w=final_norm_w if i == depth - 1 else None,
                      conv=(ffn_conv_w[i].reshape(9, -1), ffn_conv_b[i]))

    y_ctx, y_lat = x
    return (y_ctx.reshape(Bc, SEQ, D), y_lat.reshape(Bs, LS, D), st_ssm, st_gla, st_gdn)
```

```python
import functools
from typing import NamedTuple

import numpy as np
import jax
import jax.numpy as jnp
from jax import lax
from jax.experimental import pallas as pl
from jax.experimental.pallas import tpu as pltpu

f32, bf16 = jnp.float32, jnp.bfloat16

BLK = 256
CHUNK = 64
SSD_CHUNK = 128
HALO = 16
GRID_W = 64
NORM_EPS = 1e-6
SMALL = 128
GLA_RANK = 16
GLA_NORMALIZER = 16.0
SSM_GROUPS = 8
SSM_HEADDIM = 64
SSM_STATE = 128
SSD_GP = 8
GDN_HP = 8
VMEM_LIMIT = 48 * 1024 * 1024

NN = (((1,), (0,)), ((), ()))
NT = (((1,), (1,)), ((), ()))
TN = (((0,), (0,)), ((), ()))


class Geo(NamedTuple):
    D: int
    Tc: int
    Ts: int
    SEQ: int
    LS: int
    Bc: int
    Bs: int

    @property
    def T(self):
        return self.Tc + self.Ts

    @property
    def nb(self):
        return self.T // BLK

    @property
    def ncb(self):
        return self.Tc // BLK

    @property
    def bpc(self):
        return self.SEQ // BLK

    @property
    def bps(self):
        return self.LS // BLK


def _mm(a, b, dims=NN):
    return lax.dot_general(a, b, dims, preferred_element_type=f32)


def _split3(a):
    hi = a.astype(bf16)
    r1 = a - hi.astype(f32)
    mid = r1.astype(bf16)
    lo = (r1 - mid.astype(f32)).astype(bf16)
    return hi, mid, lo


def _mm_x(a, b, dims=NN):
    if isinstance(a, tuple):
        ps = [_mm(p, b, dims) for p in a]
    elif a.dtype == bf16:
        ps = [_mm(a, p, dims) for p in _split3(b)]
    else:
        ps = [_mm(p, b, dims) for p in _split3(a)]
    return ps[0] + ps[1] + ps[2]


def _mm_hi(a, b):
    ah = a.astype(bf16)
    al = (a - ah.astype(f32)).astype(bf16)
    bh = b.astype(bf16)
    bl = (b - bh.astype(f32)).astype(bf16)
    return _mm(ah, bh) + _mm(al, bh) + _mm(ah, bl)


def _silu(x):
    h = 0.5 * x
    return h + h * jnp.tanh(h)


def _tri_mask(n, upper, chunk=None, strict=False):
    r = lax.broadcasted_iota(jnp.int32, (n, n), 0)
    c = lax.broadcasted_iota(jnp.int32, (n, n), 1)
    if upper:
        m = (c > r) if strict else (c >= r)
    else:
        m = (c < r) if strict else (c <= r)
    if chunk is not None and chunk < n:
        sh = int(np.log2(chunk))
        same = lax.shift_right_logical(r, sh) == lax.shift_right_logical(c, sh)
        m = jnp.logical_and(m, same)
    return m


def _as_bf16(mask):
    return jnp.where(mask, 1.0, 0.0).astype(bf16)


def _ada_kernel(c_ref, w_ref, b_ref, o_ref):
    o_ref[0] = _mm_hi(_silu(c_ref[...]), w_ref[0]) + b_ref[0]


def _ada_table(cond8, ada_w, ada_b):
    depth, D, N6 = ada_w.shape
    tn = N6 // 4
    return pl.pallas_call(
        _ada_kernel,
        out_shape=jax.ShapeDtypeStruct((depth, 8, N6), f32),
        grid=(depth, N6 // tn),
        in_specs=[pl.BlockSpec((8, D), lambda l, j: (0, 0)),
                  pl.BlockSpec((1, D, tn), lambda l, j: (l, 0, j)),
                  pl.BlockSpec((1, 1, tn), lambda l, j: (l, 0, j))],
        out_specs=pl.BlockSpec((1, 8, tn), lambda l, j: (l, 0, j)),
        compiler_params=pltpu.CompilerParams(
            dimension_semantics=("parallel", "parallel"), vmem_limit_bytes=VMEM_LIMIT),
        name="ada_table",
    )(cond8, ada_w, ada_b.reshape(depth, 1, N6))


def _load_x(refs, n_ctx_tiles, pair):
    if not pair:
        return refs[0][...], refs[1:]
    return jnp.where(pl.program_id(0) < n_ctx_tiles, refs[0][...], refs[1][...]), refs[2:]


def _x_specs(x, tm, geo):
    if not isinstance(x, tuple):
        return [pl.BlockSpec((tm, geo.D), lambda i: (i, 0))], [x]
    nct = geo.Tc // tm
    return ([pl.BlockSpec((tm, geo.D), lambda i: (jnp.minimum(i, nct - 1), 0)),
             pl.BlockSpec((tm, geo.D), lambda i: (jnp.maximum(i - nct, 0), 0))], list(x))


def _in_proj_kernel(*refs, has_small, tn, dst_blk, n_ctx_tiles, pair):
    x, refs = _load_x(refs, n_ctx_tiles, pair)
    if has_small:
        nw_ref, sh_ref, sc_ref, w_ref, ws_ref, o_ref, os_ref = refs
    else:
        nw_ref, sh_ref, sc_ref, w_ref, o_ref = refs
    h = x * lax.rsqrt(jnp.mean(x * x, axis=-1, keepdims=True) + NORM_EPS) * nw_ref[...]
    h = h * (1.0 + sc_ref[0]) + sh_ref[0]
    hb = h.astype(bf16)
    if has_small:
        ws = ws_ref[...]
        wh = ws.astype(bf16)
        wl = (ws - wh.astype(f32)).astype(bf16)
        hl = (h - hb.astype(f32)).astype(bf16)
        os_ref[...] = _mm(hb, wh) + _mm(hl, wh) + _mm(hb, wl)
    per = tn // 128
    for n in range(w_ref.shape[2] // tn):
        r = _mm(hb, w_ref[0, :, n * tn:(n + 1) * tn]).astype(o_ref.dtype)
        if dst_blk is None:
            o_ref[:, n * tn:(n + 1) * tn] = r
            continue
        k = 0
        while k < per:
            d0 = dst_blk[n * per + k]
            run = 1
            while k + run < per and dst_blk[n * per + k + run] == d0 + run:
                run += 1
            o_ref[:, d0 * 128:(d0 + run) * 128] = r[:, k * 128:(k + run) * 128]
            k += run


def _row_tile(geo, pref):
    for tm in (pref, 512, 256):
        if tm <= pref and geo.Tc % tm == 0 and geo.LS % tm == 0:
            return tm
    raise ValueError("token counts must be multiples of 256")


def _cond_of_tile(i, geo, tm):
    nct = geo.Tc // tm
    return jnp.where(i < nct, 0, 1 + jnp.maximum(i - nct, 0) // (geo.LS // tm))


def _in_proj(x, norm_w, mod, shift_idx, w_stack, layer, N, w_small, geo, tn=512, dst_blk=None):
    T, D = geo.T, geo.D
    tm = _row_tile(geo, 512)
    assert N % tn == 0 and w_stack.dtype == bf16
    x_specs, x_args = _x_specs(x, tm, geo)
    has_small = w_small is not None
    cond = functools.partial(_cond_of_tile, geo=geo, tm=tm)
    in_specs = [*x_specs,
                pl.BlockSpec((1, D), lambda i: (0, 0)),
                pl.BlockSpec((1, 1, D), lambda i: (cond(i), 0, shift_idx)),
                pl.BlockSpec((1, 1, D), lambda i: (cond(i), 0, shift_idx + 1)),
                pl.BlockSpec((1, D, N), lambda i: (layer, 0, 0), pipeline_mode=pl.Buffered(1))]
    args = [*x_args, norm_w.reshape(1, D), mod, mod, w_stack]
    out_shape = [jax.ShapeDtypeStruct((T, N), bf16)]
    out_specs = [pl.BlockSpec((tm, N), lambda i: (i, 0))]
    if has_small:
        in_specs.append(pl.BlockSpec((D, SMALL), lambda i: (0, 0)))
        args.append(w_small)
        out_shape.append(jax.ShapeDtypeStruct((T, SMALL), f32))
        out_specs.append(pl.BlockSpec((tm, SMALL), lambda i: (i, 0)))
    res = pl.pallas_call(
        functools.partial(_in_proj_kernel, has_small=has_small, tn=tn, dst_blk=dst_blk,
                          n_ctx_tiles=geo.Tc // tm, pair=isinstance(x, tuple)),
        out_shape=out_shape,
        grid=(T // tm,),
        in_specs=in_specs,
        out_specs=out_specs,
        compiler_params=pltpu.CompilerParams(
            dimension_semantics=("parallel",), vmem_limit_bytes=VMEM_LIMIT),
        name="in_proj",
    )(*args)
    return res if has_small else (res[0], None)


CONV_HALO = 128
CONV_TF = 256


def _conv_glu_matmul(a_ref, ap_ref, an_ref, v_ref, cw_ref, cb_ref, w_ref, pad_ref, u_ref, acc_ref,
                     *, n_ctx_tiles, tiles_per_img, SEQ):
    i = pl.program_id(0)
    tm, F = a_ref.shape
    H = CONV_HALO
    pos_img = lax.rem(jnp.maximum(i - n_ctx_tiles, 0), tiles_per_img)
    top_ok = jnp.logical_and(i >= n_ctx_tiles, pos_img > 0)
    bot_ok = jnp.logical_and(i >= n_ctx_tiles, pos_img < tiles_per_img - 1)
    pad_ref[0:H, :] = jnp.where(top_ok, ap_ref[...].astype(f32), 0.0)
    pad_ref[H + tm:H + tm + H, :] = jnp.where(bot_ok, an_ref[...].astype(f32), 0.0)
    pad_ref[H:H + tm, :] = a_ref[...].astype(f32)
    tf = CONV_TF
    row = lax.broadcasted_iota(jnp.int32, (BLK, tf), 0)

    def run(grid):
        acc = None
        for n in range(F // tf):
            cs = slice(n * tf, (n + 1) * tf)
            w = cw_ref[:, cs]
            b = cb_ref[:, cs]

            def taps(s, pos, width, kr):
                left = jnp.where(pos > 0, pad_ref[s - 1:s - 1 + BLK, cs], 0.0)
                right = jnp.where(pos < width - 1, pad_ref[s + 1:s + 1 + BLK, cs], 0.0)
                return (left * w[3 * kr:3 * kr + 1] + pad_ref[s:s + BLK, cs] * w[3 * kr + 1:3 * kr + 2]
                        + right * w[3 * kr + 2:3 * kr + 3])

            for c in range(tm // BLK):
                s = H + c * BLK
                if grid:
                    col = jnp.bitwise_and(row, GRID_W - 1)
                    conv = (taps(s - GRID_W, col, GRID_W, 0) + taps(s, col, GRID_W, 1)
                            + taps(s + GRID_W, col, GRID_W, 2))
                else:
                    pos = jnp.bitwise_and(row + (c * BLK) % SEQ, SEQ - 1)
                    conv = taps(s, pos, SEQ, 1)
                v = v_ref[c * BLK:(c + 1) * BLK, cs].astype(f32)
                u_ref[c * BLK:(c + 1) * BLK, cs] = (_silu(conv + b) * v).astype(bf16)
            part = _mm(u_ref[:, cs], w_ref[0, cs, :])
            acc = part if acc is None else acc + part
        acc_ref[...] = acc

    @pl.when(i < n_ctx_tiles)
    def _():
        run(False)

    @pl.when(i >= n_ctx_tiles)
    def _():
        run(True)

    return acc_ref[...]


def _out_proj_kernel(*refs, n_ctx_tiles, final, conv, pair):
    if conv is not None:
        a_ref, ap_ref, an_ref, v_ref, cw_ref, cb_ref = refs[:6]
        refs = refs[6:]
    else:
        u_ref = refs[0]
        refs = refs[1:]
    w_ref = refs[0]
    x, refs = _load_x(refs[1:], n_ctx_tiles, pair)
    g_ref = refs[0]
    refs = refs[1:]
    if final:
        fw_ref, oc_ref, os_ref = refs[:3]
        scratch = refs[3:]
    else:
        o_ref = refs[0]
        scratch = refs[1:]
    if conv is not None:
        acc = _conv_glu_matmul(a_ref, ap_ref, an_ref, v_ref, cw_ref, cb_ref, w_ref, *scratch,
                               n_ctx_tiles=n_ctx_tiles, **conv)
    else:
        acc = _mm(u_ref[...], w_ref[0])
    r = x + g_ref[0] * acc
    if not final:
        o_ref[...] = r
        return
    r = r * lax.rsqrt(jnp.mean(r * r, axis=-1, keepdims=True) + NORM_EPS) * fw_ref[...]

    @pl.when(pl.program_id(0) < n_ctx_tiles)
    def _():
        oc_ref[...] = r

    @pl.when(pl.program_id(0) >= n_ctx_tiles)
    def _():
        os_ref[...] = r


def _out_proj(u, w_stack, layer, x, mod, gate_idx, geo, final_w=None, conv=None):
    T = u.shape[0]
    K, D = w_stack.shape[1:]
    tm = _row_tile(geo, 512)
    cond = functools.partial(_cond_of_tile, geo=geo, tm=tm)
    final = final_w is not None
    nct = geo.Tc // tm
    if conv is None:
        in_specs = [pl.BlockSpec((tm, K), lambda i: (i, 0))]
        args = [u]
        scratch, conv_cfg = [], None
    else:
        w9, cb = conv
        H = CONV_HALO
        per, lastb = tm // H, T // H - 1
        assert geo.LS % tm == 0 and tm % geo.SEQ == 0 and geo.SEQ & (geo.SEQ - 1) == 0 and K % CONV_TF == 0
        assert BLK % GRID_W == 0 and H > GRID_W
        in_specs = [pl.BlockSpec((tm, K), lambda i: (i, 0)),
                    pl.BlockSpec((H, K), lambda i: (jnp.maximum(i * per - 1, 0), 0)),
                    pl.BlockSpec((H, K), lambda i: (jnp.minimum((i + 1) * per, lastb), 0)),
                    pl.BlockSpec((tm, K), lambda i: (i, 1)),
                    pl.BlockSpec((9, K), lambda i: (0, 0)),
                    pl.BlockSpec((1, K), lambda i: (0, 0))]
        args = [u, u, u, u, w9, cb.reshape(1, K)]
        scratch = [pltpu.VMEM((tm + 2 * H, K), f32), pltpu.VMEM((tm, K), bf16), pltpu.VMEM((tm, D), f32)]
        conv_cfg = dict(tiles_per_img=geo.LS // tm, SEQ=geo.SEQ)
    x_specs, x_args = _x_specs(x, tm, geo)
    in_specs += [pl.BlockSpec((1, K, D), lambda i: (layer, 0, 0), pipeline_mode=pl.Buffered(1)),
                 *x_specs,
                 pl.BlockSpec((1, 1, D), lambda i: (cond(i), 0, gate_idx))]
    args += [w_stack, *x_args, mod]
    if final:
        in_specs.append(pl.BlockSpec((1, D), lambda i: (0, 0)))
        args.append(final_w.reshape(1, D))
        out_shape = [jax.ShapeDtypeStruct((geo.Tc, D), f32), jax.ShapeDtypeStruct((geo.Ts, D), f32)]
        out_specs = [pl.BlockSpec((tm, D), lambda i: (jnp.minimum(i, nct - 1), 0)),
                     pl.BlockSpec((tm, D), lambda i: (jnp.maximum(i - nct, 0), 0))]
    else:
        out_shape = jax.ShapeDtypeStruct((T, D), f32)
        out_specs = pl.BlockSpec((tm, D), lambda i: (i, 0))
    return pl.pallas_call(
        functools.partial(_out_proj_kernel, n_ctx_tiles=nct, final=final, conv=conv_cfg,
                          pair=isinstance(x, tuple)),
        out_shape=out_shape,
        grid=(T // tm,),
        in_specs=in_specs,
        out_specs=out_specs,
        scratch_shapes=scratch,
        compiler_params=pltpu.CompilerParams(
            dimension_semantics=("parallel" if not (final or isinstance(x, tuple)) else "arbitrary",),
            vmem_limit_bytes=56 * 1024 * 1024),
        name="ffn_out" if conv is not None else "out_proj",
    )(*args)


def _blk_idx(i, geo, rev):
    return (geo.nb - 1 - i) if rev else i


def _tok_spec(cw, colfn, geo, rev):
    return pl.BlockSpec((BLK, cw), lambda u, i: (_blk_idx(i, geo, rev), colfn(u)))


def _halo_specs(cw, colfn, geo, rev):
    per = BLK // HALO
    last = geo.T // HALO - 1
    prev = pl.BlockSpec((HALO, cw), lambda u, i: (jnp.maximum(_blk_idx(i, geo, rev) * per - 1, 0), colfn(u)))
    nxt = pl.BlockSpec((HALO, cw), lambda u, i: (jnp.minimum((_blk_idx(i, geo, rev) + 1) * per, last), colfn(u)))
    return prev, nxt


def _h0_spec(geo, rev, hp):
    def idx(u, i):
        ip = _blk_idx(i, geo, rev)
        return (jnp.maximum(ip - geo.ncb, 0) // geo.bps, u, 0, 0)
    return pl.BlockSpec((1, hp, 128, 256), idx)


def _st_spec(geo, rev, layer, blk):
    d = 1 if rev else 0

    def idx(u, i):
        ip = _blk_idx(i, geo, rev)
        return (jnp.minimum(ip // geo.bpc, geo.Bc - 1), layer, d, u, 0, 0)
    return pl.BlockSpec((1, 1, 1) + blk, idx)


def _param_spec(rows, cw, colfn):
    return pl.BlockSpec((rows, cw), lambda u, i: (0, colfn(u)))


def _flags(geo, rev):
    ip = _blk_idx(pl.program_id(1), geo, rev)
    is_ctx = ip < geo.ncb
    pos = jnp.where(is_ctx, lax.rem(ip, geo.bpc), lax.rem(jnp.maximum(ip - geo.ncb, 0), geo.bps))
    n = jnp.where(is_ctx, geo.bpc, geo.bps)
    first = pos == 0
    last = pos == n - 1
    return is_ctx, first, last, (last if rev else first)


def _conv3_silu(cur_ref, prev_ref, next_ref, w_ref, b_ref, first, last):
    cur = cur_ref[...].astype(f32)
    n = cur.shape[0]
    row = lax.broadcasted_iota(jnp.int32, cur.shape, 0)
    pr = jnp.where(first, 0.0, prev_ref[HALO - 1:HALO, :].astype(f32))
    nx = jnp.where(last, 0.0, next_ref[0:1, :].astype(f32))
    xm = jnp.where(row == 0, pr, pltpu.roll(cur, 1, axis=0))
    xp = jnp.where(row == n - 1, nx, pltpu.roll(cur, n - 1, axis=0))
    w = w_ref[...]
    y = xm * w[0:1] + cur * w[1:2] + xp * w[2:3]
    if b_ref is not None:
        y = y + b_ref[...]
    return _silu(y)


def _init_state(S, h0_ref, is_ctx, start):
    @pl.when(start)
    def _():
        S[...] = jnp.where(is_ctx, 0.0, h0_ref[0])


def _norm_gate(o, gate, nw, gate_first):
    g = _silu(gate)
    if gate_first:
        o = o * g
    o = o * lax.rsqrt(jnp.mean(o * o, axis=-1, keepdims=True) + NORM_EPS) * nw
    if not gate_first:
        o = o * g
    return o


def _scan_call(kernel, name, grid_u, in_specs, args, outs, st, st_blk, layer, geo, rev, hp):
    out_shape = [jax.ShapeDtypeStruct((geo.T, c), dt) for c, dt, _ in outs]
    out_specs = [_tok_spec(bc, lambda u: u, geo, rev) for _, _, bc in outs]
    return pl.pallas_call(
        kernel,
        out_shape=out_shape + [jax.ShapeDtypeStruct(st.shape, st.dtype)],
        grid=(grid_u, geo.nb),
        in_specs=in_specs + [pl.BlockSpec(memory_space=pl.ANY)],
        out_specs=out_specs + [_st_spec(geo, rev, layer, st_blk)],
        scratch_shapes=[pltpu.VMEM((hp, 128, 256), f32)],
        input_output_aliases={len(args): len(outs)},
        compiler_params=pltpu.CompilerParams(
            dimension_semantics=("parallel", "arbitrary"), vmem_limit_bytes=VMEM_LIMIT),
        name=name,
    )(*args, st)


def _ssd_kernel(*refs, rev, geo, gp):
    skip = 1
    if rev:
        (pc_ref, dt_ref, dtb_ref, alog_ref, xsel_ref, xh_ref, h0_ref,
         dsk_ref, z_ref, nw_ref, yf_ref) = refs[:11]
        y_ref, st_ref, S = refs[11 + skip:]
    else:
        (xbc_c, xbc_p, xbc_n, dt_ref, cw_ref, cb_ref, dtb_ref, alog_ref, xsel_ref, xh_ref, h0_ref) = refs[:11]
        y_ref, pc_ref, st_ref, S = refs[11 + skip:]
    is_ctx, first, last, start = _flags(geo, rev)
    _init_state(S, h0_ref, is_ctx, start)
    Q = SSD_CHUNK
    nch = BLK // Q
    groups = range(gp)

    if rev:
        xbc = pc_ref[...].astype(f32)
    else:
        xbc = _conv3_silu(xbc_c, xbc_p, xbc_n, cw_ref, cb_ref, first, last)
        pc_ref[...] = xbc.astype(bf16)
    dt = jax.nn.softplus(dt_ref[...] + dtb_ref[...])
    la = dt * (-jnp.exp(alog_ref[...]))
    tri = _tri_mask(BLK, rev, chunk=Q)
    cs3 = _split3(_mm_x(_as_bf16(tri), la))
    dtb16 = dt.astype(bf16)
    tri_q = _tri_mask(Q, rev)

    xs, xvs, bms, cms, cses, csts = [], [], [], [], [], []
    for g in groups:
        x = xbc[:, g * 512:g * 512 + 256]
        xs.append(x)
        bms.append(xbc[:, g * 512 + 256:g * 512 + 384].astype(bf16))
        cms.append(xbc[:, g * 512 + 384:g * 512 + 512].astype(bf16))
        xsel = xsel_ref[g]
        xvs.append(x * _mm(dtb16, xsel))
        cses.append(_mm_x(cs3, xsel))
        csts.append(_mm_x(cs3, xh_ref[g]).T)
    Svs = [S[g] for g in groups]
    youts = [[None] * nch for _ in groups]
    for c in (range(nch - 1, -1, -1) if rev else range(nch)):
        lo = c * Q
        li = lo if rev else lo + Q - 1
        for g in groups:
            cs_e = cses[g][lo:lo + Q]
            xv = xvs[g][lo:lo + Q]
            xvb = xv.astype(bf16)
            cmb = cms[g][lo:lo + Q]
            bmb = bms[g][lo:lo + Q]
            gm = _mm(cmb, bmb, NT)
            ys = []
            for r in range(4):
                seg = cs_e[:, r * 64:r * 64 + 1] - csts[g][r:r + 1, lo:lo + Q]
                m = jnp.where(tri_q, gm * jnp.exp(seg), 0.0)
                ys.append(_mm(m.astype(bf16), xvb[:, r * 64:(r + 1) * 64]))
            Sv = Svs[g]
            y = jnp.concatenate(ys, axis=1) + _mm(cmb, Sv.astype(bf16)) * jnp.exp(cs_e)
            cl = cses[g][li:li + 1, :]
            xw = xv * jnp.exp(cl - cs_e)
            Svs[g] = Sv * jnp.exp(cl) + _mm(bmb, xw.astype(bf16), TN)
            youts[g][c] = y

    for g in groups:
        S[g] = Svs[g]
        y = jnp.concatenate(youts[g], axis=0)
        if rev:
            cols = slice(g * 256, (g + 1) * 256)
            o = yf_ref[:, cols] + y + xs[g] * dsk_ref[:, cols]
            y = _norm_gate(o, z_ref[:, cols].astype(f32), nw_ref[:, cols], gate_first=True)
        y_ref[:, g * 256:(g + 1) * 256] = y.astype(y_ref.dtype)

    @pl.when(is_ctx)
    def _():
        for g in groups:
            for r in range(4):
                st_ref[0, 0, 0, g * 4 + r] = Svs[g][:, r * 64:(r + 1) * 64]


def _ssd_scan(proj, pc, dtp, conv_w, conv_b, dtb, alog, dskip, norm_w, h0, yf, st, layer, geo, rev):
    G, gp = SSM_GROUPS, SSD_GP
    nu = G // gp
    d = 1 if rev else 0
    xsel = np.zeros((G, 128, 256), np.float32)
    xh = np.zeros((G, 128, 128), np.float32)
    for g in range(G):
        for r in range(4):
            xsel[g, d * 32 + 4 * g + r, r * 64:(r + 1) * 64] = 1.0
            xh[g, d * 32 + 4 * g + r, r] = 1.0
    cw, zw = 512 * gp, 256 * gp
    head_specs = [_param_spec(1, SMALL, lambda u: 0), _param_spec(1, SMALL, lambda u: 0),
                  pl.BlockSpec((gp, 128, 256), lambda u, i: (u, 0, 0)),
                  pl.BlockSpec((gp, 128, 128), lambda u, i: (u, 0, 0)),
                  _h0_spec(geo, rev, gp)]
    head_args = [dtb, alog, jnp.asarray(xsel, bf16), jnp.asarray(xh, bf16), h0]
    if rev:
        in_specs = [_tok_spec(cw, lambda u: u, geo, rev), _tok_spec(SMALL, lambda u: 0, geo, rev), *head_specs,
                    _param_spec(1, zw, lambda u: u), _tok_spec(zw, lambda u: 2 * nu + u, geo, rev),
                    _param_spec(1, zw, lambda u: u), _tok_spec(zw, lambda u: u, geo, rev)]
        args = [pc, dtp, *head_args, dskip, proj, norm_w, yf]
        outs = [(2048, bf16, zw)]
    else:
        in_specs = [_tok_spec(cw, lambda u: u, geo, rev), *_halo_specs(cw, lambda u: u, geo, rev),
                    _tok_spec(SMALL, lambda u: 0, geo, rev),
                    _param_spec(3, cw, lambda u: u), _param_spec(1, cw, lambda u: u), *head_specs]
        args = [proj, proj, proj, dtp, conv_w, conv_b, *head_args]
        outs = [(2048, f32, zw), (4096, bf16, cw)]
    kern = functools.partial(_ssd_kernel, rev=rev, geo=geo, gp=gp)
    return _scan_call(kern, "ssd_bwd" if rev else "ssd_fwd",
                      nu, in_specs, args, outs, st, (4 * gp, SSM_STATE, SSM_HEADDIM), layer, geo, rev, gp)


def _gla_kernel(*refs, rev, geo, nh):
    qkv_ref, lr_ref, w2_ref, gb_ref, h0_ref = refs[:5]
    skip = 1
    if rev:
        r_ref, nw_ref, yf_ref = refs[5:8]
        y_ref, st_ref, S = refs[8 + skip:]
    else:
        y_ref, st_ref, S = refs[5 + skip:]
    is_ctx, first, last, start = _flags(geo, rev)
    _init_state(S, h0_ref, is_ctx, start)
    d = 1 if rev else 0
    dk, dv = 128, 256
    heads = range(nh)
    K = nh * dk

    q_all = qkv_ref[:, 0:K].astype(f32) * dk ** -0.5
    k_all = qkv_ref[:, K:2 * K].astype(f32)
    vb_all = qkv_ref[:, 2 * K:2 * K + nh * dv]
    lr = lr_ref[...][:, d * GLA_RANK:(d + 1) * GLA_RANK]
    logit = _mm_hi(lr, w2_ref[0]) + gb_ref[0]
    lg = jax.nn.log_sigmoid(logit) * (1.0 / GLA_NORMALIZER)
    tri = _tri_mask(BLK, rev, chunk=CHUNK)
    cs = _mm_x(_as_bf16(tri), lg)
    cs_t = _mm_x(lg, _as_bf16(_tri_mask(BLK, not rev, chunk=CHUNK)), TN)
    qgb = (q_all * jnp.exp(cs)).astype(bf16)
    kgb = (k_all * jnp.exp(-cs)).astype(bf16)

    ys = []
    for h in heads:
        sc = jnp.where(tri, _mm(qgb[:, h * dk:(h + 1) * dk], kgb[:, h * dk:(h + 1) * dk], NT), 0.0)
        ys.append(_mm(sc.astype(bf16), vb_all[:, h * dv:(h + 1) * dv]))
    Svs = [S[h] for h in heads]
    nch = BLK // CHUNK
    yin = [[None] * nch for _ in heads]
    for c in (range(nch - 1, -1, -1) if rev else range(nch)):
        lo = c * CHUNK
        li = lo if rev else lo + CHUNK - 1
        kw_all = (k_all[lo:lo + CHUNK] * jnp.exp(cs[li:li + 1, :] - cs[lo:lo + CHUNK])).astype(bf16)
        for h in heads:
            yin[h][c] = _mm(qgb[lo:lo + CHUNK, h * dk:(h + 1) * dk], Svs[h].astype(bf16))
        for h in heads:
            Svs[h] = (Svs[h] * jnp.exp(cs_t[h * dk:(h + 1) * dk, li:li + 1])
                      + _mm(kw_all[:, h * dk:(h + 1) * dk], vb_all[lo:lo + CHUNK, h * dv:(h + 1) * dv], TN))

    for h in heads:
        S[h] = Svs[h]
        y = ys[h] + jnp.concatenate(yin[h], axis=0)
        cols = slice(h * dv, (h + 1) * dv)
        if rev:
            y = _norm_gate(yf_ref[:, cols] + y, r_ref[:, cols].astype(f32), nw_ref[...], gate_first=False)
        y_ref[:, cols] = y.astype(y_ref.dtype)

    @pl.when(is_ctx)
    def _():
        for h in heads:
            st_ref[0, 0, 0, h] = Svs[h]


def _gla_scan(proj, lrp, w2, gb, norm_w, h0, yf, st, layer, geo, rev):
    H = 4
    d = 1 if rev else 0
    in_specs = [_tok_spec(2048, lambda u: 0, geo, rev), _tok_spec(SMALL, lambda u: 0, geo, rev),
                pl.BlockSpec((1, GLA_RANK, 512), lambda u, i: (d, 0, 0)),
                pl.BlockSpec((1, 1, 512), lambda u, i: (d, 0, 0)),
                _h0_spec(geo, rev, H)]
    args = [proj, lrp, w2, gb, h0]
    if rev:
        in_specs += [_tok_spec(1024, lambda u: 2, geo, rev), _param_spec(1, 256, lambda u: 0),
                     _tok_spec(1024, lambda u: 0, geo, rev)]
        args += [proj, norm_w, yf]
    kern = functools.partial(_gla_kernel, rev=rev, geo=geo, nh=H)
    return _scan_call(kern, "gla_bwd" if rev else "gla_fwd",
                      1, in_specs, args, [(1024, bf16 if rev else f32, 1024)], st, (H, 128, 256), layer, geo, rev, H)


def _unit_tri_inverse(ms, lev):
    eye = jnp.where(lev == -1, 1.0, 0.0)
    dinv = [eye - jnp.where(lev == 0, m, 0.0) for m in ms]
    for b in range(1, 6):
        ts = [_mm(jnp.where(lev == b, m, 0.0).astype(bf16), d.astype(bf16)).astype(bf16)
              for m, d in zip(ms, dinv)]
        dinv = [d - _mm(d.astype(bf16), t) for d, t in zip(dinv, ts)]
    return dinv


def _gdn_kernel(*refs, rev, geo, hp):
    skip = 1
    if rev:
        (pc_ref, ab_ref, dtb_ref, alog_ref, xab_ref, lev_ref, h0_ref, z_ref, nw_ref, yf_ref) = refs[:10]
        y_ref, st_ref, S = refs[10 + skip:]
    else:
        (qkv_c, qkv_p, qkv_n, cw_ref, ab_ref, dtb_ref, alog_ref, xab_ref, lev_ref, h0_ref) = refs[:10]
        y_ref, pc_ref, st_ref, S = refs[10 + skip:]
    is_ctx, first, last, start = _flags(geo, rev)
    _init_state(S, h0_ref, is_ctx, start)
    dk, dv = 128, 256
    heads = range(hp)

    if rev:
        qkv = pc_ref[...].astype(f32)
    else:
        qkv = _conv3_silu(qkv_c, qkv_p, qkv_n, cw_ref, None, first, last)

    ab = ab_ref[...]
    gall = -jnp.exp(alog_ref[...]) * jax.nn.softplus(ab + dtb_ref[...])
    lane = lax.broadcasted_iota(jnp.int32, ab.shape, 1)
    sel = _mm_x(jnp.where(lane < 16, gall, jax.nn.sigmoid(ab)), xab_ref[0])
    lev = lev_ref[...]
    r_i = lax.broadcasted_iota(jnp.int32, (BLK, BLK), 0)
    c_i = lax.broadcasted_iota(jnp.int32, (BLK, BLK), 1)
    in_chunk = lev < 6
    incl = jnp.logical_and(in_chunk, (c_i >= r_i) if rev else (c_i <= r_i))
    incl_t = jnp.logical_and(in_chunk, (c_i <= r_i) if rev else (c_i >= r_i))
    strict = jnp.logical_and(incl, lev >= 0)
    cs_all = _mm_x(_as_bf16(incl), sel)
    cs_rows = _mm_x(sel, _as_bf16(incl_t), TN)

    qs, ks, kbfs, kbs, vbs, css, lmasks, ms = [], [], [], [], [], [], [], []
    for hh in heads:
        q = qkv[:, hh * dk:(hh + 1) * dk]
        k = qkv[:, (hp + hh) * dk:(hp + hh + 1) * dk]
        v = qkv[:, 2 * hp * dk + hh * dv:2 * hp * dk + (hh + 1) * dv]
        if not rev:
            q = q * lax.rsqrt(jnp.sum(q * q, axis=-1, keepdims=True) + 1e-6) * dk ** -0.5
            k = k * lax.rsqrt(jnp.sum(k * k, axis=-1, keepdims=True) + 1e-6)
            pc_ref[:, hh * dk:(hh + 1) * dk] = q.astype(bf16)
            pc_ref[:, (hp + hh) * dk:(hp + hh + 1) * dk] = k.astype(bf16)
            pc_ref[:, 2 * hp * dk + hh * dv:2 * hp * dk + (hh + 1) * dv] = v.astype(bf16)
        beta = sel[:, 2 * hh + 1:2 * hh + 2]
        cs = cs_all[:, 2 * hh:2 * hh + 1]
        lmask = jnp.where(incl, jnp.exp(cs - cs_rows[2 * hh:2 * hh + 1, :]), 0.0)
        kb = k * beta
        kbf = k.astype(bf16)
        qs.append(q)
        ks.append(k)
        kbfs.append(kbf)
        kbs.append(kb)
        vbs.append((v * beta).astype(bf16))
        css.append(cs)
        lmasks.append(lmask)
        ms.append(jnp.where(strict, _mm(kb.astype(bf16), kbf, NT) * lmask, 0.0))
    tbs = [t.astype(bf16) for t in _unit_tri_inverse(ms, lev)]
    ecs = [jnp.exp(cs) for cs in css]
    us = [_mm(tbs[h], vbs[h]) for h in heads]
    wks = [_mm(tbs[h], (kbs[h] * ecs[h]).astype(bf16)).astype(bf16) for h in heads]
    a_ins = [(_mm(qs[h].astype(bf16), kbfs[h], NT) * lmasks[h]).astype(bf16) for h in heads]
    qds = [(qs[h] * ecs[h]).astype(bf16) for h in heads]

    Svs = [S[h] for h in heads]
    nch = BLK // CHUNK
    outs = [[None] * nch for _ in heads]
    for c in (range(nch - 1, -1, -1) if rev else range(nch)):
        lo = c * CHUNK
        li = lo if rev else lo + CHUNK - 1
        Sbs = [Sv.astype(bf16) for Sv in Svs]
        v_news = [(us[h][lo:lo + CHUNK] - _mm(wks[h][lo:lo + CHUNK], Sbs[h])).astype(bf16) for h in heads]
        for h in heads:
            outs[h][c] = (_mm(qds[h][lo:lo + CHUNK], Sbs[h])
                          + _mm(a_ins[h][lo:lo + CHUNK, lo:lo + CHUNK], v_news[h]))
        for h in heads:
            cl = css[h][li:li + 1, :]
            kd = ks[h][lo:lo + CHUNK] * jnp.exp(cl - css[h][lo:lo + CHUNK])
            Svs[h] = Svs[h] * jnp.exp(cl) + _mm(kd.astype(bf16), v_news[h], TN)

    for hh in heads:
        S[hh] = Svs[hh]
        o = jnp.concatenate(outs[hh], axis=0)
        cols = slice(hh * dv, (hh + 1) * dv)
        if rev:
            o = _norm_gate(yf_ref[:, cols] + o, z_ref[:, cols].astype(f32), nw_ref[...], gate_first=False)
        y_ref[:, cols] = o.astype(y_ref.dtype)

    @pl.when(is_ctx)
    def _():
        for hh in heads:
            st_ref[0, 0, 0, hh] = Svs[hh]


def _gdn_scan(proj, pc, abp, conv_w, dtb, alog, norm_w, h0, yf, st, layer, geo, rev):
    H, hp = 8, GDN_HP
    nu = H // hp
    d = 1 if rev else 0
    xab = np.zeros((nu, 128, 128), np.float32)
    for h in range(H):
        xab[h // hp, d * H + h, 2 * (h % hp)] = 1.0
        xab[h // hp, 2 * H + d * H + h, 2 * (h % hp) + 1] = 1.0
    idx = np.arange(BLK)
    x = idx[:, None] ^ idx[None, :]
    lev = np.where(x == 0, -1, np.floor(np.log2(np.maximum(x, 1)))).astype(np.int32)
    cw, vw = 512 * hp, 256 * hp
    head_specs = [_tok_spec(SMALL, lambda u: 0, geo, rev),
                  _param_spec(1, SMALL, lambda u: 0), _param_spec(1, SMALL, lambda u: 0),
                  pl.BlockSpec((1, 128, 128), lambda u, i: (u, 0, 0)),
                  pl.BlockSpec((BLK, BLK), lambda u, i: (0, 0)),
                  _h0_spec(geo, rev, hp)]
    head_args = [abp, dtb, alog, jnp.asarray(xab, bf16), jnp.asarray(lev), h0]
    if rev:
        in_specs = [_tok_spec(cw, lambda u: u, geo, rev), *head_specs,
                    _tok_spec(vw, lambda u: 2 * nu + u, geo, rev), _param_spec(1, 256, lambda u: 0),
                    _tok_spec(vw, lambda u: u, geo, rev)]
        args = [pc, *head_args, proj, norm_w, yf]
        outs = [(2048, bf16, vw)]
    else:
        in_specs = [_tok_spec(cw, lambda u: u, geo, rev), *_halo_specs(cw, lambda u: u, geo, rev),
                    _param_spec(3, cw, lambda u: u), *head_specs]
        args = [proj, proj, proj, conv_w, *head_args]
        outs = [(2048, f32, vw), (4096, bf16, cw)]
    kern = functools.partial(_gdn_kernel, rev=rev, geo=geo, hp=hp)
    return _scan_call(kern, "gdn_bwd" if rev else "gdn_fwd",
                      nu, in_specs, args, outs, st, (hp, 128, 256), layer, geo, rev, hp)


def _pad_cols(w, n=SMALL):
    return jnp.pad(w.astype(f32), ((0, 0), (0, n - w.shape[1])))


def _pad_row(v, n=SMALL):
    v = v.reshape(1, -1).astype(f32)
    return jnp.pad(v, ((0, 0), (0, n - v.shape[1])))


def _ssd_perm():
    perm = []
    for g in range(SSM_GROUPS):
        perm += list(range(g * 256, (g + 1) * 256))
        perm += list(range(2048 + g * 128, 2048 + (g + 1) * 128))
        perm += list(range(3072 + g * 128, 3072 + (g + 1) * 128))
    return np.asarray(perm, np.int32)


def _ssd_dst_blocks():
    dst = [32 + i for i in range(16)]
    dst += [4 * (i // 2) + i % 2 for i in range(16)]
    dst += [4 * g + 2 for g in range(SSM_GROUPS)]
    dst += [4 * g + 3 for g in range(SSM_GROUPS)]
    return tuple(dst)


def _gdn_dst_blocks(hp):
    inv = np.argsort(_gdn_perm(hp)[::128] // 128)
    return tuple(int(b) for b in inv) + tuple(range(32, 48))


def _gdn_perm(hp):
    perm = []
    for u in range(8 // hp):
        perm += list(range(u * hp * 128, (u + 1) * hp * 128))
        perm += list(range(1024 + u * hp * 128, 1024 + (u + 1) * hp * 128))
        perm += list(range(2048 + u * hp * 256, 2048 + (u + 1) * hp * 256))
    return np.asarray(perm, np.int32)


def kernel(x_prompt, x_sample, state_ssm, state_gla, state_gdn, c, c_ctx, norm_mix_w, norm_ffn_w, ada_w, ada_b, ffn_in_w, ffn_conv_w, ffn_conv_b, ffn_out_w, ssm_in_w, ssm_conv_w, ssm_conv_b, ssm_dt_bias, ssm_a_log, ssm_d, ssm_norm_w, ssm_out_w, gla_in_w, gla_gate_w1, gla_gate_w2, gla_gate_b, gla_norm_w, gla_out_w, gdn_in_w, gdn_conv_w, gdn_dt_bias, gdn_a_log, gdn_norm_w, gdn_out_w, final_norm_w):
    Bc, SEQ, D = x_prompt.shape
    Bs, LS, _ = x_sample.shape
    geo = Geo(D=D, Tc=Bc * SEQ, Ts=Bs * LS, SEQ=SEQ, LS=LS, Bc=Bc, Bs=Bs)
    assert SEQ % BLK == 0 and LS % BLK == 0 and Bs < 8
    depth = ada_w.shape[0]
    G, P, N = SSM_GROUPS, SSM_HEADDIM, SSM_STATE

    x = (x_prompt.reshape(geo.Tc, D), x_sample.reshape(geo.Ts, D))
    cond8 = jnp.zeros((8, D), f32).at[0].set(c_ctx).at[1:1 + Bs].set(c)
    mod_all = _ada_table(cond8, ada_w, ada_b)[:, :1 + Bs].reshape(depth, 1 + Bs, 1, 6 * D)

    n_ssm = ssm_out_w.shape[1] + 4096
    n_gdn = 6144
    wb = dict(ssm_in=ssm_in_w[:, :, :n_ssm].astype(bf16), ssm_out=ssm_out_w.astype(bf16),
              gla_in=gla_in_w.astype(bf16), gla_out=gla_out_w.astype(bf16),
              gdn_in=gdn_in_w[:, :, :n_gdn].astype(bf16), gdn_out=gdn_out_w.astype(bf16),
              ffn_in=ffn_in_w.astype(bf16), ffn_out=ffn_out_w.astype(bf16))
    st_ssm = jnp.zeros((Bc, ssm_in_w.shape[0], 2, G * 4, N, P), f32)
    st_gla = jnp.zeros((Bc, gla_in_w.shape[0], 2) + state_gla.shape[3:], f32)
    st_gdn = jnp.zeros((Bc, gdn_in_w.shape[0], 2) + state_gdn.shape[3:], f32)
    for i in range(depth):
        mod = mod_all[i]
        kind, j = i % 3, i // 3
        if kind == 0:
            di = ssm_out_w.shape[1]
            perm = _ssd_perm()
            proj, dtp = _in_proj(x, norm_mix_w[i], mod, 0, wb['ssm_in'], j, n_ssm,
                                 _pad_cols(ssm_in_w[j, :, n_ssm:]), geo, dst_blk=_ssd_dst_blocks())
            dtb, alog = _pad_row(ssm_dt_bias[j]), _pad_row(ssm_a_log[j])
            dskip = jnp.repeat(ssm_d[j], P).reshape(1, di)
            nw = ssm_norm_w[j].reshape(1, di)
            cw, cb = ssm_conv_w[j][:, perm], ssm_conv_b[j][perm].reshape(1, -1)
            h0 = state_ssm[:, j].reshape(Bs, 2, G, 4, N, P).transpose(1, 0, 2, 4, 3, 5).reshape(2, Bs, G, N, 4 * P)
            yf, pc, st_ssm = _ssd_scan(proj, None, dtp, cw, cb, dtb, alog, None, None, h0[0], None,
                                       st_ssm, j, geo, False)
            u, st_ssm = _ssd_scan(proj, pc, dtp, None, None, dtb, alog, dskip, nw, h0[1], yf,
                                  st_ssm, j, geo, True)
            w_out = wb['ssm_out']
        elif kind == 1:
            w_small = _pad_cols(jnp.concatenate([gla_gate_w1[j, 0], gla_gate_w1[j, 1]], axis=1))
            proj, lrp = _in_proj(x, norm_mix_w[i], mod, 0, wb['gla_in'], j, gla_in_w.shape[2], w_small, geo)
            w2 = gla_gate_w2[j]
            gb = gla_gate_b[j].reshape(2, 1, -1)
            nw = gla_norm_w[j].reshape(1, -1)
            yf, st_gla = _gla_scan(proj, lrp, w2, gb, None, state_gla[:, j, 0], None, st_gla, j, geo, False)
            u, st_gla = _gla_scan(proj, lrp, w2, gb, nw, state_gla[:, j, 1], yf, st_gla, j, geo, True)
            w_out = wb['gla_out']
        else:
            perm = _gdn_perm(GDN_HP)
            proj, abp = _in_proj(x, norm_mix_w[i], mod, 0, wb['gdn_in'], j, n_gdn,
                                 _pad_cols(gdn_in_w[j, :, n_gdn:]), geo, dst_blk=_gdn_dst_blocks(GDN_HP))
            dtb, alog = _pad_row(gdn_dt_bias[j]), _pad_row(gdn_a_log[j])
            nw = gdn_norm_w[j].reshape(1, -1)
            cw = gdn_conv_w[j][:, perm]
            yf, pc, st_gdn = _gdn_scan(proj, None, abp, cw, dtb, alog, None, state_gdn[:, j, 0], None,
                                       st_gdn, j, geo, False)
            u, st_gdn = _gdn_scan(proj, pc, abp, None, dtb, alog, nw, state_gdn[:, j, 1], yf,
                                  st_gdn, j, geo, True)
            w_out = wb['gdn_out']
        x = _out_proj(u, w_out, j, x, mod, 2, geo)

        proj, _ = _in_proj(x, norm_ffn_w[i], mod, 3, wb['ffn_in'], i, ffn_in_w.shape[2], None, geo)
        x = _out_proj(proj, wb['ffn_out'], i, x, mod, 5, geo,
                      final_w=final_norm_w if i == depth - 1 else None,
                      conv=(ffn_conv_w[i].reshape(9, -1), ffn_conv_b[i]))

    y_ctx, y_lat = x
    return (y_ctx.reshape(Bc, SEQ, D), y_lat.reshape(Bs, LS, D), st_ssm, st_gla, st_gdn)
```

```python
import functools
from typing import NamedTuple

import numpy as np
import jax
import jax.numpy as jnp
from jax import lax
from jax.experimental import pallas as pl
from jax.experimental.pallas import tpu as pltpu

f32, bf16 = jnp.float32, jnp.bfloat16

BLK = 256
CHUNK = 64
SSD_CHUNK = 128
HALO = 16
GRID_W = 64
NORM_EPS = 1e-6
SMALL = 128
GLA_RANK = 16
GLA_NORMALIZER = 16.0
SSM_GROUPS = 8
SSM_HEADDIM = 64
SSM_STATE = 128
SSD_GP = 8
GDN_HP = 8
VMEM_LIMIT = 48 * 1024 * 1024

NN = (((1,), (0,)), ((), ()))
NT = (((1,), (1,)), ((), ()))
TN = (((0,), (0,)), ((), ()))


class Geo(NamedTuple):
    D: int
    Tc: int
    Ts: int
    SEQ: int
    LS: int
    Bc: int
    Bs: int

    @property
    def T(self):
        return self.Tc + self.Ts

    @property
    def nb(self):
        return self.T // BLK

    @property
    def ncb(self):
        return self.Tc // BLK

    @property
    def bpc(self):
        return self.SEQ // BLK

    @property
    def bps(self):
        return self.LS // BLK


def _mm(a, b, dims=NN):
    return lax.dot_general(a, b, dims, preferred_element_type=f32)


def _split3(a):
    hi = a.astype(bf16)
    r1 = a - hi.astype(f32)
    mid = r1.astype(bf16)
    lo = (r1 - mid.astype(f32)).astype(bf16)
    return hi, mid, lo


def _mm_x(a, b, dims=NN):
    if isinstance(a, tuple):
        ps = [_mm(p, b, dims) for p in a]
    elif a.dtype == bf16:
        ps = [_mm(a, p, dims) for p in _split3(b)]
    else:
        ps = [_mm(p, b, dims) for p in _split3(a)]
    return ps[0] + ps[1] + ps[2]


def _mm_hi(a, b):
    ah = a.astype(bf16)
    al = (a - ah.astype(f32)).astype(bf16)
    bh = b.astype(bf16)
    bl = (b - bh.astype(f32)).astype(bf16)
    return _mm(ah, bh) + _mm(al, bh) + _mm(ah, bl)


def _silu(x):
    h = 0.5 * x
    return h + h * jnp.tanh(h)


def _tri_mask(n, upper, chunk=None, strict=False):
    r = lax.broadcasted_iota(jnp.int32, (n, n), 0)
    c = lax.broadcasted_iota(jnp.int32, (n, n), 1)
    if upper:
        m = (c > r) if strict else (c >= r)
    else:
        m = (c < r) if strict else (c <= r)
    if chunk is not None and chunk < n:
        sh = int(np.log2(chunk))
        same = lax.shift_right_logical(r, sh) == lax.shift_right_logical(c, sh)
        m = jnp.logical_and(m, same)
    return m


def _as_bf16(mask):
    return jnp.where(mask, 1.0, 0.0).astype(bf16)


def _ada_kernel(c_ref, w_ref, b_ref, o_ref):
    o_ref[0] = _mm_hi(_silu(c_ref[...]), w_ref[0]) + b_ref[0]


def _ada_table(cond8, ada_w, ada_b):
    depth, D, N6 = ada_w.shape
    tn = N6 // 4
    return pl.pallas_call(
        _ada_kernel,
        out_shape=jax.ShapeDtypeStruct((depth, 8, N6), f32),
        grid=(depth, N6 // tn),
        in_specs=[pl.BlockSpec((8, D), lambda l, j: (0, 0)),
                  pl.BlockSpec((1, D, tn), lambda l, j: (l, 0, j)),
                  pl.BlockSpec((1, 1, tn), lambda l, j: (l, 0, j))],
        out_specs=pl.BlockSpec((1, 8, tn), lambda l, j: (l, 0, j)),
        compiler_params=pltpu.CompilerParams(
            dimension_semantics=("parallel", "parallel"), vmem_limit_bytes=VMEM_LIMIT),
        name="ada_table",
    )(cond8, ada_w, ada_b.reshape(depth, 1, N6))


def _load_x(refs, n_ctx_tiles, pair):
    if not pair:
        return refs[0][...], refs[1:]
    return jnp.where(pl.program_id(0) < n_ctx_tiles, refs[0][...], refs[1][...]), refs[2:]


def _x_specs(x, tm, geo):
    if not isinstance(x, tuple):
        return [pl.BlockSpec((tm, geo.D), lambda i: (i, 0))], [x]
    nct = geo.Tc // tm
    return ([pl.BlockSpec((tm, geo.D), lambda i: (jnp.minimum(i, nct - 1), 0)),
             pl.BlockSpec((tm, geo.D), lambda i: (jnp.maximum(i - nct, 0), 0))], list(x))


def _in_proj_kernel(*refs, has_small, tn, dst_blk, n_ctx_tiles, pair):
    x, refs = _load_x(refs, n_ctx_tiles, pair)
    if has_small:
        nw_ref, sh_ref, sc_ref, w_ref, ws_ref, o_ref, os_ref = refs
    else:
        nw_ref, sh_ref, sc_ref, w_ref, o_ref = refs
    h = x * lax.rsqrt(jnp.mean(x * x, axis=-1, keepdims=True) + NORM_EPS) * nw_ref[...]
    h = h * (1.0 + sc_ref[0]) + sh_ref[0]
    hb = h.astype(bf16)
    if has_small:
        ws = ws_ref[...]
        wh = ws.astype(bf16)
        wl = (ws - wh.astype(f32)).astype(bf16)
        hl = (h - hb.astype(f32)).astype(bf16)
        os_ref[...] = _mm(hb, wh) + _mm(hl, wh) + _mm(hb, wl)
    per = tn // 128
    for n in range(w_ref.shape[2] // tn):
        r = _mm(hb, w_ref[0, :, n * tn:(n + 1) * tn]).astype(o_ref.dtype)
        if dst_blk is None:
            o_ref[:, n * tn:(n + 1) * tn] = r
            continue
        k = 0
        while k < per:
            d0 = dst_blk[n * per + k]
            run = 1
            while k + run < per and dst_blk[n * per + k + run] == d0 + run:
                run += 1
            o_ref[:, d0 * 128:(d0 + run) * 128] = r[:, k * 128:(k + run) * 128]
            k += run


def _row_tile(geo, pref):
    for tm in (pref, 512, 256):
        if tm <= pref and geo.Tc % tm == 0 and geo.LS % tm == 0:
            return tm
    raise ValueError("token counts must be multiples of 256")


def _cond_of_tile(i, geo, tm):
    nct = geo.Tc // tm
    return jnp.where(i < nct, 0, 1 + jnp.maximum(i - nct, 0) // (geo.LS // tm))


def _in_proj(x, norm_w, mod, shift_idx, w_stack, layer, N, w_small, geo, tn=512, dst_blk=None):
    T, D = geo.T, geo.D
    tm = _row_tile(geo, 512)
    assert N % tn == 0 and w_stack.dtype == bf16
    x_specs, x_args = _x_specs(x, tm, geo)
    has_small = w_small is not None
    cond = functools.partial(_cond_of_tile, geo=geo, tm=tm)
    in_specs = [*x_specs,
                pl.BlockSpec((1, D), lambda i: (0, 0)),
                pl.BlockSpec((1, 1, D), lambda i: (cond(i), 0, shift_idx)),
                pl.BlockSpec((1, 1, D), lambda i: (cond(i), 0, shift_idx + 1)),
                pl.BlockSpec((1, D, N), lambda i: (layer, 0, 0), pipeline_mode=pl.Buffered(1))]
    args = [*x_args, norm_w.reshape(1, D), mod, mod, w_stack]
    out_shape = [jax.ShapeDtypeStruct((T, N), bf16)]
    out_specs = [pl.BlockSpec((tm, N), lambda i: (i, 0))]
    if has_small:
        in_specs.append(pl.BlockSpec((D, SMALL), lambda i: (0, 0)))
        args.append(w_small)
        out_shape.append(jax.ShapeDtypeStruct((T, SMALL), f32))
        out_specs.append(pl.BlockSpec((tm, SMALL), lambda i: (i, 0)))
    res = pl.pallas_call(
        functools.partial(_in_proj_kernel, has_small=has_small, tn=tn, dst_blk=dst_blk,
                          n_ctx_tiles=geo.Tc // tm, pair=isinstance(x, tuple)),
        out_shape=out_shape,
        grid=(T // tm,),
        in_specs=in_specs,
        out_specs=out_specs,
        compiler_params=pltpu.CompilerParams(
            dimension_semantics=("parallel",), vmem_limit_bytes=VMEM_LIMIT),
        name="in_proj",
    )(*args)
    return res if has_small else (res[0], None)


CONV_HALO = 128
CONV_TF = 256


def _conv_glu_matmul(a_ref, ap_ref, an_ref, v_ref, cw_ref, cb_ref, w_ref, pad_ref, u_ref, acc_ref,
                     *, n_ctx_tiles, tiles_per_img, SEQ):
    i = pl.program_id(0)
    tm, F = a_ref.shape
    H = CONV_HALO
    pos_img = lax.rem(jnp.maximum(i - n_ctx_tiles, 0), tiles_per_img)
    top_ok = jnp.logical_and(i >= n_ctx_tiles, pos_img > 0)
    bot_ok = jnp.logical_and(i >= n_ctx_tiles, pos_img < tiles_per_img - 1)
    pad_ref[0:H, :] = jnp.where(top_ok, ap_ref[...].astype(f32), 0.0)
    pad_ref[H + tm:H + tm + H, :] = jnp.where(bot_ok, an_ref[...].astype(f32), 0.0)
    pad_ref[H:H + tm, :] = a_ref[...].astype(f32)
    tf = CONV_TF
    row = lax.broadcasted_iota(jnp.int32, (BLK, tf), 0)

    def run(grid):
        acc = None
        for n in range(F // tf):
            cs = slice(n * tf, (n + 1) * tf)
            w = cw_ref[:, cs]
            b = cb_ref[:, cs]

            def taps(s, pos, width, kr):
                left = jnp.where(pos > 0, pad_ref[s - 1:s - 1 + BLK, cs], 0.0)
                right = jnp.where(pos < width - 1, pad_ref[s + 1:s + 1 + BLK, cs], 0.0)
                return (left * w[3 * kr:3 * kr + 1] + pad_ref[s:s + BLK, cs] * w[3 * kr + 1:3 * kr + 2]
                        + right * w[3 * kr + 2:3 * kr + 3])

            for c in range(tm // BLK):
                s = H + c * BLK
                if grid:
                    col = jnp.bitwise_and(row, GRID_W - 1)
                    conv = (taps(s - GRID_W, col, GRID_W, 0) + taps(s, col, GRID_W, 1)
                            + taps(s + GRID_W, col, GRID_W, 2))
                else:
                    pos = jnp.bitwise_and(row + (c * BLK) % SEQ, SEQ - 1)
                    conv = taps(s, pos, SEQ, 1)
                v = v_ref[c * BLK:(c + 1) * BLK, cs].astype(f32)
                u_ref[c * BLK:(c + 1) * BLK, cs] = (_silu(conv + b) * v).astype(bf16)
            part = _mm(u_ref[:, cs], w_ref[0, cs, :])
            acc = part if acc is None else acc + part
        acc_ref[...] = acc

    @pl.when(i < n_ctx_tiles)
    def _():
        run(False)

    @pl.when(i >= n_ctx_tiles)
    def _():
        run(True)

    return acc_ref[...]


def _out_proj_kernel(*refs, n_ctx_tiles, final, conv, pair):
    if conv is not None:
        a_ref, ap_ref, an_ref, v_ref, cw_ref, cb_ref = refs[:6]
        refs = refs[6:]
    else:
        u_ref = refs[0]
        refs = refs[1:]
    w_ref = refs[0]
    x, refs = _load_x(refs[1:], n_ctx_tiles, pair)
    g_ref = refs[0]
    refs = refs[1:]
    if final:
        fw_ref, oc_ref, os_ref = refs[:3]
        scratch = refs[3:]
    else:
        o_ref = refs[0]
        scratch = refs[1:]
    if conv is not None:
        acc = _conv_glu_matmul(a_ref, ap_ref, an_ref, v_ref, cw_ref, cb_ref, w_ref, *scratch,
                               n_ctx_tiles=n_ctx_tiles, **conv)
    else:
        acc = _mm(u_ref[...], w_ref[0])
    r = x + g_ref[0] * acc
    if not final:
        o_ref[...] = r
        return
    r = r * lax.rsqrt(jnp.mean(r * r, axis=-1, keepdims=True) + NORM_EPS) * fw_ref[...]

    @pl.when(pl.program_id(0) < n_ctx_tiles)
    def _():
        oc_ref[...] = r

    @pl.when(pl.program_id(0) >= n_ctx_tiles)
    def _():
        os_ref[...] = r


def _out_proj(u, w_stack, layer, x, mod, gate_idx, geo, final_w=None, conv=None):
    T = u.shape[0]
    K, D = w_stack.shape[1:]
    tm = _row_tile(geo, 512)
    cond = functools.partial(_cond_of_tile, geo=geo, tm=tm)
    final = final_w is not None
    nct = geo.Tc // tm
    if conv is None:
        in_specs = [pl.BlockSpec((tm, K), lambda i: (i, 0))]
        args = [u]
        scratch, conv_cfg = [], None
    else:
        w9, cb = conv
        H = CONV_HALO
        per, lastb = tm // H, T // H - 1
        assert geo.LS % tm == 0 and tm % geo.SEQ == 0 and geo.SEQ & (geo.SEQ - 1) == 0 and K % CONV_TF == 0
        assert BLK % GRID_W == 0 and H > GRID_W
        in_specs = [pl.BlockSpec((tm, K), lambda i: (i, 0)),
                    pl.BlockSpec((H, K), lambda i: (jnp.maximum(i * per - 1, 0), 0)),
                    pl.BlockSpec((H, K), lambda i: (jnp.minimum((i + 1) * per, lastb), 0)),
                    pl.BlockSpec((tm, K), lambda i: (i, 1)),
                    pl.BlockSpec((9, K), lambda i: (0, 0)),
                    pl.BlockSpec((1, K), lambda i: (0, 0))]
        args = [u, u, u, u, w9, cb.reshape(1, K)]
        scratch = [pltpu.VMEM((tm + 2 * H, K), f32), pltpu.VMEM((tm, K), bf16), pltpu.VMEM((tm, D), f32)]
        conv_cfg = dict(tiles_per_img=geo.LS // tm, SEQ=geo.SEQ)
    x_specs, x_args = _x_specs(x, tm, geo)
    in_specs += [pl.BlockSpec((1, K, D), lambda i: (layer, 0, 0), pipeline_mode=pl.Buffered(1)),
                 *x_specs,
                 pl.BlockSpec((1, 1, D), lambda i: (cond(i), 0, gate_idx))]
    args += [w_stack, *x_args, mod]
    if final:
        in_specs.append(pl.BlockSpec((1, D), lambda i: (0, 0)))
        args.append(final_w.reshape(1, D))
        out_shape = [jax.ShapeDtypeStruct((geo.Tc, D), f32), jax.ShapeDtypeStruct((geo.Ts, D), f32)]
        out_specs = [pl.BlockSpec((tm, D), lambda i: (jnp.minimum(i, nct - 1), 0)),
                     pl.BlockSpec((tm, D), lambda i: (jnp.maximum(i - nct, 0), 0))]
    else:
        out_shape = jax.ShapeDtypeStruct((T, D), f32)
        out_specs = pl.BlockSpec((tm, D), lambda i: (i, 0))
    return pl.pallas_call(
        functools.partial(_out_proj_kernel, n_ctx_tiles=nct, final=final, conv=conv_cfg,
                          pair=isinstance(x, tuple)),
        out_shape=out_shape,
        grid=(T // tm,),
        in_specs=in_specs,
        out_specs=out_specs,
        scratch_shapes=scratch,
        compiler_params=pltpu.CompilerParams(
            dimension_semantics=("parallel" if not (final or isinstance(x, tuple)) else "arbitrary",),
            vmem_limit_bytes=56 * 1024 * 1024),
        name="ffn_out" if conv is not None else "out_proj",
    )(*args)


def _blk_idx(i, geo, rev):
    return (geo.nb - 1 - i) if rev else i


def _tok_spec(cw, colfn, geo, rev):
    return pl.BlockSpec((BLK, cw), lambda u, i: (_blk_idx(i, geo, rev), colfn(u)))


def _halo_specs(cw, colfn, geo, rev):
    per = BLK // HALO
    last = geo.T // HALO - 1
    prev = pl.BlockSpec((HALO, cw), lambda u, i: (jnp.maximum(_blk_idx(i, geo, rev) * per - 1, 0), colfn(u)))
    nxt = pl.BlockSpec((HALO, cw), lambda u, i: (jnp.minimum((_blk_idx(i, geo, rev) + 1) * per, last), colfn(u)))
    return prev, nxt


def _h0_spec(geo, rev, hp):
    def idx(u, i):
        ip = _blk_idx(i, geo, rev)
        return (jnp.maximum(ip - geo.ncb, 0) // geo.bps, u, 0, 0)
    return pl.BlockSpec((1, hp, 128, 256), idx)


def _st_spec(geo, rev, layer, blk):
    d = 1 if rev else 0

    def idx(u, i):
        ip = _blk_idx(i, geo, rev)
        return (jnp.minimum(ip // geo.bpc, geo.Bc - 1), layer, d, u, 0, 0)
    return pl.BlockSpec((1, 1, 1) + blk, idx)


def _param_spec(rows, cw, colfn):
    return pl.BlockSpec((rows, cw), lambda u, i: (0, colfn(u)))


def _flags(geo, rev):
    ip = _blk_idx(pl.program_id(1), geo, rev)
    is_ctx = ip < geo.ncb
    pos = jnp.where(is_ctx, lax.rem(ip, geo.bpc), lax.rem(jnp.maximum(ip - geo.ncb, 0), geo.bps))
    n = jnp.where(is_ctx, geo.bpc, geo.bps)
    first = pos == 0
    last = pos == n - 1
    return is_ctx, first, last, (last if rev else first)


def _conv3_silu(cur_ref, prev_ref, next_ref, w_ref, b_ref, first, last):
    cur = cur_ref[...].astype(f32)
    n = cur.shape[0]
    row = lax.broadcasted_iota(jnp.int32, cur.shape, 0)
    pr = jnp.where(first, 0.0, prev_ref[HALO - 1:HALO, :].astype(f32))
    nx = jnp.where(last, 0.0, next_ref[0:1, :].astype(f32))
    xm = jnp.where(row == 0, pr, pltpu.roll(cur, 1, axis=0))
    xp = jnp.where(row == n - 1, nx, pltpu.roll(cur, n - 1, axis=0))
    w = w_ref[...]
    y = xm * w[0:1] + cur * w[1:2] + xp * w[2:3]
    if b_ref is not None:
        y = y + b_ref[...]
    return _silu(y)


def _init_state(S, h0_ref, is_ctx, start):
    @pl.when(start)
    def _():
        S[...] = jnp.where(is_ctx, 0.0, h0_ref[0])


def _norm_gate(o, gate, nw, gate_first):
    g = _silu(gate)
    if gate_first:
        o = o * g
    o = o * lax.rsqrt(jnp.mean(o * o, axis=-1, keepdims=True) + NORM_EPS) * nw
    if not gate_first:
        o = o * g
    return o


def _scan_call(kernel, name, grid_u, in_specs, args, outs, st, st_blk, layer, geo, rev, hp):
    out_shape = [jax.ShapeDtypeStruct((geo.T, c), dt) for c, dt, _ in outs]
    out_specs = [_tok_spec(bc, lambda u: u, geo, rev) for _, _, bc in outs]
    return pl.pallas_call(
        kernel,
        out_shape=out_shape + [jax.ShapeDtypeStruct(st.shape, st.dtype)],
        grid=(grid_u, geo.nb),
        in_specs=in_specs + [pl.BlockSpec(memory_space=pl.ANY)],
        out_specs=out_specs + [_st_spec(geo, rev, layer, st_blk)],
        scratch_shapes=[pltpu.VMEM((hp, 128, 256), f32)],
        input_output_aliases={len(args): len(outs)},
        compiler_params=pltpu.CompilerParams(
            dimension_semantics=("parallel", "arbitrary"), vmem_limit_bytes=VMEM_LIMIT),
        name=name,
    )(*args, st)


def _ssd_kernel(*refs, rev, geo, gp):
    skip = 1
    if rev:
        (pc_ref, dt_ref, dtb_ref, alog_ref, xsel_ref, h0_ref,
         dsk_ref, z_ref, nw_ref, yf_ref) = refs[:10]
        y_ref, st_ref, S = refs[10 + skip:]
    else:
        (xbc_c, xbc_p, xbc_n, dt_ref, cw_ref, cb_ref, dtb_ref, alog_ref, xsel_ref, h0_ref) = refs[:10]
        y_ref, pc_ref, st_ref, S = refs[10 + skip:]
    is_ctx, first, last, start = _flags(geo, rev)
    _init_state(S, h0_ref, is_ctx, start)
    Q = SSD_CHUNK
    nch = BLK // Q
    groups = range(gp)

    if rev:
        xbc = pc_ref[...].astype(f32)
    else:
        xbc = _conv3_silu(xbc_c, xbc_p, xbc_n, cw_ref, cb_ref, first, last)
        pc_ref[...] = xbc.astype(bf16)
    dt = jax.nn.softplus(dt_ref[...] + dtb_ref[...])
    la = dt * (-jnp.exp(alog_ref[...]))
    tri = _tri_mask(BLK, rev, chunk=Q)
    cs = _mm_x(_as_bf16(tri), la)
    cs3 = _split3(cs)
    cs_t = cs.T
    d = 1 if rev else 0
    dtb16 = dt.astype(bf16)
    tri_q = _tri_mask(Q, rev)

    xs, xvs, bms, cms, cses = [], [], [], [], []
    for g in groups:
        x = xbc[:, g * 512:g * 512 + 256]
        xs.append(x)
        bms.append(xbc[:, g * 512 + 256:g * 512 + 384].astype(bf16))
        cms.append(xbc[:, g * 512 + 384:g * 512 + 512].astype(bf16))
        xsel = xsel_ref[g]
        xvs.append(x * _mm(dtb16, xsel))
        cses.append(_mm_x(cs3, xsel))
    Svs = [S[g] for g in groups]
    youts = [[None] * nch for _ in groups]
    for c in (range(nch - 1, -1, -1) if rev else range(nch)):
        lo = c * Q
        li = lo if rev else lo + Q - 1
        for g in groups:
            cs_e = cses[g][lo:lo + Q]
            xv = xvs[g][lo:lo + Q]
            xvb = xv.astype(bf16)
            cmb = cms[g][lo:lo + Q]
            bmb = bms[g][lo:lo + Q]
            gm = _mm(cmb, bmb, NT)
            ys = []
            for r in range(4):
                h = d * 32 + 4 * g + r
                seg = cs_e[:, r * 64:r * 64 + 1] - cs_t[h:h + 1, lo:lo + Q]
                m = jnp.where(tri_q, gm * jnp.exp(seg), 0.0)
                ys.append(_mm(m.astype(bf16), xvb[:, r * 64:(r + 1) * 64]))
            Sv = Svs[g]
            y = jnp.concatenate(ys, axis=1) + _mm(cmb, Sv.astype(bf16)) * jnp.exp(cs_e)
            cl = cses[g][li:li + 1, :]
            xw = xv * jnp.exp(cl - cs_e)
            Svs[g] = Sv * jnp.exp(cl) + _mm(bmb, xw.astype(bf16), TN)
            youts[g][c] = y

    for g in groups:
        S[g] = Svs[g]
        y = jnp.concatenate(youts[g], axis=0)
        if rev:
            cols = slice(g * 256, (g + 1) * 256)
            o = yf_ref[:, cols] + y + xs[g] * dsk_ref[:, cols]
            y = _norm_gate(o, z_ref[:, cols].astype(f32), nw_ref[:, cols], gate_first=True)
        y_ref[:, g * 256:(g + 1) * 256] = y.astype(y_ref.dtype)

    @pl.when(is_ctx)
    def _():
        for g in groups:
            for r in range(4):
                st_ref[0, 0, 0, g * 4 + r] = Svs[g][:, r * 64:(r + 1) * 64]


def _ssd_scan(proj, pc, dtp, conv_w, conv_b, dtb, alog, dskip, norm_w, h0, yf, st, layer, geo, rev):
    G, gp = SSM_GROUPS, SSD_GP
    nu = G // gp
    assert nu == 1
    d = 1 if rev else 0
    xsel = np.zeros((G, 128, 256), np.float32)
    for g in range(G):
        for r in range(4):
            xsel[g, d * 32 + 4 * g + r, r * 64:(r + 1) * 64] = 1.0
    cw, zw = 512 * gp, 256 * gp
    head_specs = [_param_spec(1, SMALL, lambda u: 0), _param_spec(1, SMALL, lambda u: 0),
                  pl.BlockSpec((gp, 128, 256), lambda u, i: (u, 0, 0)),
                  _h0_spec(geo, rev, gp)]
    head_args = [dtb, alog, jnp.asarray(xsel, bf16), h0]
    if rev:
        in_specs = [_tok_spec(cw, lambda u: u, geo, rev), _tok_spec(SMALL, lambda u: 0, geo, rev), *head_specs,
                    _param_spec(1, zw, lambda u: u), _tok_spec(zw, lambda u: 2 * nu + u, geo, rev),
                    _param_spec(1, zw, lambda u: u), _tok_spec(zw, lambda u: u, geo, rev)]
        args = [pc, dtp, *head_args, dskip, proj, norm_w, yf]
        outs = [(2048, bf16, zw)]
    else:
        in_specs = [_tok_spec(cw, lambda u: u, geo, rev), *_halo_specs(cw, lambda u: u, geo, rev),
                    _tok_spec(SMALL, lambda u: 0, geo, rev),
                    _param_spec(3, cw, lambda u: u), _param_spec(1, cw, lambda u: u), *head_specs]
        args = [proj, proj, proj, dtp, conv_w, conv_b, *head_args]
        outs = [(2048, f32, zw), (4096, bf16, cw)]
    kern = functools.partial(_ssd_kernel, rev=rev, geo=geo, gp=gp)
    return _scan_call(kern, "ssd_bwd" if rev else "ssd_fwd",
                      nu, in_specs, args, outs, st, (4 * gp, SSM_STATE, SSM_HEADDIM), layer, geo, rev, gp)


def _gla_kernel(*refs, rev, geo, nh):
    qkv_ref, lr_ref, w2_ref, gb_ref, h0_ref = refs[:5]
    skip = 1
    if rev:
        r_ref, nw_ref, yf_ref = refs[5:8]
        y_ref, st_ref, S = refs[8 + skip:]
    else:
        y_ref, st_ref, S = refs[5 + skip:]
    is_ctx, first, last, start = _flags(geo, rev)
    _init_state(S, h0_ref, is_ctx, start)
    d = 1 if rev else 0
    dk, dv = 128, 256
    heads = range(nh)
    K = nh * dk

    q_all = qkv_ref[:, 0:K].astype(f32) * dk ** -0.5
    k_all = qkv_ref[:, K:2 * K].astype(f32)
    vb_all = qkv_ref[:, 2 * K:2 * K + nh * dv]
    lr = lr_ref[...][:, d * GLA_RANK:(d + 1) * GLA_RANK]
    logit = _mm_hi(lr, w2_ref[0]) + gb_ref[0]
    lg = jax.nn.log_sigmoid(logit) * (1.0 / GLA_NORMALIZER)
    tri = _tri_mask(BLK, rev, chunk=CHUNK)
    cs = _mm_x(_as_bf16(tri), lg)
    cs_t = cs.T
    qgb = (q_all * jnp.exp(cs)).astype(bf16)
    kgb = (k_all * jnp.exp(-cs)).astype(bf16)

    ys = []
    for h in heads:
        sc = jnp.where(tri, _mm(qgb[:, h * dk:(h + 1) * dk], kgb[:, h * dk:(h + 1) * dk], NT), 0.0)
        ys.append(_mm(sc.astype(bf16), vb_all[:, h * dv:(h + 1) * dv]))
    Svs = [S[h] for h in heads]
    nch = BLK // CHUNK
    yin = [[None] * nch for _ in heads]
    for c in (range(nch - 1, -1, -1) if rev else range(nch)):
        lo = c * CHUNK
        li = lo if rev else lo + CHUNK - 1
        kw_all = (k_all[lo:lo + CHUNK] * jnp.exp(cs[li:li + 1, :] - cs[lo:lo + CHUNK])).astype(bf16)
        for h in heads:
            yin[h][c] = _mm(qgb[lo:lo + CHUNK, h * dk:(h + 1) * dk], Svs[h].astype(bf16))
        for h in heads:
            Svs[h] = (Svs[h] * jnp.exp(cs_t[h * dk:(h + 1) * dk, li:li + 1])
                      + _mm(kw_all[:, h * dk:(h + 1) * dk], vb_all[lo:lo + CHUNK, h * dv:(h + 1) * dv], TN))

    for h in heads:
        S[h] = Svs[h]
        y = ys[h] + jnp.concatenate(yin[h], axis=0)
        cols = slice(h * dv, (h + 1) * dv)
        if rev:
            y = _norm_gate(yf_ref[:, cols] + y, r_ref[:, cols].astype(f32), nw_ref[...], gate_first=False)
        y_ref[:, cols] = y.astype(y_ref.dtype)

    @pl.when(is_ctx)
    def _():
        for h in heads:
            st_ref[0, 0, 0, h] = Svs[h]


def _gla_scan(proj, lrp, w2, gb, norm_w, h0, yf, st, layer, geo, rev):
    H = 4
    d = 1 if rev else 0
    in_specs = [_tok_spec(2048, lambda u: 0, geo, rev), _tok_spec(SMALL, lambda u: 0, geo, rev),
                pl.BlockSpec((1, GLA_RANK, 512), lambda u, i: (d, 0, 0)),
                pl.BlockSpec((1, 1, 512), lambda u, i: (d, 0, 0)),
                _h0_spec(geo, rev, H)]
    args = [proj, lrp, w2, gb, h0]
    if rev:
        in_specs += [_tok_spec(1024, lambda u: 2, geo, rev), _param_spec(1, 256, lambda u: 0),
                     _tok_spec(1024, lambda u: 0, geo, rev)]
        args += [proj, norm_w, yf]
    kern = functools.partial(_gla_kernel, rev=rev, geo=geo, nh=H)
    return _scan_call(kern, "gla_bwd" if rev else "gla_fwd",
                      1, in_specs, args, [(1024, bf16 if rev else f32, 1024)], st, (H, 128, 256), layer, geo, rev, H)


def _unit_tri_inverse(ms, lev):
    eye = jnp.where(lev == -1, 1.0, 0.0)
    dinv = [eye - jnp.where(lev == 0, m, 0.0) for m in ms]
    for b in range(1, 6):
        ts = [_mm(jnp.where(lev == b, m, 0.0).astype(bf16), d.astype(bf16)).astype(bf16)
              for m, d in zip(ms, dinv)]
        dinv = [d - _mm(d.astype(bf16), t) for d, t in zip(dinv, ts)]
    return dinv


def _gdn_kernel(*refs, rev, geo, hp):
    skip = 1
    if rev:
        (pc_ref, ab_ref, dtb_ref, alog_ref, xab_ref, lev_ref, h0_ref, z_ref, nw_ref, yf_ref) = refs[:10]
        y_ref, st_ref, S = refs[10 + skip:]
    else:
        (qkv_c, qkv_p, qkv_n, cw_ref, ab_ref, dtb_ref, alog_ref, xab_ref, lev_ref, h0_ref) = refs[:10]
        y_ref, pc_ref, st_ref, S = refs[10 + skip:]
    is_ctx, first, last, start = _flags(geo, rev)
    _init_state(S, h0_ref, is_ctx, start)
    dk, dv = 128, 256
    heads = range(hp)

    if rev:
        qkv = pc_ref[...].astype(f32)
    else:
        qkv = _conv3_silu(qkv_c, qkv_p, qkv_n, cw_ref, None, first, last)

    ab = ab_ref[...]
    gall = -jnp.exp(alog_ref[...]) * jax.nn.softplus(ab + dtb_ref[...])
    lane = lax.broadcasted_iota(jnp.int32, ab.shape, 1)
    sel = _mm_x(jnp.where(lane < 16, gall, jax.nn.sigmoid(ab)), xab_ref[0])
    lev = lev_ref[...]
    r_i = lax.broadcasted_iota(jnp.int32, (BLK, BLK), 0)
    c_i = lax.broadcasted_iota(jnp.int32, (BLK, BLK), 1)
    in_chunk = lev < 6
    incl = jnp.logical_and(in_chunk, (c_i >= r_i) if rev else (c_i <= r_i))
    strict = jnp.logical_and(incl, lev >= 0)
    cs_all = _mm_x(_as_bf16(incl), sel)
    cs_rows = cs_all.T

    qs, ks, kbfs, kbs, vbs, css, lmasks, ms = [], [], [], [], [], [], [], []
    for hh in heads:
        q = qkv[:, hh * dk:(hh + 1) * dk]
        k = qkv[:, (hp + hh) * dk:(hp + hh + 1) * dk]
        v = qkv[:, 2 * hp * dk + hh * dv:2 * hp * dk + (hh + 1) * dv]
        if not rev:
            q = q * lax.rsqrt(jnp.sum(q * q, axis=-1, keepdims=True) + 1e-6) * dk ** -0.5
            k = k * lax.rsqrt(jnp.sum(k * k, axis=-1, keepdims=True) + 1e-6)
            pc_ref[:, hh * dk:(hh + 1) * dk] = q.astype(bf16)
            pc_ref[:, (hp + hh) * dk:(hp + hh + 1) * dk] = k.astype(bf16)
            pc_ref[:, 2 * hp * dk + hh * dv:2 * hp * dk + (hh + 1) * dv] = v.astype(bf16)
        beta = sel[:, 2 * hh + 1:2 * hh + 2]
        cs = cs_all[:, 2 * hh:2 * hh + 1]
        lmask = jnp.where(incl, jnp.exp(cs - cs_rows[2 * hh:2 * hh + 1, :]), 0.0)
        kb = k * beta
        kbf = k.astype(bf16)
        qs.append(q)
        ks.append(k)
        kbfs.append(kbf)
        kbs.append(kb)
        vbs.append((v * beta).astype(bf16))
        css.append(cs)
        lmasks.append(lmask)
        ms.append(jnp.where(strict, _mm(kb.astype(bf16), kbf, NT) * lmask, 0.0))
    tbs = [t.astype(bf16) for t in _unit_tri_inverse(ms, lev)]
    ecs = [jnp.exp(cs) for cs in css]
    us = [_mm(tbs[h], vbs[h]) for h in heads]
    wks = [_mm(tbs[h], (kbs[h] * ecs[h]).astype(bf16)).astype(bf16) for h in heads]
    a_ins = [(_mm(qs[h].astype(bf16), kbfs[h], NT) * lmasks[h]).astype(bf16) for h in heads]
    qds = [(qs[h] * ecs[h]).astype(bf16) for h in heads]

    Svs = [S[h] for h in heads]
    nch = BLK // CHUNK
    outs = [[None] * nch for _ in heads]
    for c in (range(nch - 1, -1, -1) if rev else range(nch)):
        lo = c * CHUNK
        li = lo if rev else lo + CHUNK - 1
        Sbs = [Sv.astype(bf16) for Sv in Svs]
        v_news = [(us[h][lo:lo + CHUNK] - _mm(wks[h][lo:lo + CHUNK], Sbs[h])).astype(bf16) for h in heads]
        for h in heads:
            outs[h][c] = (_mm(qds[h][lo:lo + CHUNK], Sbs[h])
                          + _mm(a_ins[h][lo:lo + CHUNK, lo:lo + CHUNK], v_news[h]))
        for h in heads:
            cl = css[h][li:li + 1, :]
            kd = ks[h][lo:lo + CHUNK] * jnp.exp(cl - css[h][lo:lo + CHUNK])
            Svs[h] = Svs[h] * jnp.exp(cl) + _mm(kd.astype(bf16), v_news[h], TN)

    for hh in heads:
        S[hh] = Svs[hh]
        o = jnp.concatenate(outs[hh], axis=0)
        cols = slice(hh * dv, (hh + 1) * dv)
        if rev:
            o = _norm_gate(yf_ref[:, cols] + o, z_ref[:, cols].astype(f32), nw_ref[...], gate_first=False)
        y_ref[:, cols] = o.astype(y_ref.dtype)

    @pl.when(is_ctx)
    def _():
        for hh in heads:
            st_ref[0, 0, 0, hh] = Svs[hh]


def _gdn_scan(proj, pc, abp, conv_w, dtb, alog, norm_w, h0, yf, st, layer, geo, rev):
    H, hp = 8, GDN_HP
    nu = H // hp
    d = 1 if rev else 0
    xab = np.zeros((nu, 128, 128), np.float32)
    for h in range(H):
        xab[h // hp, d * H + h, 2 * (h % hp)] = 1.0
        xab[h // hp, 2 * H + d * H + h, 2 * (h % hp) + 1] = 1.0
    idx = np.arange(BLK)
    x = idx[:, None] ^ idx[None, :]
    lev = np.where(x == 0, -1, np.floor(np.log2(np.maximum(x, 1)))).astype(np.int32)
    cw, vw = 512 * hp, 256 * hp
    head_specs = [_tok_spec(SMALL, lambda u: 0, geo, rev),
                  _param_spec(1, SMALL, lambda u: 0), _param_spec(1, SMALL, lambda u: 0),
                  pl.BlockSpec((1, 128, 128), lambda u, i: (u, 0, 0)),
                  pl.BlockSpec((BLK, BLK), lambda u, i: (0, 0)),
                  _h0_spec(geo, rev, hp)]
    head_args = [abp, dtb, alog, jnp.asarray(xab, bf16), jnp.asarray(lev), h0]
    if rev:
        in_specs = [_tok_spec(cw, lambda u: u, geo, rev), *head_specs,
                    _tok_spec(vw, lambda u: 2 * nu + u, geo, rev), _param_spec(1, 256, lambda u: 0),
                    _tok_spec(vw, lambda u: u, geo, rev)]
        args = [pc, *head_args, proj, norm_w, yf]
        outs = [(2048, bf16, vw)]
    else:
        in_specs = [_tok_spec(cw, lambda u: u, geo, rev), *_halo_specs(cw, lambda u: u, geo, rev),
                    _param_spec(3, cw, lambda u: u), *head_specs]
        args = [proj, proj, proj, conv_w, *head_args]
        outs = [(2048, f32, vw), (4096, bf16, cw)]
    kern = functools.partial(_gdn_kernel, rev=rev, geo=geo, hp=hp)
    return _scan_call(kern, "gdn_bwd" if rev else "gdn_fwd",
                      nu, in_specs, args, outs, st, (hp, 128, 256), layer, geo, rev, hp)


def _pad_cols(w, n=SMALL):
    return jnp.pad(w.astype(f32), ((0, 0), (0, n - w.shape[1])))


def _pad_row(v, n=SMALL):
    v = v.reshape(1, -1).astype(f32)
    return jnp.pad(v, ((0, 0), (0, n - v.shape[1])))


def _ssd_perm():
    perm = []
    for g in range(SSM_GROUPS):
        perm += list(range(g * 256, (g + 1) * 256))
        perm += list(range(2048 + g * 128, 2048 + (g + 1) * 128))
        perm += list(range(3072 + g * 128, 3072 + (g + 1) * 128))
    return np.asarray(perm, np.int32)


def _ssd_dst_blocks():
    dst = [32 + i for i in range(16)]
    dst += [4 * (i // 2) + i % 2 for i in range(16)]
    dst += [4 * g + 2 for g in range(SSM_GROUPS)]
    dst += [4 * g + 3 for g in range(SSM_GROUPS)]
    return tuple(dst)


def _gdn_dst_blocks(hp):
    inv = np.argsort(_gdn_perm(hp)[::128] // 128)
    return tuple(int(b) for b in inv) + tuple(range(32, 48))


def _gdn_perm(hp):
    perm = []
    for u in range(8 // hp):
        perm += list(range(u * hp * 128, (u + 1) * hp * 128))
        perm += list(range(1024 + u * hp * 128, 1024 + (u + 1) * hp * 128))
        perm += list(range(2048 + u * hp * 256, 2048 + (u + 1) * hp * 256))
    return np.asarray(perm, np.int32)


def kernel(x_prompt, x_sample, state_ssm, state_gla, state_gdn, c, c_ctx, norm_mix_w, norm_ffn_w, ada_w, ada_b, ffn_in_w, ffn_conv_w, ffn_conv_b, ffn_out_w, ssm_in_w, ssm_conv_w, ssm_conv_b, ssm_dt_bias, ssm_a_log, ssm_d, ssm_norm_w, ssm_out_w, gla_in_w, gla_gate_w1, gla_gate_w2, gla_gate_b, gla_norm_w, gla_out_w, gdn_in_w, gdn_conv_w, gdn_dt_bias, gdn_a_log, gdn_norm_w, gdn_out_w, final_norm_w):
    Bc, SEQ, D = x_prompt.shape
    Bs, LS, _ = x_sample.shape
    geo = Geo(D=D, Tc=Bc * SEQ, Ts=Bs * LS, SEQ=SEQ, LS=LS, Bc=Bc, Bs=Bs)
    assert SEQ % BLK == 0 and LS % BLK == 0 and Bs < 8
    depth = ada_w.shape[0]
    G, P, N = SSM_GROUPS, SSM_HEADDIM, SSM_STATE

    x = (x_prompt.reshape(geo.Tc, D), x_sample.reshape(geo.Ts, D))
    cond8 = jnp.zeros((8, D), f32).at[0].set(c_ctx).at[1:1 + Bs].set(c)
    mod_all = _ada_table(cond8, ada_w, ada_b)[:, :1 + Bs].reshape(depth, 1 + Bs, 1, 6 * D)

    n_ssm = ssm_out_w.shape[1] + 4096
    n_gdn = 6144
    wb = dict(ssm_in=ssm_in_w[:, :, :n_ssm].astype(bf16), ssm_out=ssm_out_w.astype(bf16),
              gla_in=gla_in_w.astype(bf16), gla_out=gla_out_w.astype(bf16),
              gdn_in=gdn_in_w[:, :, :n_gdn].astype(bf16), gdn_out=gdn_out_w.astype(bf16),
              ffn_in=ffn_in_w.astype(bf16), ffn_out=ffn_out_w.astype(bf16))
    st_ssm = jnp.zeros((Bc, ssm_in_w.shape[0], 2, G * 4, N, P), f32)
    st_gla = jnp.zeros((Bc, gla_in_w.shape[0], 2) + state_gla.shape[3:], f32)
    st_gdn = jnp.zeros((Bc, gdn_in_w.shape[0], 2) + state_gdn.shape[3:], f32)
    for i in range(depth):
        mod = mod_all[i]
        kind, j = i % 3, i // 3
        if kind == 0:
            di = ssm_out_w.shape[1]
            perm = _ssd_perm()
            proj, dtp = _in_proj(x, norm_mix_w[i], mod, 0, wb['ssm_in'], j, n_ssm,
                                 _pad_cols(ssm_in_w[j, :, n_ssm:]), geo, dst_blk=_ssd_dst_blocks())
            dtb, alog = _pad_row(ssm_dt_bias[j]), _pad_row(ssm_a_log[j])
            dskip = jnp.repeat(ssm_d[j], P).reshape(1, di)
            nw = ssm_norm_w[j].reshape(1, di)
            cw, cb = ssm_conv_w[j][:, perm], ssm_conv_b[j][perm].reshape(1, -1)
            h0 = state_ssm[:, j].reshape(Bs, 2, G, 4, N, P).transpose(1, 0, 2, 4, 3, 5).reshape(2, Bs, G, N, 4 * P)
            yf, pc, st_ssm = _ssd_scan(proj, None, dtp, cw, cb, dtb, alog, None, None, h0[0], None,
                                       st_ssm, j, geo, False)
            u, st_ssm = _ssd_scan(proj, pc, dtp, None, None, dtb, alog, dskip, nw, h0[1], yf,
                                  st_ssm, j, geo, True)
            w_out = wb['ssm_out']
        elif kind == 1:
            w_small = _pad_cols(jnp.concatenate([gla_gate_w1[j, 0], gla_gate_w1[j, 1]], axis=1))
            proj, lrp = _in_proj(x, norm_mix_w[i], mod, 0, wb['gla_in'], j, gla_in_w.shape[2], w_small, geo)
            w2 = gla_gate_w2[j]
            gb = gla_gate_b[j].reshape(2, 1, -1)
            nw = gla_norm_w[j].reshape(1, -1)
            yf, st_gla = _gla_scan(proj, lrp, w2, gb, None, state_gla[:, j, 0], None, st_gla, j, geo, False)
            u, st_gla = _gla_scan(proj, lrp, w2, gb, nw, state_gla[:, j, 1], yf, st_gla, j, geo, True)
            w_out = wb['gla_out']
        else:
            perm = _gdn_perm(GDN_HP)
            proj, abp = _in_proj(x, norm_mix_w[i], mod, 0, wb['gdn_in'], j, n_gdn,
                                 _pad_cols(gdn_in_w[j, :, n_gdn:]), geo, dst_blk=_gdn_dst_blocks(GDN_HP))
            dtb, alog = _pad_row(gdn_dt_bias[j]), _pad_row(gdn_a_log[j])
            nw = gdn_norm_w[j].reshape(1, -1)
            cw = gdn_conv_w[j][:, perm]
            yf, pc, st_gdn = _gdn_scan(proj, None, abp, cw, dtb, alog, None, state_gdn[:, j, 0], None,
                                       st_gdn, j, geo, False)
            u, st_gdn = _gdn_scan(proj, pc, abp, None, dtb, alog, nw, state_gdn[:, j, 1], yf,
                                  st_gdn, j, geo, True)
            w_out = wb['gdn_out']
        x = _out_proj(u, w_out, j, x, mod, 2, geo)

        proj, _ = _in_proj(x, norm_ffn_w[i], mod, 3, wb['ffn_in'], i, ffn_in_w.shape[2], None, geo)
        x = _out_proj(proj, wb['ffn_out'], i, x, mod, 5, geo,
                      final_w=final_norm_w if i == depth - 1 else None,
                      conv=(ffn_conv_w[i].reshape(9, -1), ffn_conv_b[i]))

    y_ctx, y_lat = x
    return (y_ctx.reshape(Bc, SEQ, D), y_lat.reshape(Bs, LS, D), st_ssm, st_gla, st_gdn)
```

```python
import functools
from typing import NamedTuple

import numpy as np
import jax
import jax.numpy as jnp
from jax import lax
from jax.experimental import pallas as pl
from jax.experimental.pallas import tpu as pltpu

f32, bf16 = jnp.float32, jnp.bfloat16

BLK = 256
CHUNK = 64
SSD_CHUNK = 128
HALO = 16
GRID_W = 64
NORM_EPS = 1e-6
SMALL = 128
GLA_RANK = 16
GLA_NORMALIZER = 16.0
SSM_GROUPS = 8
SSM_HEADDIM = 64
SSM_STATE = 128
SSD_GP = 8
GDN_HP = 8
VMEM_LIMIT = 48 * 1024 * 1024

NN = (((1,), (0,)), ((), ()))
NT = (((1,), (1,)), ((), ()))
TN = (((0,), (0,)), ((), ()))


class Geo(NamedTuple):
    D: int
    Tc: int
    Ts: int
    SEQ: int
    LS: int
    Bc: int
    Bs: int

    @property
    def T(self):
        return self.Tc + self.Ts

    @property
    def nb(self):
        return self.T // BLK

    @property
    def ncb(self):
        return self.Tc // BLK

    @property
    def bpc(self):
        return self.SEQ // BLK

    @property
    def bps(self):
        return self.LS // BLK


def _mm(a, b, dims=NN):
    return lax.dot_general(a, b, dims, preferred_element_type=f32)


def _split3(a):
    hi = a.astype(bf16)
    r1 = a - hi.astype(f32)
    mid = r1.astype(bf16)
    lo = (r1 - mid.astype(f32)).astype(bf16)
    return hi, mid, lo


def _mm_x(a, b, dims=NN):
    if isinstance(a, tuple):
        ps = [_mm(p, b, dims) for p in a]
    elif a.dtype == bf16:
        ps = [_mm(a, p, dims) for p in _split3(b)]
    else:
        ps = [_mm(p, b, dims) for p in _split3(a)]
    return ps[0] + ps[1] + ps[2]


def _mm_hi(a, b):
    ah = a.astype(bf16)
    al = (a - ah.astype(f32)).astype(bf16)
    bh = b.astype(bf16)
    bl = (b - bh.astype(f32)).astype(bf16)
    return _mm(ah, bh) + _mm(al, bh) + _mm(ah, bl)


def _silu(x):
    h = 0.5 * x
    return h + h * jnp.tanh(h)


def _tri_mask(n, upper, chunk=None, strict=False):
    r = lax.broadcasted_iota(jnp.int32, (n, n), 0)
    c = lax.broadcasted_iota(jnp.int32, (n, n), 1)
    if upper:
        m = (c > r) if strict else (c >= r)
    else:
        m = (c < r) if strict else (c <= r)
    if chunk is not None and chunk < n:
        sh = int(np.log2(chunk))
        same = lax.shift_right_logical(r, sh) == lax.shift_right_logical(c, sh)
        m = jnp.logical_and(m, same)
    return m


def _as_bf16(mask):
    return jnp.where(mask, 1.0, 0.0).astype(bf16)


def _ada_kernel(c_ref, w_ref, b_ref, o_ref):
    o_ref[0] = _mm_hi(_silu(c_ref[...]), w_ref[0]) + b_ref[0]


def _ada_table(cond8, ada_w, ada_b):
    depth, D, N6 = ada_w.shape
    tn = N6 // 4
    return pl.pallas_call(
        _ada_kernel,
        out_shape=jax.ShapeDtypeStruct((depth, 8, N6), f32),
        grid=(depth, N6 // tn),
        in_specs=[pl.BlockSpec((8, D), lambda l, j: (0, 0)),
                  pl.BlockSpec((1, D, tn), lambda l, j: (l, 0, j)),
                  pl.BlockSpec((1, 1, tn), lambda l, j: (l, 0, j))],
        out_specs=pl.BlockSpec((1, 8, tn), lambda l, j: (l, 0, j)),
        compiler_params=pltpu.CompilerParams(
            dimension_semantics=("parallel", "parallel"), vmem_limit_bytes=VMEM_LIMIT),
        name="ada_table",
    )(cond8, ada_w, ada_b.reshape(depth, 1, N6))


def _load_x(refs, n_ctx_tiles, pair):
    if not pair:
        return refs[0][...], refs[1:]
    return jnp.where(pl.program_id(0) < n_ctx_tiles, refs[0][...], refs[1][...]), refs[2:]


def _x_specs(x, tm, geo):
    if not isinstance(x, tuple):
        return [pl.BlockSpec((tm, geo.D), lambda i: (i, 0))], [x]
    nct = geo.Tc // tm
    return ([pl.BlockSpec((tm, geo.D), lambda i: (jnp.minimum(i, nct - 1), 0)),
             pl.BlockSpec((tm, geo.D), lambda i: (jnp.maximum(i - nct, 0), 0))], list(x))


def _in_proj_kernel(*refs, has_small, tn, dst_blk, n_ctx_tiles, pair, wt):
    x, refs = _load_x(refs, n_ctx_tiles, pair)
    if has_small:
        nw_ref, sh_ref, sc_ref, w_ref, ws_ref, o_ref, os_ref = refs
    else:
        nw_ref, sh_ref, sc_ref, w_ref, o_ref = refs
    h = x * lax.rsqrt(jnp.mean(x * x, axis=-1, keepdims=True) + NORM_EPS) * nw_ref[...]
    h = h * (1.0 + sc_ref[0]) + sh_ref[0]
    hb = h.astype(bf16)
    if has_small:
        ws = ws_ref[...]
        wh = ws.astype(bf16)
        wl = (ws - wh.astype(f32)).astype(bf16)
        hl = (h - hb.astype(f32)).astype(bf16)
        os_ref[...] = _mm(hb, wh) + _mm(hl, wh) + _mm(hb, wl)
    per = tn // 128
    for n in range(o_ref.shape[1] // tn):
        if wt:
            r = _mm(hb, w_ref[0, n * tn:(n + 1) * tn, :], NT).astype(o_ref.dtype)
        else:
            r = _mm(hb, w_ref[0, :, n * tn:(n + 1) * tn]).astype(o_ref.dtype)
        if dst_blk is None:
            o_ref[:, n * tn:(n + 1) * tn] = r
            continue
        k = 0
        while k < per:
            d0 = dst_blk[n * per + k]
            run = 1
            while k + run < per and dst_blk[n * per + k + run] == d0 + run:
                run += 1
            o_ref[:, d0 * 128:(d0 + run) * 128] = r[:, k * 128:(k + run) * 128]
            k += run


def _row_tile(geo, pref):
    for tm in (pref, 512, 256):
        if tm <= pref and geo.Tc % tm == 0 and geo.LS % tm == 0:
            return tm
    raise ValueError("token counts must be multiples of 256")


def _cond_of_tile(i, geo, tm):
    nct = geo.Tc // tm
    return jnp.where(i < nct, 0, 1 + jnp.maximum(i - nct, 0) // (geo.LS // tm))


def _in_proj(x, norm_w, mod, shift_idx, w_stack, layer, N, w_small, geo, tn=512, dst_blk=None, wt=False):
    T, D = geo.T, geo.D
    tm = _row_tile(geo, 512)
    assert N % tn == 0 and w_stack.dtype == bf16
    x_specs, x_args = _x_specs(x, tm, geo)
    has_small = w_small is not None
    cond = functools.partial(_cond_of_tile, geo=geo, tm=tm)
    in_specs = [*x_specs,
                pl.BlockSpec((1, D), lambda i: (0, 0)),
                pl.BlockSpec((1, 1, D), lambda i: (cond(i), 0, shift_idx)),
                pl.BlockSpec((1, 1, D), lambda i: (cond(i), 0, shift_idx + 1)),
                pl.BlockSpec((1, N, D) if wt else (1, D, N), lambda i: (layer, 0, 0),
                             pipeline_mode=pl.Buffered(1))]
    args = [*x_args, norm_w.reshape(1, D), mod, mod, w_stack]
    out_shape = [jax.ShapeDtypeStruct((T, N), bf16)]
    out_specs = [pl.BlockSpec((tm, N), lambda i: (i, 0))]
    if has_small:
        in_specs.append(pl.BlockSpec((D, SMALL), lambda i: (0, 0)))
        args.append(w_small)
        out_shape.append(jax.ShapeDtypeStruct((T, SMALL), f32))
        out_specs.append(pl.BlockSpec((tm, SMALL), lambda i: (i, 0)))
    res = pl.pallas_call(
        functools.partial(_in_proj_kernel, has_small=has_small, tn=tn, dst_blk=dst_blk,
                          n_ctx_tiles=geo.Tc // tm, pair=isinstance(x, tuple), wt=wt),
        out_shape=out_shape,
        grid=(T // tm,),
        in_specs=in_specs,
        out_specs=out_specs,
        compiler_params=pltpu.CompilerParams(
            dimension_semantics=("parallel",), vmem_limit_bytes=VMEM_LIMIT),
        name="in_proj",
    )(*args)
    return res if has_small else (res[0], None)


CONV_HALO = 128
CONV_TF = 256


def _conv_glu_matmul(a_ref, ap_ref, an_ref, v_ref, cw_ref, cb_ref, w_ref, pad_ref, u_ref, acc_ref,
                     *, n_ctx_tiles, tiles_per_img, SEQ):
    i = pl.program_id(0)
    tm, F = a_ref.shape
    H = CONV_HALO
    pos_img = lax.rem(jnp.maximum(i - n_ctx_tiles, 0), tiles_per_img)
    top_ok = jnp.logical_and(i >= n_ctx_tiles, pos_img > 0)
    bot_ok = jnp.logical_and(i >= n_ctx_tiles, pos_img < tiles_per_img - 1)
    pad_ref[0:H, :] = jnp.where(top_ok, ap_ref[...].astype(f32), 0.0)
    pad_ref[H + tm:H + tm + H, :] = jnp.where(bot_ok, an_ref[...].astype(f32), 0.0)
    pad_ref[H:H + tm, :] = a_ref[...].astype(f32)
    tf = CONV_TF
    row = lax.broadcasted_iota(jnp.int32, (BLK, tf), 0)

    def run(grid):
        acc = None
        for n in range(F // tf):
            cs = slice(n * tf, (n + 1) * tf)
            w = cw_ref[:, cs]
            b = cb_ref[:, cs]

            def taps(s, pos, width, kr):
                left = jnp.where(pos > 0, pad_ref[s - 1:s - 1 + BLK, cs], 0.0)
                right = jnp.where(pos < width - 1, pad_ref[s + 1:s + 1 + BLK, cs], 0.0)
                return (left * w[3 * kr:3 * kr + 1] + pad_ref[s:s + BLK, cs] * w[3 * kr + 1:3 * kr + 2]
                        + right * w[3 * kr + 2:3 * kr + 3])

            for c in range(tm // BLK):
                s = H + c * BLK
                if grid:
                    col = jnp.bitwise_and(row, GRID_W - 1)
                    conv = (taps(s - GRID_W, col, GRID_W, 0) + taps(s, col, GRID_W, 1)
                            + taps(s + GRID_W, col, GRID_W, 2))
                else:
                    pos = jnp.bitwise_and(row + (c * BLK) % SEQ, SEQ - 1)
                    conv = taps(s, pos, SEQ, 1)
                v = v_ref[c * BLK:(c + 1) * BLK, cs].astype(f32)
                u_ref[c * BLK:(c + 1) * BLK, cs] = (_silu(conv + b) * v).astype(bf16)
            part = _mm(u_ref[:, cs], w_ref[0, cs, :])
            acc = part if acc is None else acc + part
        acc_ref[...] = acc

    @pl.when(i < n_ctx_tiles)
    def _():
        run(False)

    @pl.when(i >= n_ctx_tiles)
    def _():
        run(True)

    return acc_ref[...]


def _out_proj_kernel(*refs, n_ctx_tiles, final, conv, pair):
    if conv is not None:
        a_ref, ap_ref, an_ref, v_ref, cw_ref, cb_ref = refs[:6]
        refs = refs[6:]
    else:
        u_ref = refs[0]
        refs = refs[1:]
    w_ref = refs[0]
    x, refs = _load_x(refs[1:], n_ctx_tiles, pair)
    g_ref = refs[0]
    refs = refs[1:]
    if final:
        fw_ref, oc_ref, os_ref = refs[:3]
        scratch = refs[3:]
    else:
        o_ref = refs[0]
        scratch = refs[1:]
    if conv is not None:
        acc = _conv_glu_matmul(a_ref, ap_ref, an_ref, v_ref, cw_ref, cb_ref, w_ref, *scratch,
                               n_ctx_tiles=n_ctx_tiles, **conv)
    else:
        acc = _mm(u_ref[...], w_ref[0])
    r = x + g_ref[0] * acc
    if not final:
        o_ref[...] = r
        return
    r = r * lax.rsqrt(jnp.mean(r * r, axis=-1, keepdims=True) + NORM_EPS) * fw_ref[...]

    @pl.when(pl.program_id(0) < n_ctx_tiles)
    def _():
        oc_ref[...] = r

    @pl.when(pl.program_id(0) >= n_ctx_tiles)
    def _():
        os_ref[...] = r


def _out_proj(u, w_stack, layer, x, mod, gate_idx, geo, final_w=None, conv=None):
    T = u.shape[0]
    K, D = w_stack.shape[1:]
    tm = _row_tile(geo, 512)
    cond = functools.partial(_cond_of_tile, geo=geo, tm=tm)
    final = final_w is not None
    nct = geo.Tc // tm
    if conv is None:
        in_specs = [pl.BlockSpec((tm, K), lambda i: (i, 0))]
        args = [u]
        scratch, conv_cfg = [], None
    else:
        w9, cb = conv
        H = CONV_HALO
        per, lastb = tm // H, T // H - 1
        assert geo.LS % tm == 0 and tm % geo.SEQ == 0 and geo.SEQ & (geo.SEQ - 1) == 0 and K % CONV_TF == 0
        assert BLK % GRID_W == 0 and H > GRID_W
        in_specs = [pl.BlockSpec((tm, K), lambda i: (i, 0)),
                    pl.BlockSpec((H, K), lambda i: (jnp.maximum(i * per - 1, 0), 0)),
                    pl.BlockSpec((H, K), lambda i: (jnp.minimum((i + 1) * per, lastb), 0)),
                    pl.BlockSpec((tm, K), lambda i: (i, 1)),
                    pl.BlockSpec((9, K), lambda i: (0, 0)),
                    pl.BlockSpec((1, K), lambda i: (0, 0))]
        args = [u, u, u, u, w9, cb.reshape(1, K)]
        scratch = [pltpu.VMEM((tm + 2 * H, K), f32), pltpu.VMEM((tm, K), bf16), pltpu.VMEM((tm, D), f32)]
        conv_cfg = dict(tiles_per_img=geo.LS // tm, SEQ=geo.SEQ)
    x_specs, x_args = _x_specs(x, tm, geo)
    in_specs += [pl.BlockSpec((1, K, D), lambda i: (layer, 0, 0), pipeline_mode=pl.Buffered(1)),
                 *x_specs,
                 pl.BlockSpec((1, 1, D), lambda i: (cond(i), 0, gate_idx))]
    args += [w_stack, *x_args, mod]
    if final:
        in_specs.append(pl.BlockSpec((1, D), lambda i: (0, 0)))
        args.append(final_w.reshape(1, D))
        out_shape = [jax.ShapeDtypeStruct((geo.Tc, D), f32), jax.ShapeDtypeStruct((geo.Ts, D), f32)]
        out_specs = [pl.BlockSpec((tm, D), lambda i: (jnp.minimum(i, nct - 1), 0)),
                     pl.BlockSpec((tm, D), lambda i: (jnp.maximum(i - nct, 0), 0))]
    else:
        out_shape = jax.ShapeDtypeStruct((T, D), f32)
        out_specs = pl.BlockSpec((tm, D), lambda i: (i, 0))
    return pl.pallas_call(
        functools.partial(_out_proj_kernel, n_ctx_tiles=nct, final=final, conv=conv_cfg,
                          pair=isinstance(x, tuple)),
        out_shape=out_shape,
        grid=(T // tm,),
        in_specs=in_specs,
        out_specs=out_specs,
        scratch_shapes=scratch,
        compiler_params=pltpu.CompilerParams(
            dimension_semantics=("parallel" if not (final or isinstance(x, tuple)) else "arbitrary",),
            vmem_limit_bytes=56 * 1024 * 1024),
        name="ffn_out" if conv is not None else "out_proj",
    )(*args)


def _blk_idx(i, geo, rev):
    return (geo.nb - 1 - i) if rev else i


def _tok_spec(cw, colfn, geo, rev):
    return pl.BlockSpec((BLK, cw), lambda u, i: (_blk_idx(i, geo, rev), colfn(u)))


def _halo_specs(cw, colfn, geo, rev):
    per = BLK // HALO
    last = geo.T // HALO - 1
    prev = pl.BlockSpec((HALO, cw), lambda u, i: (jnp.maximum(_blk_idx(i, geo, rev) * per - 1, 0), colfn(u)))
    nxt = pl.BlockSpec((HALO, cw), lambda u, i: (jnp.minimum((_blk_idx(i, geo, rev) + 1) * per, last), colfn(u)))
    return prev, nxt


def _h0_spec(geo, rev, hp):
    def idx(u, i):
        ip = _blk_idx(i, geo, rev)
        return (jnp.maximum(ip - geo.ncb, 0) // geo.bps, u, 0, 0)
    return pl.BlockSpec((1, hp, 128, 256), idx)


def _st_spec(geo, rev, layer, blk):
    d = 1 if rev else 0

    def idx(u, i):
        ip = _blk_idx(i, geo, rev)
        return (jnp.minimum(ip // geo.bpc, geo.Bc - 1), layer, d, u, 0, 0)
    return pl.BlockSpec((1, 1, 1) + blk, idx)


def _param_spec(rows, cw, colfn):
    return pl.BlockSpec((rows, cw), lambda u, i: (0, colfn(u)))


def _flags(geo, rev):
    ip = _blk_idx(pl.program_id(1), geo, rev)
    is_ctx = ip < geo.ncb
    pos = jnp.where(is_ctx, lax.rem(ip, geo.bpc), lax.rem(jnp.maximum(ip - geo.ncb, 0), geo.bps))
    n = jnp.where(is_ctx, geo.bpc, geo.bps)
    first = pos == 0
    last = pos == n - 1
    return is_ctx, first, last, (last if rev else first)


def _conv3_silu(cur_ref, prev_ref, next_ref, w_ref, b_ref, first, last):
    cur = cur_ref[...].astype(f32)
    n = cur.shape[0]
    row = lax.broadcasted_iota(jnp.int32, cur.shape, 0)
    pr = jnp.where(first, 0.0, prev_ref[HALO - 1:HALO, :].astype(f32))
    nx = jnp.where(last, 0.0, next_ref[0:1, :].astype(f32))
    xm = jnp.where(row == 0, pr, pltpu.roll(cur, 1, axis=0))
    xp = jnp.where(row == n - 1, nx, pltpu.roll(cur, n - 1, axis=0))
    w = w_ref[...]
    y = xm * w[0:1] + cur * w[1:2] + xp * w[2:3]
    if b_ref is not None:
        y = y + b_ref[...]
    return _silu(y)


def _init_state(S, h0_ref, is_ctx, start):
    @pl.when(start)
    def _():
        S[...] = jnp.where(is_ctx, 0.0, h0_ref[0])


def _norm_gate(o, gate, nw, gate_first):
    g = _silu(gate)
    if gate_first:
        o = o * g
    o = o * lax.rsqrt(jnp.mean(o * o, axis=-1, keepdims=True) + NORM_EPS) * nw
    if not gate_first:
        o = o * g
    return o


def _scan_call(kernel, name, grid_u, in_specs, args, outs, st, st_blk, layer, geo, rev, hp):
    out_shape = [jax.ShapeDtypeStruct((geo.T, c), dt) for c, dt, _ in outs]
    out_specs = [_tok_spec(bc, lambda u: u, geo, rev) for _, _, bc in outs]
    return pl.pallas_call(
        kernel,
        out_shape=out_shape + [jax.ShapeDtypeStruct(st.shape, st.dtype)],
        grid=(grid_u, geo.nb),
        in_specs=in_specs + [pl.BlockSpec(memory_space=pl.ANY)],
        out_specs=out_specs + [_st_spec(geo, rev, layer, st_blk)],
        scratch_shapes=[pltpu.VMEM((hp, 128, 256), f32)],
        input_output_aliases={len(args): len(outs)},
        compiler_params=pltpu.CompilerParams(
            dimension_semantics=("parallel", "arbitrary"), vmem_limit_bytes=VMEM_LIMIT),
        name=name,
    )(*args, st)


def _ssd_kernel(*refs, rev, geo, gp):
    skip = 1
    if rev:
        (pc_ref, dt_ref, dtb_ref, alog_ref, xsel_ref, h0_ref,
         dsk_ref, z_ref, nw_ref, yf_ref) = refs[:10]
        y_ref, st_ref, S = refs[10 + skip:]
    else:
        (xbc_c, xbc_p, xbc_n, dt_ref, cw_ref, cb_ref, dtb_ref, alog_ref, xsel_ref, h0_ref) = refs[:10]
        y_ref, pc_ref, st_ref, S = refs[10 + skip:]
    is_ctx, first, last, start = _flags(geo, rev)
    _init_state(S, h0_ref, is_ctx, start)
    Q = SSD_CHUNK
    nch = BLK // Q
    groups = range(gp)

    if rev:
        xbc = pc_ref[...].astype(f32)
    else:
        xbc = _conv3_silu(xbc_c, xbc_p, xbc_n, cw_ref, cb_ref, first, last)
        pc_ref[...] = xbc.astype(bf16)
    dt = jax.nn.softplus(dt_ref[...] + dtb_ref[...])
    la = dt * (-jnp.exp(alog_ref[...]))
    tri = _tri_mask(BLK, rev, chunk=Q)
    cs = _mm_x(_as_bf16(tri), la)
    cs3 = _split3(cs)
    cs_t = cs.T
    d = 1 if rev else 0
    dtb16 = dt.astype(bf16)
    tri_q = _tri_mask(Q, rev)

    xs, xvs, bms, cms, cses = [], [], [], [], []
    for g in groups:
        x = xbc[:, g * 512:g * 512 + 256]
        xs.append(x)
        bms.append(xbc[:, g * 512 + 256:g * 512 + 384].astype(bf16))
        cms.append(xbc[:, g * 512 + 384:g * 512 + 512].astype(bf16))
        xsel = xsel_ref[g]
        xvs.append(x * _mm(dtb16, xsel))
        cses.append(_mm_x(cs3, xsel))
    Svs = [S[g] for g in groups]
    youts = [[None] * nch for _ in groups]
    for c in (range(nch - 1, -1, -1) if rev else range(nch)):
        lo = c * Q
        li = lo if rev else lo + Q - 1
        for g in groups:
            cs_e = cses[g][lo:lo + Q]
            xv = xvs[g][lo:lo + Q]
            xvb = xv.astype(bf16)
            cmb = cms[g][lo:lo + Q]
            bmb = bms[g][lo:lo + Q]
            gm = _mm(cmb, bmb, NT)
            ys = []
            for r in range(4):
                h = d * 32 + 4 * g + r
                seg = cs_e[:, r * 64:r * 64 + 1] - cs_t[h:h + 1, lo:lo + Q]
                m = jnp.where(tri_q, gm * jnp.exp(seg), 0.0)
                ys.append(_mm(m.astype(bf16), xvb[:, r * 64:(r + 1) * 64]))
            Sv = Svs[g]
            y = jnp.concatenate(ys, axis=1) + _mm(cmb, Sv.astype(bf16)) * jnp.exp(cs_e)
            cl = cses[g][li:li + 1, :]
            xw = xv * jnp.exp(cl - cs_e)
            Svs[g] = Sv * jnp.exp(cl) + _mm(bmb, xw.astype(bf16), TN)
            youts[g][c] = y

    for g in groups:
        S[g] = Svs[g]
        y = jnp.concatenate(youts[g], axis=0)
        if rev:
            cols = slice(g * 256, (g + 1) * 256)
            o = yf_ref[:, cols] + y + xs[g] * dsk_ref[:, cols]
            y = _norm_gate(o, z_ref[:, cols].astype(f32), nw_ref[:, cols], gate_first=True)
        y_ref[:, g * 256:(g + 1) * 256] = y.astype(y_ref.dtype)

    @pl.when(is_ctx)
    def _():
        for g in groups:
            for r in range(4):
                st_ref[0, 0, 0, g * 4 + r] = Svs[g][:, r * 64:(r + 1) * 64]


def _ssd_scan(proj, pc, dtp, conv_w, conv_b, dtb, alog, dskip, norm_w, h0, yf, st, layer, geo, rev):
    G, gp = SSM_GROUPS, SSD_GP
    nu = G // gp
    assert nu == 1
    d = 1 if rev else 0
    xsel = np.zeros((G, 128, 256), np.float32)
    for g in range(G):
        for r in range(4):
            xsel[g, d * 32 + 4 * g + r, r * 64:(r + 1) * 64] = 1.0
    cw, zw = 512 * gp, 256 * gp
    head_specs = [_param_spec(1, SMALL, lambda u: 0), _param_spec(1, SMALL, lambda u: 0),
                  pl.BlockSpec((gp, 128, 256), lambda u, i: (u, 0, 0)),
                  _h0_spec(geo, rev, gp)]
    head_args = [dtb, alog, jnp.asarray(xsel, bf16), h0]
    if rev:
        in_specs = [_tok_spec(cw, lambda u: u, geo, rev), _tok_spec(SMALL, lambda u: 0, geo, rev), *head_specs,
                    _param_spec(1, zw, lambda u: u), _tok_spec(zw, lambda u: 2 * nu + u, geo, rev),
                    _param_spec(1, zw, lambda u: u), _tok_spec(zw, lambda u: u, geo, rev)]
        args = [pc, dtp, *head_args, dskip, proj, norm_w, yf]
        outs = [(2048, bf16, zw)]
    else:
        in_specs = [_tok_spec(cw, lambda u: u, geo, rev), *_halo_specs(cw, lambda u: u, geo, rev),
                    _tok_spec(SMALL, lambda u: 0, geo, rev),
                    _param_spec(3, cw, lambda u: u), _param_spec(1, cw, lambda u: u), *head_specs]
        args = [proj, proj, proj, dtp, conv_w, conv_b, *head_args]
        outs = [(2048, f32, zw), (4096, bf16, cw)]
    kern = functools.partial(_ssd_kernel, rev=rev, geo=geo, gp=gp)
    return _scan_call(kern, "ssd_bwd" if rev else "ssd_fwd",
                      nu, in_specs, args, outs, st, (4 * gp, SSM_STATE, SSM_HEADDIM), layer, geo, rev, gp)


def _gla_kernel(*refs, rev, geo, nh):
    qkv_ref, lr_ref, w2_ref, gb_ref, h0_ref = refs[:5]
    skip = 1
    if rev:
        r_ref, nw_ref, yf_ref = refs[5:8]
        y_ref, st_ref, S = refs[8 + skip:]
    else:
        y_ref, st_ref, S = refs[5 + skip:]
    is_ctx, first, last, start = _flags(geo, rev)
    _init_state(S, h0_ref, is_ctx, start)
    d = 1 if rev else 0
    dk, dv = 128, 256
    heads = range(nh)
    K = nh * dk

    q_all = qkv_ref[:, 0:K].astype(f32) * dk ** -0.5
    k_all = qkv_ref[:, K:2 * K].astype(f32)
    vb_all = qkv_ref[:, 2 * K:2 * K + nh * dv]
    lr = lr_ref[...][:, d * GLA_RANK:(d + 1) * GLA_RANK]
    logit = _mm_hi(lr, w2_ref[0]) + gb_ref[0]
    lg = jax.nn.log_sigmoid(logit) * (1.0 / GLA_NORMALIZER)
    tri = _tri_mask(BLK, rev, chunk=CHUNK)
    cs = _mm_x(_as_bf16(tri), lg)
    cs_t = cs.T
    qgb = (q_all * jnp.exp(cs)).astype(bf16)
    kgb = (k_all * jnp.exp(-cs)).astype(bf16)

    ys = []
    for h in heads:
        sc = jnp.where(tri, _mm(qgb[:, h * dk:(h + 1) * dk], kgb[:, h * dk:(h + 1) * dk], NT), 0.0)
        ys.append(_mm(sc.astype(bf16), vb_all[:, h * dv:(h + 1) * dv]))
    Svs = [S[h] for h in heads]
    nch = BLK // CHUNK
    yin = [[None] * nch for _ in heads]
    for c in (range(nch - 1, -1, -1) if rev else range(nch)):
        lo = c * CHUNK
        li = lo if rev else lo + CHUNK - 1
        kw_all = (k_all[lo:lo + CHUNK] * jnp.exp(cs[li:li + 1, :] - cs[lo:lo + CHUNK])).astype(bf16)
        for h in heads:
            yin[h][c] = _mm(qgb[lo:lo + CHUNK, h * dk:(h + 1) * dk], Svs[h].astype(bf16))
        for h in heads:
            Svs[h] = (Svs[h] * jnp.exp(cs_t[h * dk:(h + 1) * dk, li:li + 1])
                      + _mm(kw_all[:, h * dk:(h + 1) * dk], vb_all[lo:lo + CHUNK, h * dv:(h + 1) * dv], TN))

    for h in heads:
        S[h] = Svs[h]
        y = ys[h] + jnp.concatenate(yin[h], axis=0)
        cols = slice(h * dv, (h + 1) * dv)
        if rev:
            y = _norm_gate(yf_ref[:, cols] + y, r_ref[:, cols].astype(f32), nw_ref[...], gate_first=False)
        y_ref[:, cols] = y.astype(y_ref.dtype)

    @pl.when(is_ctx)
    def _():
        for h in heads:
            st_ref[0, 0, 0, h] = Svs[h]


def _gla_scan(proj, lrp, w2, gb, norm_w, h0, yf, st, layer, geo, rev):
    H = 4
    d = 1 if rev else 0
    in_specs = [_tok_spec(2048, lambda u: 0, geo, rev), _tok_spec(SMALL, lambda u: 0, geo, rev),
                pl.BlockSpec((1, GLA_RANK, 512), lambda u, i: (d, 0, 0)),
                pl.BlockSpec((1, 1, 512), lambda u, i: (d, 0, 0)),
                _h0_spec(geo, rev, H)]
    args = [proj, lrp, w2, gb, h0]
    if rev:
        in_specs += [_tok_spec(1024, lambda u: 2, geo, rev), _param_spec(1, 256, lambda u: 0),
                     _tok_spec(1024, lambda u: 0, geo, rev)]
        args += [proj, norm_w, yf]
    kern = functools.partial(_gla_kernel, rev=rev, geo=geo, nh=H)
    return _scan_call(kern, "gla_bwd" if rev else "gla_fwd",
                      1, in_specs, args, [(1024, bf16 if rev else f32, 1024)], st, (H, 128, 256), layer, geo, rev, H)


def _unit_tri_inverse(ms, lev):
    eye = jnp.where(lev == -1, 1.0, 0.0)
    dinv = [eye - jnp.where(lev == 0, m, 0.0) for m in ms]
    for b in range(1, 6):
        ts = [_mm(jnp.where(lev == b, m, 0.0).astype(bf16), d.astype(bf16)).astype(bf16)
              for m, d in zip(ms, dinv)]
        dinv = [d - _mm(d.astype(bf16), t) for d, t in zip(dinv, ts)]
    return dinv


def _gdn_kernel(*refs, rev, geo, hp):
    skip = 1
    if rev:
        (pc_ref, ab_ref, dtb_ref, alog_ref, xab_ref, lev_ref, h0_ref, z_ref, nw_ref, yf_ref) = refs[:10]
        y_ref, st_ref, S = refs[10 + skip:]
    else:
        (qkv_c, qkv_p, qkv_n, cw_ref, ab_ref, dtb_ref, alog_ref, xab_ref, lev_ref, h0_ref) = refs[:10]
        y_ref, pc_ref, st_ref, S = refs[10 + skip:]
    is_ctx, first, last, start = _flags(geo, rev)
    _init_state(S, h0_ref, is_ctx, start)
    dk, dv = 128, 256
    heads = range(hp)

    if rev:
        qkv = pc_ref[...].astype(f32)
    else:
        qkv = _conv3_silu(qkv_c, qkv_p, qkv_n, cw_ref, None, first, last)

    ab = ab_ref[...]
    gall = -jnp.exp(alog_ref[...]) * jax.nn.softplus(ab + dtb_ref[...])
    lane = lax.broadcasted_iota(jnp.int32, ab.shape, 1)
    sel = _mm_x(jnp.where(lane < 16, gall, jax.nn.sigmoid(ab)), xab_ref[0])
    lev = lev_ref[...]
    r_i = lax.broadcasted_iota(jnp.int32, (BLK, BLK), 0)
    c_i = lax.broadcasted_iota(jnp.int32, (BLK, BLK), 1)
    in_chunk = lev < 6
    incl = jnp.logical_and(in_chunk, (c_i >= r_i) if rev else (c_i <= r_i))
    strict = jnp.logical_and(incl, lev >= 0)
    cs_all = _mm_x(_as_bf16(incl), sel)
    cs_rows = cs_all.T

    qs, ks, kbfs, kbs, vbs, css, lmasks, ms = [], [], [], [], [], [], [], []
    for hh in heads:
        q = qkv[:, hh * dk:(hh + 1) * dk]
        k = qkv[:, (hp + hh) * dk:(hp + hh + 1) * dk]
        v = qkv[:, 2 * hp * dk + hh * dv:2 * hp * dk + (hh + 1) * dv]
        if not rev:
            q = q * lax.rsqrt(jnp.sum(q * q, axis=-1, keepdims=True) + 1e-6) * dk ** -0.5
            k = k * lax.rsqrt(jnp.sum(k * k, axis=-1, keepdims=True) + 1e-6)
            pc_ref[:, hh * dk:(hh + 1) * dk] = q.astype(bf16)
            pc_ref[:, (hp + hh) * dk:(hp + hh + 1) * dk] = k.astype(bf16)
            pc_ref[:, 2 * hp * dk + hh * dv:2 * hp * dk + (hh + 1) * dv] = v.astype(bf16)
        beta = sel[:, 2 * hh + 1:2 * hh + 2]
        cs = cs_all[:, 2 * hh:2 * hh + 1]
        lmask = jnp.where(incl, jnp.exp(cs - cs_rows[2 * hh:2 * hh + 1, :]), 0.0)
        kb = k * beta
        kbf = k.astype(bf16)
        qs.append(q)
        ks.append(k)
        kbfs.append(kbf)
        kbs.append(kb)
        vbs.append((v * beta).astype(bf16))
        css.append(cs)
        lmasks.append(lmask)
        ms.append(jnp.where(strict, _mm(kb.astype(bf16), kbf, NT) * lmask, 0.0))
    tbs = [t.astype(bf16) for t in _unit_tri_inverse(ms, lev)]
    ecs = [jnp.exp(cs) for cs in css]
    us = [_mm(tbs[h], vbs[h]) for h in heads]
    wks = [_mm(tbs[h], (kbs[h] * ecs[h]).astype(bf16)).astype(bf16) for h in heads]
    a_ins = [(_mm(qs[h].astype(bf16), kbfs[h], NT) * lmasks[h]).astype(bf16) for h in heads]
    qds = [(qs[h] * ecs[h]).astype(bf16) for h in heads]

    Svs = [S[h] for h in heads]
    nch = BLK // CHUNK
    outs = [[None] * nch for _ in heads]
    for c in (range(nch - 1, -1, -1) if rev else range(nch)):
        lo = c * CHUNK
        li = lo if rev else lo + CHUNK - 1
        Sbs = [Sv.astype(bf16) for Sv in Svs]
        v_news = [(us[h][lo:lo + CHUNK] - _mm(wks[h][lo:lo + CHUNK], Sbs[h])).astype(bf16) for h in heads]
        for h in heads:
            outs[h][c] = (_mm(qds[h][lo:lo + CHUNK], Sbs[h])
                          + _mm(a_ins[h][lo:lo + CHUNK, lo:lo + CHUNK], v_news[h]))
        for h in heads:
            cl = css[h][li:li + 1, :]
            kd = ks[h][lo:lo + CHUNK] * jnp.exp(cl - css[h][lo:lo + CHUNK])
            Svs[h] = Svs[h] * jnp.exp(cl) + _mm(kd.astype(bf16), v_news[h], TN)

    for hh in heads:
        S[hh] = Svs[hh]
        o = jnp.concatenate(outs[hh], axis=0)
        cols = slice(hh * dv, (hh + 1) * dv)
        if rev:
            o = _norm_gate(yf_ref[:, cols] + o, z_ref[:, cols].astype(f32), nw_ref[...], gate_first=False)
        y_ref[:, cols] = o.astype(y_ref.dtype)

    @pl.when(is_ctx)
    def _():
        for hh in heads:
            st_ref[0, 0, 0, hh] = Svs[hh]


def _gdn_scan(proj, pc, abp, conv_w, dtb, alog, norm_w, h0, yf, st, layer, geo, rev):
    H, hp = 8, GDN_HP
    nu = H // hp
    d = 1 if rev else 0
    xab = np.zeros((nu, 128, 128), np.float32)
    for h in range(H):
        xab[h // hp, d * H + h, 2 * (h % hp)] = 1.0
        xab[h // hp, 2 * H + d * H + h, 2 * (h % hp) + 1] = 1.0
    idx = np.arange(BLK)
    x = idx[:, None] ^ idx[None, :]
    lev = np.where(x == 0, -1, np.floor(np.log2(np.maximum(x, 1)))).astype(np.int32)
    cw, vw = 512 * hp, 256 * hp
    head_specs = [_tok_spec(SMALL, lambda u: 0, geo, rev),
                  _param_spec(1, SMALL, lambda u: 0), _param_spec(1, SMALL, lambda u: 0),
                  pl.BlockSpec((1, 128, 128), lambda u, i: (u, 0, 0)),
                  pl.BlockSpec((BLK, BLK), lambda u, i: (0, 0)),
                  _h0_spec(geo, rev, hp)]
    head_args = [abp, dtb, alog, jnp.asarray(xab, bf16), jnp.asarray(lev), h0]
    if rev:
        in_specs = [_tok_spec(cw, lambda u: u, geo, rev), *head_specs,
                    _tok_spec(vw, lambda u: 2 * nu + u, geo, rev), _param_spec(1, 256, lambda u: 0),
                    _tok_spec(vw, lambda u: u, geo, rev)]
        args = [pc, *head_args, proj, norm_w, yf]
        outs = [(2048, bf16, vw)]
    else:
        in_specs = [_tok_spec(cw, lambda u: u, geo, rev), *_halo_specs(cw, lambda u: u, geo, rev),
                    _param_spec(3, cw, lambda u: u), *head_specs]
        args = [proj, proj, proj, conv_w, *head_args]
        outs = [(2048, f32, vw), (4096, bf16, cw)]
    kern = functools.partial(_gdn_kernel, rev=rev, geo=geo, hp=hp)
    return _scan_call(kern, "gdn_bwd" if rev else "gdn_fwd",
                      nu, in_specs, args, outs, st, (hp, 128, 256), layer, geo, rev, hp)


def _pad_cols(w, n=SMALL):
    return jnp.pad(w.astype(f32), ((0, 0), (0, n - w.shape[1])))


def _pad_row(v, n=SMALL):
    v = v.reshape(1, -1).astype(f32)
    return jnp.pad(v, ((0, 0), (0, n - v.shape[1])))


def _ssd_perm():
    perm = []
    for g in range(SSM_GROUPS):
        perm += list(range(g * 256, (g + 1) * 256))
        perm += list(range(2048 + g * 128, 2048 + (g + 1) * 128))
        perm += list(range(3072 + g * 128, 3072 + (g + 1) * 128))
    return np.asarray(perm, np.int32)


def _ssd_dst_blocks():
    dst = [32 + i for i in range(16)]
    dst += [4 * (i // 2) + i % 2 for i in range(16)]
    dst += [4 * g + 2 for g in range(SSM_GROUPS)]
    dst += [4 * g + 3 for g in range(SSM_GROUPS)]
    return tuple(dst)


def _gdn_dst_blocks(hp):
    inv = np.argsort(_gdn_perm(hp)[::128] // 128)
    return tuple(int(b) for b in inv) + tuple(range(32, 48))


def _gdn_perm(hp):
    perm = []
    for u in range(8 // hp):
        perm += list(range(u * hp * 128, (u + 1) * hp * 128))
        perm += list(range(1024 + u * hp * 128, 1024 + (u + 1) * hp * 128))
        perm += list(range(2048 + u * hp * 256, 2048 + (u + 1) * hp * 256))
    return np.asarray(perm, np.int32)


def kernel(x_prompt, x_sample, state_ssm, state_gla, state_gdn, c, c_ctx, norm_mix_w, norm_ffn_w, ada_w, ada_b, ffn_in_w, ffn_conv_w, ffn_conv_b, ffn_out_w, ssm_in_w, ssm_conv_w, ssm_conv_b, ssm_dt_bias, ssm_a_log, ssm_d, ssm_norm_w, ssm_out_w, gla_in_w, gla_gate_w1, gla_gate_w2, gla_gate_b, gla_norm_w, gla_out_w, gdn_in_w, gdn_conv_w, gdn_dt_bias, gdn_a_log, gdn_norm_w, gdn_out_w, final_norm_w):
    Bc, SEQ, D = x_prompt.shape
    Bs, LS, _ = x_sample.shape
    geo = Geo(D=D, Tc=Bc * SEQ, Ts=Bs * LS, SEQ=SEQ, LS=LS, Bc=Bc, Bs=Bs)
    assert SEQ % BLK == 0 and LS % BLK == 0 and Bs < 8
    depth = ada_w.shape[0]
    G, P, N = SSM_GROUPS, SSM_HEADDIM, SSM_STATE

    x = (x_prompt.reshape(geo.Tc, D), x_sample.reshape(geo.Ts, D))
    cond8 = jnp.zeros((8, D), f32).at[0].set(c_ctx).at[1:1 + Bs].set(c)
    mod_all = _ada_table(cond8, ada_w, ada_b)[:, :1 + Bs].reshape(depth, 1 + Bs, 1, 6 * D)

    n_ssm = ssm_out_w.shape[1] + 4096
    n_gdn = 6144
    wb = dict(ssm_in=jnp.swapaxes(ssm_in_w, 1, 2).astype(bf16), ssm_out=ssm_out_w.astype(bf16),
              gla_in=gla_in_w.astype(bf16), gla_out=gla_out_w.astype(bf16),
              gdn_in=jnp.swapaxes(gdn_in_w, 1, 2).astype(bf16), gdn_out=gdn_out_w.astype(bf16),
              ffn_in=ffn_in_w.astype(bf16), ffn_out=ffn_out_w.astype(bf16))
    st_ssm = jnp.zeros((Bc, ssm_in_w.shape[0], 2, G * 4, N, P), f32)
    st_gla = jnp.zeros((Bc, gla_in_w.shape[0], 2) + state_gla.shape[3:], f32)
    st_gdn = jnp.zeros((Bc, gdn_in_w.shape[0], 2) + state_gdn.shape[3:], f32)
    for i in range(depth):
        mod = mod_all[i]
        kind, j = i % 3, i // 3
        if kind == 0:
            di = ssm_out_w.shape[1]
            perm = _ssd_perm()
            proj, dtp = _in_proj(x, norm_mix_w[i], mod, 0, wb['ssm_in'], j, n_ssm,
                                 _pad_cols(ssm_in_w[j, :, n_ssm:]), geo, dst_blk=_ssd_dst_blocks(), wt=True)
            dtb, alog = _pad_row(ssm_dt_bias[j]), _pad_row(ssm_a_log[j])
            dskip = jnp.repeat(ssm_d[j], P).reshape(1, di)
            nw = ssm_norm_w[j].reshape(1, di)
            cw, cb = ssm_conv_w[j][:, perm], ssm_conv_b[j][perm].reshape(1, -1)
            h0 = state_ssm[:, j].reshape(Bs, 2, G, 4, N, P).transpose(1, 0, 2, 4, 3, 5).reshape(2, Bs, G, N, 4 * P)
            yf, pc, st_ssm = _ssd_scan(proj, None, dtp, cw, cb, dtb, alog, None, None, h0[0], None,
                                       st_ssm, j, geo, False)
            u, st_ssm = _ssd_scan(proj, pc, dtp, None, None, dtb, alog, dskip, nw, h0[1], yf,
                                  st_ssm, j, geo, True)
            w_out = wb['ssm_out']
        elif kind == 1:
            w_small = _pad_cols(jnp.concatenate([gla_gate_w1[j, 0], gla_gate_w1[j, 1]], axis=1))
            proj, lrp = _in_proj(x, norm_mix_w[i], mod, 0, wb['gla_in'], j, gla_in_w.shape[2], w_small, geo)
            w2 = gla_gate_w2[j]
            gb = gla_gate_b[j].reshape(2, 1, -1)
            nw = gla_norm_w[j].reshape(1, -1)
            yf, st_gla = _gla_scan(proj, lrp, w2, gb, None, state_gla[:, j, 0], None, st_gla, j, geo, False)
            u, st_gla = _gla_scan(proj, lrp, w2, gb, nw, state_gla[:, j, 1], yf, st_gla, j, geo, True)
            w_out = wb['gla_out']
        else:
            perm = _gdn_perm(GDN_HP)
            proj, abp = _in_proj(x, norm_mix_w[i], mod, 0, wb['gdn_in'], j, n_gdn,
                                 _pad_cols(gdn_in_w[j, :, n_gdn:]), geo, dst_blk=_gdn_dst_blocks(GDN_HP), wt=True)
            dtb, alog = _pad_row(gdn_dt_bias[j]), _pad_row(gdn_a_log[j])
            nw = gdn_norm_w[j].reshape(1, -1)
            cw = gdn_conv_w[j][:, perm]
            yf, pc, st_gdn = _gdn_scan(proj, None, abp, cw, dtb, alog, None, state_gdn[:, j, 0], None,
                                       st_gdn, j, geo, False)
            u, st_gdn = _gdn_scan(proj, pc, abp, None, dtb, alog, nw, state_gdn[:, j, 1], yf,
                                  st_gdn, j, geo, True)
            w_out = wb['gdn_out']
        x = _out_proj(u, w_out, j, x, mod, 2, geo)

        proj, _ = _in_proj(x, norm_ffn_w[i], mod, 3, wb['ffn_in'], i, ffn_in_w.shape[2], None, geo)
        x = _out_proj(proj, wb['ffn_out'], i, x, mod, 5, geo,
                      final_w=final_norm_w if i == depth - 1 else None,
                      conv=(ffn_conv_w[i].reshape(9, -1), ffn_conv_b[i]))

    y_ctx, y_lat = x
    return (y_ctx.reshape(Bc, SEQ, D), y_lat.reshape(Bs, LS, D), st_ssm, st_gla, st_gdn)
```

```python
import functools
from typing import NamedTuple

import numpy as np
import jax
import jax.numpy as jnp
from jax import lax
from jax.experimental import pallas as pl
from jax.experimental.pallas import tpu as pltpu

f32, bf16 = jnp.float32, jnp.bfloat16

BLK = 256
CHUNK = 64
SSD_CHUNK = 128
HALO = 16
GRID_W = 64
NORM_EPS = 1e-6
SMALL = 128
GLA_RANK = 16
GLA_NORMALIZER = 16.0
SSM_GROUPS = 8
SSM_HEADDIM = 64
SSM_STATE = 128
SSD_GP = 8
GDN_HP = 8
VMEM_LIMIT = 48 * 1024 * 1024

NN = (((1,), (0,)), ((), ()))
NT = (((1,), (1,)), ((), ()))
TN = (((0,), (0,)), ((), ()))


class Geo(NamedTuple):
    D: int
    Tc: int
    Ts: int
    SEQ: int
    LS: int
    Bc: int
    Bs: int

    @property
    def T(self):
        return self.Tc + self.Ts

    @property
    def nb(self):
        return self.T // BLK

    @property
    def ncb(self):
        return self.Tc // BLK

    @property
    def bpc(self):
        return self.SEQ // BLK

    @property
    def bps(self):
        return self.LS // BLK


def _mm(a, b, dims=NN):
    return lax.dot_general(a, b, dims, preferred_element_type=f32)


def _split3(a):
    hi = a.astype(bf16)
    r1 = a - hi.astype(f32)
    mid = r1.astype(bf16)
    lo = (r1 - mid.astype(f32)).astype(bf16)
    return hi, mid, lo


def _mm_x(a, b, dims=NN):
    if isinstance(a, tuple):
        ps = [_mm(p, b, dims) for p in a]
    elif a.dtype == bf16:
        ps = [_mm(a, p, dims) for p in _split3(b)]
    else:
        ps = [_mm(p, b, dims) for p in _split3(a)]
    return ps[0] + ps[1] + ps[2]


def _mm_hi(a, b):
    ah = a.astype(bf16)
    al = (a - ah.astype(f32)).astype(bf16)
    bh = b.astype(bf16)
    bl = (b - bh.astype(f32)).astype(bf16)
    return _mm(ah, bh) + _mm(al, bh) + _mm(ah, bl)


def _silu(x):
    h = 0.5 * x
    return h + h * jnp.tanh(h)


def _tri_mask(n, upper, chunk=None, strict=False):
    r = lax.broadcasted_iota(jnp.int32, (n, n), 0)
    c = lax.broadcasted_iota(jnp.int32, (n, n), 1)
    if upper:
        m = (c > r) if strict else (c >= r)
    else:
        m = (c < r) if strict else (c <= r)
    if chunk is not None and chunk < n:
        sh = int(np.log2(chunk))
        same = lax.shift_right_logical(r, sh) == lax.shift_right_logical(c, sh)
        m = jnp.logical_and(m, same)
    return m


def _as_bf16(mask):
    return jnp.where(mask, 1.0, 0.0).astype(bf16)


def _ada_kernel(c_ref, w_ref, b_ref, o_ref):
    o_ref[0] = _mm_hi(_silu(c_ref[...]), w_ref[0]) + b_ref[0]


def _ada_table(cond8, ada_w, ada_b):
    depth, D, N6 = ada_w.shape
    tn = N6 // 4
    return pl.pallas_call(
        _ada_kernel,
        out_shape=jax.ShapeDtypeStruct((depth, 8, N6), f32),
        grid=(depth, N6 // tn),
        in_specs=[pl.BlockSpec((8, D), lambda l, j: (0, 0)),
                  pl.BlockSpec((1, D, tn), lambda l, j: (l, 0, j)),
                  pl.BlockSpec((1, 1, tn), lambda l, j: (l, 0, j))],
        out_specs=pl.BlockSpec((1, 8, tn), lambda l, j: (l, 0, j)),
        compiler_params=pltpu.CompilerParams(
            dimension_semantics=("parallel", "parallel"), vmem_limit_bytes=VMEM_LIMIT),
        name="ada_table",
    )(cond8, ada_w, ada_b.reshape(depth, 1, N6))


def _load_x(refs, n_ctx_tiles, pair):
    if not pair:
        return refs[0][...], refs[1:]
    return jnp.where(pl.program_id(0) < n_ctx_tiles, refs[0][...], refs[1][...]), refs[2:]


def _x_specs(x, tm, geo):
    if not isinstance(x, tuple):
        return [pl.BlockSpec((tm, geo.D), lambda i: (i, 0))], [x]
    nct = geo.Tc // tm
    return ([pl.BlockSpec((tm, geo.D), lambda i: (jnp.minimum(i, nct - 1), 0)),
             pl.BlockSpec((tm, geo.D), lambda i: (jnp.maximum(i - nct, 0), 0))], list(x))


def _in_proj_kernel(*refs, has_small, tn, dst_blk, n_ctx_tiles, pair, wt):
    x, refs = _load_x(refs, n_ctx_tiles, pair)
    if has_small:
        nw_ref, sh_ref, sc_ref, w_ref, ws_ref, o_ref, os_ref = refs
    else:
        nw_ref, sh_ref, sc_ref, w_ref, o_ref = refs
    h = x * lax.rsqrt(jnp.mean(x * x, axis=-1, keepdims=True) + NORM_EPS) * nw_ref[...]
    h = h * (1.0 + sc_ref[0]) + sh_ref[0]
    hb = h.astype(bf16)
    if has_small:
        ws = ws_ref[...]
        wh = ws.astype(bf16)
        wl = (ws - wh.astype(f32)).astype(bf16)
        hl = (h - hb.astype(f32)).astype(bf16)
        both = _mm(hb, jnp.concatenate([wh, wl], axis=1))
        os_ref[...] = both[:, :SMALL] + _mm(hl, wh) + both[:, SMALL:]
    per = tn // 128
    for n in range(o_ref.shape[1] // tn):
        if wt:
            r = _mm(hb, w_ref[0, n * tn:(n + 1) * tn, :], NT).astype(o_ref.dtype)
        else:
            r = _mm(hb, w_ref[0, :, n * tn:(n + 1) * tn]).astype(o_ref.dtype)
        if dst_blk is None:
            o_ref[:, n * tn:(n + 1) * tn] = r
            continue
        k = 0
        while k < per:
            d0 = dst_blk[n * per + k]
            run = 1
            while k + run < per and dst_blk[n * per + k + run] == d0 + run:
                run += 1
            o_ref[:, d0 * 128:(d0 + run) * 128] = r[:, k * 128:(k + run) * 128]
            k += run


def _row_tile(geo, pref):
    for tm in (pref, 512, 256):
        if tm <= pref and geo.Tc % tm == 0 and geo.LS % tm == 0:
            return tm
    raise ValueError("token counts must be multiples of 256")


def _cond_of_tile(i, geo, tm):
    nct = geo.Tc // tm
    return jnp.where(i < nct, 0, 1 + jnp.maximum(i - nct, 0) // (geo.LS // tm))


def _in_proj(x, norm_w, mod, shift_idx, w_stack, layer, N, w_small, geo, tn=512, dst_blk=None, wt=False):
    T, D = geo.T, geo.D
    tm = _row_tile(geo, 512)
    assert N % tn == 0 and w_stack.dtype == bf16
    x_specs, x_args = _x_specs(x, tm, geo)
    has_small = w_small is not None
    cond = functools.partial(_cond_of_tile, geo=geo, tm=tm)
    in_specs = [*x_specs,
                pl.BlockSpec((1, D), lambda i: (0, 0)),
                pl.BlockSpec((1, 1, D), lambda i: (cond(i), 0, shift_idx)),
                pl.BlockSpec((1, 1, D), lambda i: (cond(i), 0, shift_idx + 1)),
                pl.BlockSpec((1, N, D) if wt else (1, D, N), lambda i: (layer, 0, 0),
                             pipeline_mode=pl.Buffered(1))]
    args = [*x_args, norm_w.reshape(1, D), mod, mod, w_stack]
    out_shape = [jax.ShapeDtypeStruct((T, N), bf16)]
    out_specs = [pl.BlockSpec((tm, N), lambda i: (i, 0))]
    if has_small:
        in_specs.append(pl.BlockSpec((D, SMALL), lambda i: (0, 0)))
        args.append(w_small)
        out_shape.append(jax.ShapeDtypeStruct((T, SMALL), f32))
        out_specs.append(pl.BlockSpec((tm, SMALL), lambda i: (i, 0)))
    res = pl.pallas_call(
        functools.partial(_in_proj_kernel, has_small=has_small, tn=tn, dst_blk=dst_blk,
                          n_ctx_tiles=geo.Tc // tm, pair=isinstance(x, tuple), wt=wt),
        out_shape=out_shape,
        grid=(T // tm,),
        in_specs=in_specs,
        out_specs=out_specs,
        compiler_params=pltpu.CompilerParams(
            dimension_semantics=("parallel",), vmem_limit_bytes=VMEM_LIMIT),
        name="in_proj",
    )(*args)
    return res if has_small else (res[0], None)


CONV_HALO = 128
CONV_TF = 256


def _conv_glu_matmul(a_ref, ap_ref, an_ref, v_ref, cw_ref, cb_ref, w_ref, pad_ref, u_ref, acc_ref,
                     *, n_ctx_tiles, tiles_per_img, SEQ):
    i = pl.program_id(0)
    tm, F = a_ref.shape
    H = CONV_HALO
    pos_img = lax.rem(jnp.maximum(i - n_ctx_tiles, 0), tiles_per_img)
    top_ok = jnp.logical_and(i >= n_ctx_tiles, pos_img > 0)
    bot_ok = jnp.logical_and(i >= n_ctx_tiles, pos_img < tiles_per_img - 1)
    pad_ref[0:H, :] = jnp.where(top_ok, ap_ref[...].astype(f32), 0.0)
    pad_ref[H + tm:H + tm + H, :] = jnp.where(bot_ok, an_ref[...].astype(f32), 0.0)
    pad_ref[H:H + tm, :] = a_ref[...].astype(f32)
    tf = CONV_TF
    row = lax.broadcasted_iota(jnp.int32, (BLK, tf), 0)

    def run(grid):
        acc = None
        for n in range(F // tf):
            cs = slice(n * tf, (n + 1) * tf)
            w = cw_ref[:, cs]
            b = cb_ref[:, cs]

            def taps(s, pos, width, kr):
                left = jnp.where(pos > 0, pad_ref[s - 1:s - 1 + BLK, cs], 0.0)
                right = jnp.where(pos < width - 1, pad_ref[s + 1:s + 1 + BLK, cs], 0.0)
                return (left * w[3 * kr:3 * kr + 1] + pad_ref[s:s + BLK, cs] * w[3 * kr + 1:3 * kr + 2]
                        + right * w[3 * kr + 2:3 * kr + 3])

            for c in range(tm // BLK):
                s = H + c * BLK
                if grid:
                    col = jnp.bitwise_and(row, GRID_W - 1)
                    conv = (taps(s - GRID_W, col, GRID_W, 0) + taps(s, col, GRID_W, 1)
                            + taps(s + GRID_W, col, GRID_W, 2))
                else:
                    pos = jnp.bitwise_and(row + (c * BLK) % SEQ, SEQ - 1)
                    conv = taps(s, pos, SEQ, 1)
                v = v_ref[c * BLK:(c + 1) * BLK, cs].astype(f32)
                u_ref[c * BLK:(c + 1) * BLK, cs] = (_silu(conv + b) * v).astype(bf16)
            part = _mm(u_ref[:, cs], w_ref[0, cs, :])
            acc = part if acc is None else acc + part
        acc_ref[...] = acc

    @pl.when(i < n_ctx_tiles)
    def _():
        run(False)

    @pl.when(i >= n_ctx_tiles)
    def _():
        run(True)

    return acc_ref[...]


def _out_proj_kernel(*refs, n_ctx_tiles, final, conv, pair):
    if conv is not None:
        a_ref, ap_ref, an_ref, v_ref, cw_ref, cb_ref = refs[:6]
        refs = refs[6:]
    else:
        u_ref = refs[0]
        refs = refs[1:]
    w_ref = refs[0]
    x, refs = _load_x(refs[1:], n_ctx_tiles, pair)
    g_ref = refs[0]
    refs = refs[1:]
    if final:
        fw_ref, oc_ref, os_ref = refs[:3]
        scratch = refs[3:]
    else:
        o_ref = refs[0]
        scratch = refs[1:]
    if conv is not None:
        acc = _conv_glu_matmul(a_ref, ap_ref, an_ref, v_ref, cw_ref, cb_ref, w_ref, *scratch,
                               n_ctx_tiles=n_ctx_tiles, **conv)
    else:
        acc = _mm(u_ref[...], w_ref[0])
    r = x + g_ref[0] * acc
    if not final:
        o_ref[...] = r
        return
    r = r * lax.rsqrt(jnp.mean(r * r, axis=-1, keepdims=True) + NORM_EPS) * fw_ref[...]

    @pl.when(pl.program_id(0) < n_ctx_tiles)
    def _():
        oc_ref[...] = r

    @pl.when(pl.program_id(0) >= n_ctx_tiles)
    def _():
        os_ref[...] = r


def _out_proj(u, w_stack, layer, x, mod, gate_idx, geo, final_w=None, conv=None):
    T = u.shape[0]
    K, D = w_stack.shape[1:]
    tm = _row_tile(geo, 512)
    cond = functools.partial(_cond_of_tile, geo=geo, tm=tm)
    final = final_w is not None
    nct = geo.Tc // tm
    if conv is None:
        in_specs = [pl.BlockSpec((tm, K), lambda i: (i, 0))]
        args = [u]
        scratch, conv_cfg = [], None
    else:
        w9, cb = conv
        H = CONV_HALO
        per, lastb = tm // H, T // H - 1
        assert geo.LS % tm == 0 and tm % geo.SEQ == 0 and geo.SEQ & (geo.SEQ - 1) == 0 and K % CONV_TF == 0
        assert BLK % GRID_W == 0 and H > GRID_W
        in_specs = [pl.BlockSpec((tm, K), lambda i: (i, 0)),
                    pl.BlockSpec((H, K), lambda i: (jnp.maximum(i * per - 1, 0), 0)),
                    pl.BlockSpec((H, K), lambda i: (jnp.minimum((i + 1) * per, lastb), 0)),
                    pl.BlockSpec((tm, K), lambda i: (i, 1)),
                    pl.BlockSpec((9, K), lambda i: (0, 0)),
                    pl.BlockSpec((1, K), lambda i: (0, 0))]
        args = [u, u, u, u, w9, cb.reshape(1, K)]
        scratch = [pltpu.VMEM((tm + 2 * H, K), f32), pltpu.VMEM((tm, K), bf16), pltpu.VMEM((tm, D), f32)]
        conv_cfg = dict(tiles_per_img=geo.LS // tm, SEQ=geo.SEQ)
    x_specs, x_args = _x_specs(x, tm, geo)
    in_specs += [pl.BlockSpec((1, K, D), lambda i: (layer, 0, 0), pipeline_mode=pl.Buffered(1)),
                 *x_specs,
                 pl.BlockSpec((1, 1, D), lambda i: (cond(i), 0, gate_idx))]
    args += [w_stack, *x_args, mod]
    if final:
        in_specs.append(pl.BlockSpec((1, D), lambda i: (0, 0)))
        args.append(final_w.reshape(1, D))
        out_shape = [jax.ShapeDtypeStruct((geo.Tc, D), f32), jax.ShapeDtypeStruct((geo.Ts, D), f32)]
        out_specs = [pl.BlockSpec((tm, D), lambda i: (jnp.minimum(i, nct - 1), 0)),
                     pl.BlockSpec((tm, D), lambda i: (jnp.maximum(i - nct, 0), 0))]
    else:
        out_shape = jax.ShapeDtypeStruct((T, D), f32)
        out_specs = pl.BlockSpec((tm, D), lambda i: (i, 0))
    return pl.pallas_call(
        functools.partial(_out_proj_kernel, n_ctx_tiles=nct, final=final, conv=conv_cfg,
                          pair=isinstance(x, tuple)),
        out_shape=out_shape,
        grid=(T // tm,),
        in_specs=in_specs,
        out_specs=out_specs,
        scratch_shapes=scratch,
        compiler_params=pltpu.CompilerParams(
            dimension_semantics=("parallel" if not (final or isinstance(x, tuple)) else "arbitrary",),
            vmem_limit_bytes=56 * 1024 * 1024),
        name="ffn_out" if conv is not None else "out_proj",
    )(*args)


def _blk_idx(i, geo, rev):
    return (geo.nb - 1 - i) if rev else i


def _tok_spec(cw, colfn, geo, rev):
    return pl.BlockSpec((BLK, cw), lambda u, i: (_blk_idx(i, geo, rev), colfn(u)))


def _halo_specs(cw, colfn, geo, rev):
    per = BLK // HALO
    last = geo.T // HALO - 1
    prev = pl.BlockSpec((HALO, cw), lambda u, i: (jnp.maximum(_blk_idx(i, geo, rev) * per - 1, 0), colfn(u)))
    nxt = pl.BlockSpec((HALO, cw), lambda u, i: (jnp.minimum((_blk_idx(i, geo, rev) + 1) * per, last), colfn(u)))
    return prev, nxt


def _h0_spec(geo, rev, hp):
    def idx(u, i):
        ip = _blk_idx(i, geo, rev)
        return (jnp.maximum(ip - geo.ncb, 0) // geo.bps, u, 0, 0)
    return pl.BlockSpec((1, hp, 128, 256), idx)


def _st_spec(geo, rev, layer, blk):
    d = 1 if rev else 0

    def idx(u, i):
        ip = _blk_idx(i, geo, rev)
        return (jnp.minimum(ip // geo.bpc, geo.Bc - 1), layer, d, u, 0, 0)
    return pl.BlockSpec((1, 1, 1) + blk, idx)


def _param_spec(rows, cw, colfn):
    return pl.BlockSpec((rows, cw), lambda u, i: (0, colfn(u)))


def _flags(geo, rev):
    ip = _blk_idx(pl.program_id(1), geo, rev)
    is_ctx = ip < geo.ncb
    pos = jnp.where(is_ctx, lax.rem(ip, geo.bpc), lax.rem(jnp.maximum(ip - geo.ncb, 0), geo.bps))
    n = jnp.where(is_ctx, geo.bpc, geo.bps)
    first = pos == 0
    last = pos == n - 1
    return is_ctx, first, last, (last if rev else first)


def _conv3_silu(cur_ref, prev_ref, next_ref, w_ref, b_ref, first, last):
    cur = cur_ref[...].astype(f32)
    n = cur.shape[0]
    row = lax.broadcasted_iota(jnp.int32, cur.shape, 0)
    pr = jnp.where(first, 0.0, prev_ref[HALO - 1:HALO, :].astype(f32))
    nx = jnp.where(last, 0.0, next_ref[0:1, :].astype(f32))
    xm = jnp.where(row == 0, pr, pltpu.roll(cur, 1, axis=0))
    xp = jnp.where(row == n - 1, nx, pltpu.roll(cur, n - 1, axis=0))
    w = w_ref[...]
    y = xm * w[0:1] + cur * w[1:2] + xp * w[2:3]
    if b_ref is not None:
        y = y + b_ref[...]
    return _silu(y)


def _init_state(S, h0_ref, is_ctx, start):
    @pl.when(start)
    def _():
        S[...] = jnp.where(is_ctx, 0.0, h0_ref[0])


def _norm_gate(o, gate, nw, gate_first):
    g = _silu(gate)
    if gate_first:
        o = o * g
    o = o * lax.rsqrt(jnp.mean(o * o, axis=-1, keepdims=True) + NORM_EPS) * nw
    if not gate_first:
        o = o * g
    return o


def _scan_call(kernel, name, grid_u, in_specs, args, outs, st, st_blk, layer, geo, rev, hp):
    out_shape = [jax.ShapeDtypeStruct((geo.T, c), dt) for c, dt, _ in outs]
    out_specs = [_tok_spec(bc, lambda u: u, geo, rev) for _, _, bc in outs]
    return pl.pallas_call(
        kernel,
        out_shape=out_shape + [jax.ShapeDtypeStruct(st.shape, st.dtype)],
        grid=(grid_u, geo.nb),
        in_specs=in_specs + [pl.BlockSpec(memory_space=pl.ANY)],
        out_specs=out_specs + [_st_spec(geo, rev, layer, st_blk)],
        scratch_shapes=[pltpu.VMEM((hp, 128, 256), f32)],
        input_output_aliases={len(args): len(outs)},
        compiler_params=pltpu.CompilerParams(
            dimension_semantics=("parallel", "arbitrary"), vmem_limit_bytes=VMEM_LIMIT),
        name=name,
    )(*args, st)


def _ssd_kernel(*refs, rev, geo, gp):
    skip = 1
    if rev:
        (pc_ref, dt_ref, dtb_ref, alog_ref, xsel_ref, h0_ref,
         dsk_ref, z_ref, nw_ref, yf_ref) = refs[:10]
        y_ref, st_ref, S = refs[10 + skip:]
    else:
        (xbc_c, xbc_p, xbc_n, dt_ref, cw_ref, cb_ref, dtb_ref, alog_ref, xsel_ref, h0_ref) = refs[:10]
        y_ref, pc_ref, st_ref, S = refs[10 + skip:]
    is_ctx, first, last, start = _flags(geo, rev)
    _init_state(S, h0_ref, is_ctx, start)
    Q = SSD_CHUNK
    nch = BLK // Q
    groups = range(gp)

    if rev:
        xbc = pc_ref[...].astype(f32)
    else:
        xbc = _conv3_silu(xbc_c, xbc_p, xbc_n, cw_ref, cb_ref, first, last)
        pc_ref[...] = xbc.astype(bf16)
    dt = jax.nn.softplus(dt_ref[...] + dtb_ref[...])
    la = dt * (-jnp.exp(alog_ref[...]))
    tri = _tri_mask(BLK, rev, chunk=Q)
    cs = _mm_x(_as_bf16(tri), la)
    cs3 = _split3(cs)
    cs_t = cs.T
    d = 1 if rev else 0
    dtb16 = dt.astype(bf16)
    tri_q = _tri_mask(Q, rev)

    xs, xvs, bms, cms, cses = [], [], [], [], []
    for g in groups:
        x = xbc[:, g * 512:g * 512 + 256]
        xs.append(x)
        bms.append(xbc[:, g * 512 + 256:g * 512 + 384].astype(bf16))
        cms.append(xbc[:, g * 512 + 384:g * 512 + 512].astype(bf16))
        xsel = xsel_ref[g]
        xvs.append(x * _mm(dtb16, xsel))
        cses.append(_mm_x(cs3, xsel))
    Svs = [S[g] for g in groups]
    youts = [[None] * nch for _ in groups]
    for c in (range(nch - 1, -1, -1) if rev else range(nch)):
        lo = c * Q
        li = lo if rev else lo + Q - 1
        for g in groups:
            cs_e = cses[g][lo:lo + Q]
            xv = xvs[g][lo:lo + Q]
            xvb = xv.astype(bf16)
            cmb = cms[g][lo:lo + Q]
            bmb = bms[g][lo:lo + Q]
            gm = _mm(cmb, bmb, NT)
            ys = []
            for r in range(4):
                h = d * 32 + 4 * g + r
                seg = cs_e[:, r * 64:r * 64 + 1] - cs_t[h:h + 1, lo:lo + Q]
                m = jnp.where(tri_q, gm * jnp.exp(seg), 0.0)
                ys.append(_mm(m.astype(bf16), xvb[:, r * 64:(r + 1) * 64]))
            Sv = Svs[g]
            y = jnp.concatenate(ys, axis=1) + _mm(cmb, Sv.astype(bf16)) * jnp.exp(cs_e)
            cl = cses[g][li:li + 1, :]
            xw = xv * jnp.exp(cl - cs_e)
            Svs[g] = Sv * jnp.exp(cl) + _mm(bmb, xw.astype(bf16), TN)
            youts[g][c] = y

    for g in groups:
        S[g] = Svs[g]
        y = jnp.concatenate(youts[g], axis=0)
        if rev:
            cols = slice(g * 256, (g + 1) * 256)
            o = yf_ref[:, cols] + y + xs[g] * dsk_ref[:, cols]
            y = _norm_gate(o, z_ref[:, cols].astype(f32), nw_ref[:, cols], gate_first=True)
        y_ref[:, g * 256:(g + 1) * 256] = y.astype(y_ref.dtype)

    @pl.when(is_ctx)
    def _():
        for g in groups:
            for r in range(4):
                st_ref[0, 0, 0, g * 4 + r] = Svs[g][:, r * 64:(r + 1) * 64]


def _ssd_scan(proj, pc, dtp, conv_w, conv_b, dtb, alog, dskip, norm_w, h0, yf, st, layer, geo, rev):
    G, gp = SSM_GROUPS, SSD_GP
    nu = G // gp
    assert nu == 1
    d = 1 if rev else 0
    xsel = np.zeros((G, 128, 256), np.float32)
    for g in range(G):
        for r in range(4):
            xsel[g, d * 32 + 4 * g + r, r * 64:(r + 1) * 64] = 1.0
    cw, zw = 512 * gp, 256 * gp
    head_specs = [_param_spec(1, SMALL, lambda u: 0), _param_spec(1, SMALL, lambda u: 0),
                  pl.BlockSpec((gp, 128, 256), lambda u, i: (u, 0, 0)),
                  _h0_spec(geo, rev, gp)]
    head_args = [dtb, alog, jnp.asarray(xsel, bf16), h0]
    if rev:
        in_specs = [_tok_spec(cw, lambda u: u, geo, rev), _tok_spec(SMALL, lambda u: 0, geo, rev), *head_specs,
                    _param_spec(1, zw, lambda u: u), _tok_spec(zw, lambda u: 2 * nu + u, geo, rev),
                    _param_spec(1, zw, lambda u: u), _tok_spec(zw, lambda u: u, geo, rev)]
        args = [pc, dtp, *head_args, dskip, proj, norm_w, yf]
        outs = [(2048, bf16, zw)]
    else:
        in_specs = [_tok_spec(cw, lambda u: u, geo, rev), *_halo_specs(cw, lambda u: u, geo, rev),
                    _tok_spec(SMALL, lambda u: 0, geo, rev),
                    _param_spec(3, cw, lambda u: u), _param_spec(1, cw, lambda u: u), *head_specs]
        args = [proj, proj, proj, dtp, conv_w, conv_b, *head_args]
        outs = [(2048, f32, zw), (4096, bf16, cw)]
    kern = functools.partial(_ssd_kernel, rev=rev, geo=geo, gp=gp)
    return _scan_call(kern, "ssd_bwd" if rev else "ssd_fwd",
                      nu, in_specs, args, outs, st, (4 * gp, SSM_STATE, SSM_HEADDIM), layer, geo, rev, gp)


def _gla_kernel(*refs, rev, geo, nh):
    qkv_ref, lr_ref, w2_ref, gb_ref, h0_ref = refs[:5]
    skip = 1
    if rev:
        r_ref, nw_ref, yf_ref = refs[5:8]
        y_ref, st_ref, S = refs[8 + skip:]
    else:
        y_ref, st_ref, S = refs[5 + skip:]
    is_ctx, first, last, start = _flags(geo, rev)
    _init_state(S, h0_ref, is_ctx, start)
    d = 1 if rev else 0
    dk, dv = 128, 256
    heads = range(nh)
    K = nh * dk

    q_all = qkv_ref[:, 0:K].astype(f32) * dk ** -0.5
    k_all = qkv_ref[:, K:2 * K].astype(f32)
    vb_all = qkv_ref[:, 2 * K:2 * K + nh * dv]
    lr = lr_ref[...][:, d * GLA_RANK:(d + 1) * GLA_RANK]
    logit = _mm_hi(lr, w2_ref[0]) + gb_ref[0]
    lg = jax.nn.log_sigmoid(logit) * (1.0 / GLA_NORMALIZER)
    tri = _tri_mask(BLK, rev, chunk=CHUNK)
    cs = _mm_x(_as_bf16(tri), lg)
    cs_t = cs.T
    qgb = (q_all * jnp.exp(cs)).astype(bf16)
    kgb = (k_all * jnp.exp(-cs)).astype(bf16)

    ys = []
    for h in heads:
        sc = jnp.where(tri, _mm(qgb[:, h * dk:(h + 1) * dk], kgb[:, h * dk:(h + 1) * dk], NT), 0.0)
        ys.append(_mm(sc.astype(bf16), vb_all[:, h * dv:(h + 1) * dv]))
    Svs = [S[h] for h in heads]
    nch = BLK // CHUNK
    yin = [[None] * nch for _ in heads]
    for c in (range(nch - 1, -1, -1) if rev else range(nch)):
        lo = c * CHUNK
        li = lo if rev else lo + CHUNK - 1
        kw_all = (k_all[lo:lo + CHUNK] * jnp.exp(cs[li:li + 1, :] - cs[lo:lo + CHUNK])).astype(bf16)
        for h in heads:
            yin[h][c] = _mm(qgb[lo:lo + CHUNK, h * dk:(h + 1) * dk], Svs[h].astype(bf16))
        for h in heads:
            Svs[h] = (Svs[h] * jnp.exp(cs_t[h * dk:(h + 1) * dk, li:li + 1])
                      + _mm(kw_all[:, h * dk:(h + 1) * dk], vb_all[lo:lo + CHUNK, h * dv:(h + 1) * dv], TN))

    for h in heads:
        S[h] = Svs[h]
        y = ys[h] + jnp.concatenate(yin[h], axis=0)
        cols = slice(h * dv, (h + 1) * dv)
        if rev:
            y = _norm_gate(yf_ref[:, cols] + y, r_ref[:, cols].astype(f32), nw_ref[...], gate_first=False)
        y_ref[:, cols] = y.astype(y_ref.dtype)

    @pl.when(is_ctx)
    def _():
        for h in heads:
            st_ref[0, 0, 0, h] = Svs[h]


def _gla_scan(proj, lrp, w2, gb, norm_w, h0, yf, st, layer, geo, rev):
    H = 4
    d = 1 if rev else 0
    in_specs = [_tok_spec(2048, lambda u: 0, geo, rev), _tok_spec(SMALL, lambda u: 0, geo, rev),
                pl.BlockSpec((1, GLA_RANK, 512), lambda u, i: (d, 0, 0)),
                pl.BlockSpec((1, 1, 512), lambda u, i: (d, 0, 0)),
                _h0_spec(geo, rev, H)]
    args = [proj, lrp, w2, gb, h0]
    if rev:
        in_specs += [_tok_spec(1024, lambda u: 2, geo, rev), _param_spec(1, 256, lambda u: 0),
                     _tok_spec(1024, lambda u: 0, geo, rev)]
        args += [proj, norm_w, yf]
    kern = functools.partial(_gla_kernel, rev=rev, geo=geo, nh=H)
    return _scan_call(kern, "gla_bwd" if rev else "gla_fwd",
                      1, in_specs, args, [(1024, bf16 if rev else f32, 1024)], st, (H, 128, 256), layer, geo, rev, H)


def _unit_tri_inverse(ms, lev):
    eye = jnp.where(lev == -1, 1.0, 0.0)
    dinv = [eye - jnp.where(lev == 0, m, 0.0) for m in ms]
    for b in range(1, 6):
        ts = [_mm(jnp.where(lev == b, m, 0.0).astype(bf16), d.astype(bf16)).astype(bf16)
              for m, d in zip(ms, dinv)]
        dinv = [d - _mm(d.astype(bf16), t) for d, t in zip(dinv, ts)]
    return dinv


def _gdn_kernel(*refs, rev, geo, hp):
    skip = 1
    if rev:
        (pc_ref, ab_ref, dtb_ref, alog_ref, xab_ref, lev_ref, h0_ref, z_ref, nw_ref, yf_ref) = refs[:10]
        y_ref, st_ref, S = refs[10 + skip:]
    else:
        (qkv_c, qkv_p, qkv_n, cw_ref, ab_ref, dtb_ref, alog_ref, xab_ref, lev_ref, h0_ref) = refs[:10]
        y_ref, pc_ref, st_ref, S = refs[10 + skip:]
    is_ctx, first, last, start = _flags(geo, rev)
    _init_state(S, h0_ref, is_ctx, start)
    dk, dv = 128, 256
    heads = range(hp)

    if rev:
        qkv = pc_ref[...].astype(f32)
    else:
        qkv = _conv3_silu(qkv_c, qkv_p, qkv_n, cw_ref, None, first, last)

    ab = ab_ref[...]
    gall = -jnp.exp(alog_ref[...]) * jax.nn.softplus(ab + dtb_ref[...])
    lane = lax.broadcasted_iota(jnp.int32, ab.shape, 1)
    sel = _mm_x(jnp.where(lane < 16, gall, jax.nn.sigmoid(ab)), xab_ref[0])
    lev = lev_ref[...]
    r_i = lax.broadcasted_iota(jnp.int32, (BLK, BLK), 0)
    c_i = lax.broadcasted_iota(jnp.int32, (BLK, BLK), 1)
    in_chunk = lev < 6
    incl = jnp.logical_and(in_chunk, (c_i >= r_i) if rev else (c_i <= r_i))
    strict = jnp.logical_and(incl, lev >= 0)
    cs_all = _mm_x(_as_bf16(incl), sel)
    cs_rows = cs_all.T

    qs, ks, kbfs, kbs, vbs, css, lmasks, ms = [], [], [], [], [], [], [], []
    for hh in heads:
        q = qkv[:, hh * dk:(hh + 1) * dk]
        k = qkv[:, (hp + hh) * dk:(hp + hh + 1) * dk]
        v = qkv[:, 2 * hp * dk + hh * dv:2 * hp * dk + (hh + 1) * dv]
        if not rev:
            q = q * lax.rsqrt(jnp.sum(q * q, axis=-1, keepdims=True) + 1e-6) * dk ** -0.5
            k = k * lax.rsqrt(jnp.sum(k * k, axis=-1, keepdims=True) + 1e-6)
            pc_ref[:, hh * dk:(hh + 1) * dk] = q.astype(bf16)
            pc_ref[:, (hp + hh) * dk:(hp + hh + 1) * dk] = k.astype(bf16)
            pc_ref[:, 2 * hp * dk + hh * dv:2 * hp * dk + (hh + 1) * dv] = v.astype(bf16)
        beta = sel[:, 2 * hh + 1:2 * hh + 2]
        cs = cs_all[:, 2 * hh:2 * hh + 1]
        lmask = jnp.where(incl, jnp.exp(cs - cs_rows[2 * hh:2 * hh + 1, :]), 0.0)
        kb = k * beta
        kbf = k.astype(bf16)
        qs.append(q)
        ks.append(k)
        kbfs.append(kbf)
        kbs.append(kb)
        vbs.append((v * beta).astype(bf16))
        css.append(cs)
        lmasks.append(lmask)
        ms.append(jnp.where(strict, _mm(kb.astype(bf16), kbf, NT) * lmask, 0.0))
    tbs = [t.astype(bf16) for t in _unit_tri_inverse(ms, lev)]
    ecs = [jnp.exp(cs) for cs in css]
    us = [_mm(tbs[h], vbs[h]) for h in heads]
    wks = [_mm(tbs[h], (kbs[h] * ecs[h]).astype(bf16)).astype(bf16) for h in heads]
    a_ins = [(_mm(qs[h].astype(bf16), kbfs[h], NT) * lmasks[h]).astype(bf16) for h in heads]
    qds = [(qs[h] * ecs[h]).astype(bf16) for h in heads]

    Svs = [S[h] for h in heads]
    nch = BLK // CHUNK
    outs = [[None] * nch for _ in heads]
    for c in (range(nch - 1, -1, -1) if rev else range(nch)):
        lo = c * CHUNK
        li = lo if rev else lo + CHUNK - 1
        Sbs = [Sv.astype(bf16) for Sv in Svs]
        v_news = [(us[h][lo:lo + CHUNK] - _mm(wks[h][lo:lo + CHUNK], Sbs[h])).astype(bf16) for h in heads]
        for h in heads:
            outs[h][c] = (_mm(qds[h][lo:lo + CHUNK], Sbs[h])
                          + _mm(a_ins[h][lo:lo + CHUNK, lo:lo + CHUNK], v_news[h]))
        for h in heads:
            cl = css[h][li:li + 1, :]
            kd = ks[h][lo:lo + CHUNK] * jnp.exp(cl - css[h][lo:lo + CHUNK])
            Svs[h] = Svs[h] * jnp.exp(cl) + _mm(kd.astype(bf16), v_news[h], TN)

    for hh in heads:
        S[hh] = Svs[hh]
        o = jnp.concatenate(outs[hh], axis=0)
        cols = slice(hh * dv, (hh + 1) * dv)
        if rev:
            o = _norm_gate(yf_ref[:, cols] + o, z_ref[:, cols].astype(f32), nw_ref[...], gate_first=False)
        y_ref[:, cols] = o.astype(y_ref.dtype)

    @pl.when(is_ctx)
    def _():
        for hh in heads:
            st_ref[0, 0, 0, hh] = Svs[hh]


def _gdn_scan(proj, pc, abp, conv_w, dtb, alog, norm_w, h0, yf, st, layer, geo, rev):
    H, hp = 8, GDN_HP
    nu = H // hp
    d = 1 if rev else 0
    xab = np.zeros((nu, 128, 128), np.float32)
    for h in range(H):
        xab[h // hp, d * H + h, 2 * (h % hp)] = 1.0
        xab[h // hp, 2 * H + d * H + h, 2 * (h % hp) + 1] = 1.0
    idx = np.arange(BLK)
    x = idx[:, None] ^ idx[None, :]
    lev = np.where(x == 0, -1, np.floor(np.log2(np.maximum(x, 1)))).astype(np.int32)
    cw, vw = 512 * hp, 256 * hp
    head_specs = [_tok_spec(SMALL, lambda u: 0, geo, rev),
                  _param_spec(1, SMALL, lambda u: 0), _param_spec(1, SMALL, lambda u: 0),
                  pl.BlockSpec((1, 128, 128), lambda u, i: (u, 0, 0)),
                  pl.BlockSpec((BLK, BLK), lambda u, i: (0, 0)),
                  _h0_spec(geo, rev, hp)]
    head_args = [abp, dtb, alog, jnp.asarray(xab, bf16), jnp.asarray(lev), h0]
    if rev:
        in_specs = [_tok_spec(cw, lambda u: u, geo, rev), *head_specs,
                    _tok_spec(vw, lambda u: 2 * nu + u, geo, rev), _param_spec(1, 256, lambda u: 0),
                    _tok_spec(vw, lambda u: u, geo, rev)]
        args = [pc, *head_args, proj, norm_w, yf]
        outs = [(2048, bf16, vw)]
    else:
        in_specs = [_tok_spec(cw, lambda u: u, geo, rev), *_halo_specs(cw, lambda u: u, geo, rev),
                    _param_spec(3, cw, lambda u: u), *head_specs]
        args = [proj, proj, proj, conv_w, *head_args]
        outs = [(2048, f32, vw), (4096, bf16, cw)]
    kern = functools.partial(_gdn_kernel, rev=rev, geo=geo, hp=hp)
    return _scan_call(kern, "gdn_bwd" if rev else "gdn_fwd",
                      nu, in_specs, args, outs, st, (hp, 128, 256), layer, geo, rev, hp)


def _pad_cols(w, n=SMALL):
    return jnp.pad(w.astype(f32), ((0, 0), (0, n - w.shape[1])))


def _pad_row(v, n=SMALL):
    v = v.reshape(1, -1).astype(f32)
    return jnp.pad(v, ((0, 0), (0, n - v.shape[1])))


def _ssd_perm():
    perm = []
    for g in range(SSM_GROUPS):
        perm += list(range(g * 256, (g + 1) * 256))
        perm += list(range(2048 + g * 128, 2048 + (g + 1) * 128))
        perm += list(range(3072 + g * 128, 3072 + (g + 1) * 128))
    return np.asarray(perm, np.int32)


def _ssd_dst_blocks():
    dst = [32 + i for i in range(16)]
    dst += [4 * (i // 2) + i % 2 for i in range(16)]
    dst += [4 * g + 2 for g in range(SSM_GROUPS)]
    dst += [4 * g + 3 for g in range(SSM_GROUPS)]
    return tuple(dst)


def _gdn_dst_blocks(hp):
    inv = np.argsort(_gdn_perm(hp)[::128] // 128)
    return tuple(int(b) for b in inv) + tuple(range(32, 48))


def _gdn_perm(hp):
    perm = []
    for u in range(8 // hp):
        perm += list(range(u * hp * 128, (u + 1) * hp * 128))
        perm += list(range(1024 + u * hp * 128, 1024 + (u + 1) * hp * 128))
        perm += list(range(2048 + u * hp * 256, 2048 + (u + 1) * hp * 256))
    return np.asarray(perm, np.int32)


def kernel(x_prompt, x_sample, state_ssm, state_gla, state_gdn, c, c_ctx, norm_mix_w, norm_ffn_w, ada_w, ada_b, ffn_in_w, ffn_conv_w, ffn_conv_b, ffn_out_w, ssm_in_w, ssm_conv_w, ssm_conv_b, ssm_dt_bias, ssm_a_log, ssm_d, ssm_norm_w, ssm_out_w, gla_in_w, gla_gate_w1, gla_gate_w2, gla_gate_b, gla_norm_w, gla_out_w, gdn_in_w, gdn_conv_w, gdn_dt_bias, gdn_a_log, gdn_norm_w, gdn_out_w, final_norm_w):
    Bc, SEQ, D = x_prompt.shape
    Bs, LS, _ = x_sample.shape
    geo = Geo(D=D, Tc=Bc * SEQ, Ts=Bs * LS, SEQ=SEQ, LS=LS, Bc=Bc, Bs=Bs)
    assert SEQ % BLK == 0 and LS % BLK == 0 and Bs < 8
    depth = ada_w.shape[0]
    G, P, N = SSM_GROUPS, SSM_HEADDIM, SSM_STATE

    x = (x_prompt.reshape(geo.Tc, D), x_sample.reshape(geo.Ts, D))
    cond8 = jnp.zeros((8, D), f32).at[0].set(c_ctx).at[1:1 + Bs].set(c)
    mod_all = _ada_table(cond8, ada_w, ada_b)[:, :1 + Bs].reshape(depth, 1 + Bs, 1, 6 * D)

    n_ssm = ssm_out_w.shape[1] + 4096
    n_gdn = 6144
    wb = dict(ssm_in=jnp.swapaxes(ssm_in_w, 1, 2).astype(bf16), ssm_out=ssm_out_w.astype(bf16),
              gla_in=gla_in_w.astype(bf16), gla_out=gla_out_w.astype(bf16),
              gdn_in=jnp.swapaxes(gdn_in_w, 1, 2).astype(bf16), gdn_out=gdn_out_w.astype(bf16),
              ffn_in=ffn_in_w.astype(bf16), ffn_out=ffn_out_w.astype(bf16))
    st_ssm = jnp.zeros((Bc, ssm_in_w.shape[0], 2, G * 4, N, P), f32)
    st_gla = jnp.zeros((Bc, gla_in_w.shape[0], 2) + state_gla.shape[3:], f32)
    st_gdn = jnp.zeros((Bc, gdn_in_w.shape[0], 2) + state_gdn.shape[3:], f32)
    for i in range(depth):
        mod = mod_all[i]
        kind, j = i % 3, i // 3
        if kind == 0:
            di = ssm_out_w.shape[1]
            perm = _ssd_perm()
            proj, dtp = _in_proj(x, norm_mix_w[i], mod, 0, wb['ssm_in'], j, n_ssm,
                                 _pad_cols(ssm_in_w[j, :, n_ssm:]), geo, dst_blk=_ssd_dst_blocks(), wt=True)
            dtb, alog = _pad_row(ssm_dt_bias[j]), _pad_row(ssm_a_log[j])
            dskip = jnp.repeat(ssm_d[j], P).reshape(1, di)
            nw = ssm_norm_w[j].reshape(1, di)
            cw, cb = ssm_conv_w[j][:, perm], ssm_conv_b[j][perm].reshape(1, -1)
            h0 = state_ssm[:, j].reshape(Bs, 2, G, 4, N, P).transpose(1, 0, 2, 4, 3, 5).reshape(2, Bs, G, N, 4 * P)
            yf, pc, st_ssm = _ssd_scan(proj, None, dtp, cw, cb, dtb, alog, None, None, h0[0], None,
                                       st_ssm, j, geo, False)
            u, st_ssm = _ssd_scan(proj, pc, dtp, None, None, dtb, alog, dskip, nw, h0[1], yf,
                                  st_ssm, j, geo, True)
            w_out = wb['ssm_out']
        elif kind == 1:
            w_small = _pad_cols(jnp.concatenate([gla_gate_w1[j, 0], gla_gate_w1[j, 1]], axis=1))
            proj, lrp = _in_proj(x, norm_mix_w[i], mod, 0, wb['gla_in'], j, gla_in_w.shape[2], w_small, geo)
            w2 = gla_gate_w2[j]
            gb = gla_gate_b[j].reshape(2, 1, -1)
            nw = gla_norm_w[j].reshape(1, -1)
            yf, st_gla = _gla_scan(proj, lrp, w2, gb, None, state_gla[:, j, 0], None, st_gla, j, geo, False)
            u, st_gla = _gla_scan(proj, lrp, w2, gb, nw, state_gla[:, j, 1], yf, st_gla, j, geo, True)
            w_out = wb['gla_out']
        else:
            perm = _gdn_perm(GDN_HP)
            proj, abp = _in_proj(x, norm_mix_w[i], mod, 0, wb['gdn_in'], j, n_gdn,
                                 _pad_cols(gdn_in_w[j, :, n_gdn:]), geo, dst_blk=_gdn_dst_blocks(GDN_HP), wt=True)
            dtb, alog = _pad_row(gdn_dt_bias[j]), _pad_row(gdn_a_log[j])
            nw = gdn_norm_w[j].reshape(1, -1)
            cw = gdn_conv_w[j][:, perm]
            yf, pc, st_gdn = _gdn_scan(proj, None, abp, cw, dtb, alog, None, state_gdn[:, j, 0], None,
                                       st_gdn, j, geo, False)
            u, st_gdn = _gdn_scan(proj, pc, abp, None, dtb, alog, nw, state_gdn[:, j, 1], yf,
                                  st_gdn, j, geo, True)
            w_out = wb['gdn_out']
        x = _out_proj(u, w_out, j, x, mod, 2, geo)

        proj, _ = _in_proj(x, norm_ffn_w[i], mod, 3, wb['ffn_in'], i, ffn_in_w.shape[2], None, geo)
        x = _out_proj(proj, wb['ffn_out'], i, x, mod, 5, geo,
                      final_w=final_norm_w if i == depth - 1 else None,
                      conv=(ffn_conv_w[i].reshape(9, -1), ffn_conv_b[i]))

    y_ctx, y_lat = x
    return (y_ctx.reshape(Bc, SEQ, D), y_lat.reshape(Bs, LS, D), st_ssm, st_gla, st_gdn)
```
